```python
import math
import jax, jax.numpy as jnp
from jax import lax
import numpy as np

D_MODEL = 1024
BATCH = 16
SEQ = 4096
DEPTH = 1

MIX_WIDTH = D_MODEL
ATTN_HEAD_DIM = 64
ATTN_WIDTH = MIX_WIDTH // 2
ATTN_HEADS = ATTN_WIDTH // ATTN_HEAD_DIM
RWKV_HEAD_DIM = 64
RWKV_WIDTH = MIX_WIDTH - ATTN_WIDTH
RWKV_HEADS = RWKV_WIDTH // RWKV_HEAD_DIM
DILATED_PATTERNS = ((128, 1), (512, 4), (2048, 16))
ATTN_BLOCK = 128
DECAY_LORA = 64
ICLR_LORA = 64
GATE_LORA = 128
N_EXPERTS = 32
TOP_K = 4
D_FF = D_MODEL
SWIGLU_LIMIT = 7.0
SWIGLU_ALPHA = 1.702
MOE_BLOCK = 256
RMS_EPS = 1e-5
GN_EPS = 64e-5
SHIFT_WIDTH = 3 * RWKV_WIDTH + DECAY_LORA + ICLR_LORA + GATE_LORA
IN_WIDTH = 3 * ATTN_WIDTH + SHIFT_WIDTH

kernel_name = "hybrid_dilated_attn_rwkv7_moe"

F32 = jnp.float32


def rmsnorm(x, g):
    x32 = x.astype(F32)
    y = x32 * lax.rsqrt(jnp.mean(x32 * x32, axis=-1, keepdims=True) + RMS_EPS)
    return (y * g.astype(F32)).astype(x.dtype)


def token_shift(z):
    return jnp.pad(z[:, :-1], ((0, 0), (1, 0), (0, 0)))


def alibi_slopes(n_heads):
    return jnp.exp2(-8.0 / n_heads * jnp.arange(1, n_heads + 1, dtype=F32))


def dilated_window(q, k, v, slopes, window, dil):
    B, S, H, Dh = q.shape
    steps = window // dil
    C = ATTN_BLOCK
    n_sub = -(-S // dil)
    nb = -(-n_sub // C)
    Sp = nb * C * dil
    pad = ((0, 0), (0, Sp - S), (0, 0), (0, 0))

    def to_blocks(t):
        t = jnp.pad(t, pad).reshape(B, nb, C, dil, H, Dh)
        return t.transpose(0, 3, 4, 1, 2, 5)

    qb, kb, vb = to_blocks(q), to_blocks(k), to_blocks(v)
    shift = lambda t: jnp.pad(t[:, :, :, :-1], ((0, 0), (0, 0), (0, 0), (1, 0), (0, 0), (0, 0)))
    kk = jnp.concatenate([shift(kb), kb], axis=4)
    vv = jnp.concatenate([shift(vb), vb], axis=4)

    scale = 1.0 / math.sqrt(Dh)
    s = jnp.einsum('brhnqe,brhnke->brhnqk', qb, kk, preferred_element_type=F32) * scale
    diff = (jnp.arange(C)[:, None] + C) - jnp.arange(2 * C)[None, :]
    band = (diff >= 0) & (diff <= steps)
    first_ok = (jnp.arange(nb)[:, None, None] > 0) | (jnp.arange(2 * C)[None, None, :] >= C)
    valid = band[None] & first_ok
    bias = -slopes[:, None, None] * (dil * diff).astype(F32)[None]
    s = s + bias[None, None, :, None]
    s = jnp.where(valid[None, None, None], s, -jnp.inf)
    m = jnp.max(s, axis=-1, keepdims=True)
    p = jnp.exp(s - m)
    den = jnp.sum(p, axis=-1, keepdims=True)
    o = jnp.einsum('brhnqk,brhnke->brhnqe', p, vv.astype(F32)) / den
    lse = (m + jnp.log(den))[..., 0]
    o = o.transpose(0, 3, 4, 1, 2, 5).reshape(B, Sp, H, Dh)[:, :S]
    lse = lse.transpose(0, 3, 4, 1, 2).reshape(B, Sp, H)[:, :S]
    return o, lse


def dilated_attention(q, k, v):
    slopes = alibi_slopes(q.shape[2])
    outs, lses = [], []
    for window, dil in DILATED_PATTERNS:
        o, l = dilated_window(q, k, v, slopes, window, dil)
        outs.append(o)
        lses.append(l)
    wts = jax.nn.softmax(jnp.stack(lses, 0), axis=0)
    return jnp.sum(wts[..., None] * jnp.stack(outs, 0), axis=0)


def rwkv7_mix(r, k, v, wd, ad, gd, decay_w0, w_decay_up, iclr_a0, w_iclr_up, w_gate_lr_up,
              k_k, k_a, r_k, gn_w, gn_b):
    B, S, W = r.shape
    H, N = RWKV_HEADS, RWKV_HEAD_DIM
    w = -jax.nn.softplus(-(decay_w0 + jnp.tanh(wd) @ w_decay_up)) - 0.5
    decay = jnp.exp(-jnp.exp(w))
    a = jax.nn.sigmoid(iclr_a0 + ad @ w_iclr_up)
    g = jax.nn.sigmoid(gd) @ w_gate_lr_up
    heads = lambda t: t.reshape(B, S, H, N)
    kk = heads(k * k_k)
    kk = kk / jnp.maximum(jnp.sqrt(jnp.sum(kk * kk, -1, keepdims=True)), 1e-12)
    k = k * (1.0 + (a - 1.0) * k_a)
    rh, wh, kh, vh, ah = heads(r), heads(decay), heads(k), heads(v), heads(a)

    def step(state, inp):
        r_t, w_t, k_t, v_t, kk_t, a_t = inp
        sa = jnp.einsum('bhij,bhj->bhi', state, kk_t)
        state = (state * w_t[:, :, None, :]
                 - sa[..., None] * (kk_t * a_t)[:, :, None, :]
                 + v_t[..., None] * k_t[:, :, None, :])
        return state, jnp.einsum('bhij,bhj->bhi', state, r_t)

    xs = tuple(jnp.moveaxis(t, 1, 0) for t in (rh, wh, kh, vh, kk, ah))
    state0 = jnp.zeros((B, H, N, N), F32)
    _, ys = lax.scan(step, state0, xs)
    y = jnp.moveaxis(ys, 0, 1)
    mu = jnp.mean(y, -1, keepdims=True)
    var = jnp.mean(jnp.square(y - mu), -1, keepdims=True)
    yn = ((y - mu) * lax.rsqrt(var + GN_EPS)).reshape(B, S, W) * gn_w + gn_b
    bonus = (jnp.sum(rh * kh * r_k, -1, keepdims=True) * vh).reshape(B, S, W)
    return (yn + bonus) * g


def moe_ffn(xn, w_router, b_router, w_gate_up, b_gate_up, w_down, b_down):
    Bsz, S, D = xn.shape
    n_tok = Bsz * S
    xf = xn.reshape(n_tok, D)
    logits = jnp.matmul(xf, w_router, preferred_element_type=F32) + b_router.astype(F32)
    top_val, top_idx = lax.top_k(logits, TOP_K)
    gates = jax.nn.softmax(top_val, axis=-1)
    n_assign = n_tok * TOP_K
    e_flat = top_idx.reshape(n_assign).astype(jnp.int32)
    g_flat = gates.reshape(n_assign)
    t_flat = jnp.repeat(jnp.arange(n_tok, dtype=jnp.int32), TOP_K)
    order = jnp.argsort(e_flat)
    e_s, t_s, g_s = e_flat[order], t_flat[order], g_flat[order]
    counts = jnp.bincount(e_flat, length=N_EXPERTS)
    padded = (counts + MOE_BLOCK - 1) // MOE_BLOCK * MOE_BLOCK
    start = jnp.cumsum(counts) - counts
    pend = jnp.cumsum(padded)
    pstart = pend - padded
    dest = pstart[e_s] + (jnp.arange(n_assign, dtype=jnp.int32) - start[e_s])
    n_blocks = (n_assign + N_EXPERTS * (MOE_BLOCK - 1) + MOE_BLOCK - 1) // MOE_BLOCK
    n_rows = n_blocks * MOE_BLOCK
    row_tok = jnp.full((n_rows,), n_tok, jnp.int32).at[dest].set(t_s)
    row_gate = jnp.zeros((n_rows,), F32).at[dest].set(g_s)
    blk_expert = jnp.minimum(
        jnp.searchsorted(pend, jnp.arange(n_blocks, dtype=pend.dtype) * MOE_BLOCK, side='right'),
        N_EXPERTS - 1)
    xpad = jnp.concatenate([xf, jnp.zeros((1, D), xf.dtype)], axis=0)

    def expert_block(args):
        tok, gate, e = args
        hb = xpad[tok]
        gu = hb @ w_gate_up[e] + b_gate_up[e]
        g_lin = jnp.minimum(gu[:, :D_FF], SWIGLU_LIMIT)
        u_lin = jnp.clip(gu[:, D_FF:], -SWIGLU_LIMIT, SWIGLU_LIMIT)
        act = (u_lin + 1.0) * (g_lin * jax.nn.sigmoid(SWIGLU_ALPHA * g_lin))
        out = act @ w_down[e] + b_down[e]
        return out.astype(F32) * gate[:, None]

    outs = lax.map(expert_block, (row_tok.reshape(n_blocks, MOE_BLOCK),
                                  row_gate.reshape(n_blocks, MOE_BLOCK), blk_expert))
    y = jnp.zeros((n_tok + 1, D), F32).at[row_tok].add(outs.reshape(n_rows, D))
    return y[:n_tok].reshape(Bsz, S, D).astype(xn.dtype)


def setup_inputs(seed: int = 0) -> dict:
    key = jax.random.key(seed)
    ks = jax.random.split(key, 24)
    L = DEPTH
    nrm = lambda k, shape, scale: jax.random.normal(k, shape, F32) * scale
    return {
        "x": nrm(ks[0], (BATCH, SEQ, D_MODEL), 1.0),
        "norm_mix_g": 1.0 + nrm(ks[1], (L, D_MODEL), 0.02),
        "w_in": nrm(ks[2], (L, D_MODEL, IN_WIDTH), D_MODEL ** -0.5),
        "shift_mu": jax.random.uniform(ks[3], (L, SHIFT_WIDTH), F32, 0.2, 0.8),
        "decay_w0": -0.6 + nrm(ks[4], (L, RWKV_WIDTH), 0.3),
        "w_decay_up": nrm(ks[5], (L, DECAY_LORA, RWKV_WIDTH), 0.1),
        "iclr_a0": nrm(ks[6], (L, RWKV_WIDTH), 0.1),
        "w_iclr_up": nrm(ks[7], (L, ICLR_LORA, RWKV_WIDTH), 0.1),
        "w_gate_lr_up": nrm(ks[8], (L, GATE_LORA, RWKV_WIDTH), GATE_LORA ** -0.5),
        "k_k": 0.85 + nrm(ks[9], (L, RWKV_WIDTH), 0.02),
        "k_a": 1.0 + nrm(ks[10], (L, RWKV_WIDTH), 0.02),
        "r_k": nrm(ks[11], (L, RWKV_HEADS, RWKV_HEAD_DIM), 0.1),
        "gn_w": 1.0 + nrm(ks[12], (L, RWKV_WIDTH), 0.02),
        "gn_b": nrm(ks[13], (L, RWKV_WIDTH), 0.01),
        "w_out": nrm(ks[14], (L, MIX_WIDTH, D_MODEL), MIX_WIDTH ** -0.5),
        "norm_ffn_g": 1.0 + nrm(ks[15], (L, D_MODEL), 0.02),
        "w_router": nrm(ks[16], (L, D_MODEL, N_EXPERTS), D_MODEL ** -0.5),
        "b_router": nrm(ks[17], (L, N_EXPERTS), 0.01),
        "w_gate_up": nrm(ks[18], (L, N_EXPERTS, D_MODEL, 2 * D_FF), D_MODEL ** -0.5),
        "b_gate_up": nrm(ks[19], (L, N_EXPERTS, 2 * D_FF), 0.01),
        "w_down": nrm(ks[20], (L, N_EXPERTS, D_FF, D_MODEL), D_FF ** -0.5),
        "b_down": nrm(ks[21], (L, N_EXPERTS, D_MODEL), 0.01),
        "norm_final_g": 1.0 + nrm(ks[22], (D_MODEL,), 0.02),
    }


def reference(x, norm_mix_g, w_in, shift_mu, decay_w0, w_decay_up, iclr_a0, w_iclr_up,
              w_gate_lr_up, k_k, k_a, r_k, gn_w, gn_b, w_out, norm_ffn_g, w_router, b_router,
              w_gate_up, b_gate_up, w_down, b_down, norm_final_g):
    B, S, _ = x.shape
    h = x
    for l in range(DEPTH):
        xn = rmsnorm(h, norm_mix_g[l])
        proj = xn @ w_in[l]
        qa, ka, va = jnp.split(proj[..., :3 * ATTN_WIDTH], 3, axis=-1)
        rest = proj[..., 3 * ATTN_WIDTH:].astype(F32)
        rest = rest + (token_shift(rest) - rest) * shift_mu[l]
        o1 = RWKV_WIDTH
        r_, k_, v_ = rest[..., :o1], rest[..., o1:2 * o1], rest[..., 2 * o1:3 * o1]
        wd = rest[..., 3 * o1:3 * o1 + DECAY_LORA]
        ad = rest[..., 3 * o1 + DECAY_LORA:3 * o1 + DECAY_LORA + ICLR_LORA]
        gd = rest[..., 3 * o1 + DECAY_LORA + ICLR_LORA:]

        hd = lambda t: t.reshape(B, S, ATTN_HEADS, ATTN_HEAD_DIM)
        attn = dilated_attention(hd(qa), hd(ka), hd(va)).reshape(B, S, ATTN_WIDTH)
        rw = rwkv7_mix(r_, k_, v_, wd, ad, gd,
                       decay_w0[l].astype(F32), w_decay_up[l].astype(F32), iclr_a0[l].astype(F32),
                       w_iclr_up[l].astype(F32), w_gate_lr_up[l].astype(F32), k_k[l].astype(F32),
                       k_a[l].astype(F32), r_k[l].astype(F32), gn_w[l].astype(F32), gn_b[l].astype(F32))
        mixed = jnp.concatenate([attn.astype(h.dtype), rw.astype(h.dtype)], axis=-1) @ w_out[l]
        h = h + mixed
        h = h + moe_ffn(rmsnorm(h, norm_ffn_g[l]), w_router[l], b_router[l], w_gate_up[l],
                        b_gate_up[l], w_down[l], b_down[l])
    return rmsnorm(h, norm_final_g)
```

```python
import functools
import math

import jax
import jax.numpy as jnp
from jax import lax
from jax.experimental import pallas as pl
from jax.experimental.pallas import tpu as pltpu

F32 = jnp.float32
BF16 = jnp.bfloat16
I32 = jnp.int32

D_MODEL = 1024
HEAD_DIM = 64
ATTN_WIDTH = 512
ATTN_HEADS = ATTN_WIDTH // HEAD_DIM
RWKV_WIDTH = 512
RWKV_HEADS = RWKV_WIDTH // HEAD_DIM
DILATED_PATTERNS = ((128, 1), (512, 4), (2048, 16))
ATTN_BLOCK = 128
DECAY_LORA = 64
ICLR_LORA = 64
GATE_LORA = 128
SHIFT_WIDTH = 3 * RWKV_WIDTH + DECAY_LORA + ICLR_LORA + GATE_LORA
N_EXPERTS = 32
TOP_K = 4
D_FF = D_MODEL
SWIGLU_LIMIT = 7.0
SWIGLU_ALPHA = 1.702
RMS_EPS = 1e-5
GN_EPS = 64e-5

LANES = 128
CHUNK = 64
MASK_VALUE = -1e30
VMEM_LIMIT = 56 * 1024 * 1024


def _params(semantics, vmem=VMEM_LIMIT):
    return pltpu.CompilerParams(dimension_semantics=semantics, vmem_limit_bytes=vmem)


def _dot(a, b):
    return jnp.dot(a.astype(BF16), b.astype(BF16), preferred_element_type=F32)


def _dot_nt(a, b):
    return lax.dot_general(a.astype(BF16), b.astype(BF16), (((1,), (1,)), ((), ())),
                           preferred_element_type=F32)


def _dot_tn(a, b):
    return lax.dot_general(a.astype(BF16), b.astype(BF16), (((0,), (0,)), ((), ())),
                           preferred_element_type=F32)


def _split3(x):
    h = x.astype(BF16)
    r1 = x - h.astype(F32)
    m = r1.astype(BF16)
    l = (r1 - m.astype(F32)).astype(BF16)
    return h, m, l


def _dot_exact_lhs(a_bf16, x):
    h, m, l = _split3(x)
    d = lambda y: jnp.dot(a_bf16, y, preferred_element_type=F32)
    return d(h) + d(m) + d(l)


def _dot_exact_rhs(x, b_bf16):
    h, m, l = _split3(x)
    d = lambda y: jnp.dot(y, b_bf16, preferred_element_type=F32)
    return d(h) + d(m) + d(l)


def _dot_hi(a, b):
    ah = a.astype(BF16)
    al = (a - ah.astype(F32)).astype(BF16)
    bh = b.astype(BF16)
    bl = (b - bh.astype(F32)).astype(BF16)
    d = lambda x, y: jnp.dot(x, y, preferred_element_type=F32)
    return d(ah, bh) + d(ah, bl) + d(al, bh)


def _sigmoid(x):
    return 1.0 / (1.0 + jnp.exp(-x))


def _head_ones(width):
    r = lax.broadcasted_iota(I32, (width, width), 0) // HEAD_DIM
    c = lax.broadcasted_iota(I32, (width, width), 1) // HEAD_DIM
    return jnp.where(r == c, 1.0, 0.0).astype(BF16)


def _inproj_kernel(x_ref, g_ref, wq_ref, wr_ref, qkv_ref, rest_ref):
    x = x_ref[...]
    xn = x * lax.rsqrt(jnp.mean(x * x, axis=-1, keepdims=True) + RMS_EPS) * g_ref[...]
    xb = xn.astype(BF16)
    qkv_ref[...] = jnp.dot(xb, wq_ref[...], preferred_element_type=F32)
    rest_ref[...] = jnp.dot(xb, wr_ref[...], preferred_element_type=F32)


def _in_proj(xf, g, w_in):
    n = xf.shape[0]
    tm = 256
    wq = w_in[:, :3 * ATTN_WIDTH].astype(BF16)
    wr = w_in[:, 3 * ATTN_WIDTH:].astype(BF16)
    return pl.pallas_call(
        _inproj_kernel,
        grid=(n // tm,),
        in_specs=[
            pl.BlockSpec((tm, D_MODEL), lambda i: (i, 0)),
            pl.BlockSpec((1, D_MODEL), lambda i: (0, 0)),
            pl.BlockSpec((D_MODEL, 3 * ATTN_WIDTH), lambda i: (0, 0)),
            pl.BlockSpec((D_MODEL, SHIFT_WIDTH), lambda i: (0, 0)),
        ],
        out_specs=[
            pl.BlockSpec((tm, 3 * ATTN_WIDTH), lambda i: (i, 0)),
            pl.BlockSpec((tm, SHIFT_WIDTH), lambda i: (i, 0)),
        ],
        out_shape=[
            jax.ShapeDtypeStruct((n, 3 * ATTN_WIDTH), F32),
            jax.ShapeDtypeStruct((n, SHIFT_WIDTH), F32),
        ],
        compiler_params=_params(("parallel",)),
        name="in_proj",
    )(xf, g.reshape(1, D_MODEL), wq, wr)


def _attn_kernel(slopes_ref, q_ref, k_ref, v_ref, o_ref, bias_ref, m_acc, d_acc, n_acc, *, seq):
    c = ATTN_BLOCK
    hp = pl.program_id(1)
    lane = lax.broadcasted_iota(I32, (c, LANES), 1)
    lo_half = lane < HEAD_DIM
    qi = lax.broadcasted_iota(I32, (c, 2 * c), 0)
    kj = lax.broadcasted_iota(I32, (c, 2 * c), 1)
    diff = qi + c - kj
    for p, (window, dil) in enumerate(DILATED_PATTERNS):
        steps = window // dil
        valid = (diff >= 0) & (diff <= steps)
        dist = (dil * diff).astype(F32)
        for hh in range(2):
            slope = slopes_ref[hp * 2 + hh]
            bias_ref[p * 2 + hh] = jnp.where(valid, -slope * dist, MASK_VALUE)

    scale = 1.0 / math.sqrt(HEAD_DIM)

    def rows(start, dil):
        if dil == 1:
            return pl.ds(start, c)
        return pl.ds(start, c, stride=dil)

    def block(p, dil, start_cur, start_prev):
        first = start_prev is None
        qf = q_ref[0, rows(start_cur, dil), :] * scale
        kcat = k_ref[0, rows(start_cur, dil), :].astype(BF16)
        vcat = v_ref[0, rows(start_cur, dil), :].astype(BF16)
        if not first:
            kp = k_ref[0, rows(start_prev, dil), :].astype(BF16)
            vp = v_ref[0, rows(start_prev, dil), :].astype(BF16)
            kcat = jnp.concatenate([kp, kcat], axis=0)
            vcat = jnp.concatenate([vp, vcat], axis=0)
        parts = []
        for hh in range(2):
            head_mask = lo_half if hh == 0 else jnp.logical_not(lo_half)
            qh = jnp.where(head_mask, qf, 0.0).astype(BF16)
            s = lax.dot_general(qh, kcat, (((1,), (1,)), ((), ())), preferred_element_type=F32)
            bias = bias_ref[p * 2 + hh]
            s = s + (bias[:, c:] if first else bias)
            m = jnp.max(s, axis=-1, keepdims=True)
            e = jnp.exp(s - m)
            den = jnp.sum(e, axis=-1, keepdims=True)
            o = jnp.dot(e.astype(BF16), vcat, preferred_element_type=F32)
            parts.append((m, den, o))
        m_b = jnp.where(lo_half, parts[0][0], parts[1][0])
        d_b = jnp.where(lo_half, parts[0][1], parts[1][1])
        n_b = jnp.where(lo_half, parts[0][2], parts[1][2])
        sl = rows(start_cur, dil)
        if p == 0:
            m_acc[sl, :] = m_b
            d_acc[sl, :] = d_b
            n_acc[sl, :] = n_b
        else:
            m_o = m_acc[sl, :]
            m_n = jnp.maximum(m_o, m_b)
            a_o = jnp.exp(m_o - m_n)
            a_b = jnp.exp(m_b - m_n)
            d_acc[sl, :] = d_acc[sl, :] * a_o + d_b * a_b
            n_acc[sl, :] = n_acc[sl, :] * a_o + n_b * a_b
            m_acc[sl, :] = m_n

    for p, (window, dil) in enumerate(DILATED_PATTERNS):
        nb = seq // (c * dil)

        def residue(r, carry, p=p, dil=dil, nb=nb):
            block(p, dil, r, None)

            def blk(n, carry2):
                cur = r + dil * c * n
                if dil == 1:
                    cur = pl.multiple_of(cur, c)
                block(p, dil, cur, cur - dil * c)
                return carry2

            lax.fori_loop(1, nb, blk, 0)
            return carry

        if dil == 1:
            residue(0, 0)
        else:
            lax.fori_loop(0, dil, residue, 0)

    o_ref[0] = n_acc[...] / d_acc[...]


def _attention(qkv, slopes):
    b, s, _ = qkv.shape
    assert all(s % (ATTN_BLOCK * dil) == 0 and window // dil <= ATTN_BLOCK
               for window, dil in DILATED_PATTERNS)
    n_pairs = ATTN_WIDTH // LANES
    blk = lambda off: pl.BlockSpec((1, s, LANES), lambda bi, hp, sl: (bi, 0, off + hp))
    return pl.pallas_call(
        functools.partial(_attn_kernel, seq=s),
        grid_spec=pltpu.PrefetchScalarGridSpec(
            num_scalar_prefetch=1,
            grid=(b, n_pairs),
            in_specs=[blk(0), blk(n_pairs), blk(2 * n_pairs)],
            out_specs=pl.BlockSpec((1, s, LANES), lambda bi, hp, sl: (bi, 0, hp)),
            scratch_shapes=[
                pltpu.VMEM((2 * len(DILATED_PATTERNS), ATTN_BLOCK, 2 * ATTN_BLOCK), F32),
                pltpu.VMEM((s, LANES), F32),
                pltpu.VMEM((s, LANES), F32),
                pltpu.VMEM((s, LANES), F32),
            ],
        ),
        out_shape=jax.ShapeDtypeStruct((b, s, ATTN_WIDTH), F32),
        compiler_params=_params(("parallel", "parallel")),
        name="dilated_attention",
    )(slopes, qkv, qkv, qkv)


def _rwkv_prep_kernel(rest_ref, prev_ref, mu_ref, w0_ref, wdu_ref, a0_ref, wau_ref, wg_ref,
                      kk_ref, ka_ref,
                      r_out, lw_out, k_out, v_out, kk_out, a_out, g_out):
    i = pl.program_id(1)
    z = rest_ref[0]
    ts = z.shape[0]
    prev_row = jnp.where(i == 0, 0.0, prev_ref[0, 7:8, :])
    row = lax.broadcasted_iota(I32, z.shape, 0)
    zprev = jnp.where(row == 0, prev_row, pltpu.roll(z, 1, 0))
    xs = z + (zprev - z) * mu_ref[...]
    w = RWKV_WIDTH
    r = xs[:, :w]
    k = xs[:, w:2 * w]
    v = xs[:, 2 * w:3 * w]
    wd = xs[:, 3 * w:3 * w + DECAY_LORA]
    ad = xs[:, 3 * w + DECAY_LORA:3 * w + DECAY_LORA + ICLR_LORA]
    gd = xs[:, 3 * w + DECAY_LORA + ICLR_LORA:]
    zz = w0_ref[...] + _dot_hi(jnp.tanh(wd), wdu_ref[...])
    w_raw = jnp.minimum(zz, 0.0) - jnp.log(1.0 + jnp.exp(-jnp.abs(zz))) - 0.5
    lw_out[0] = -jnp.exp(w_raw)
    a = _sigmoid(a0_ref[...] + _dot_hi(ad, wau_ref[...]))
    g_out[0] = _dot_hi(_sigmoid(gd), wg_ref[...])
    kk = k * kk_ref[...]
    ss = _dot_exact_rhs(kk * kk, _head_ones(w))
    kk_out[0] = kk / jnp.maximum(jnp.sqrt(ss), 1e-12)
    k_out[0] = k * (1.0 + (a - 1.0) * ka_ref[...])
    r_out[0] = r
    v_out[0] = v
    a_out[0] = a


def _rwkv_prep(rest, shift_mu, decay_w0, w_decay_up, iclr_a0, w_iclr_up, w_gate_lr_up, k_k, k_a):
    b, s, _ = rest.shape
    ts = 256
    row = lambda x: x.reshape(1, -1)
    full = lambda shape: pl.BlockSpec(shape, lambda bi, i: (0, 0))
    w = RWKV_WIDTH
    out_spec = pl.BlockSpec((1, ts, w), lambda bi, i: (bi, i, 0))
    return pl.pallas_call(
        _rwkv_prep_kernel,
        grid=(b, s // ts),
        in_specs=[
            pl.BlockSpec((1, ts, SHIFT_WIDTH), lambda bi, i: (bi, i, 0)),
            pl.BlockSpec((1, 8, SHIFT_WIDTH), lambda bi, i: (bi, jnp.maximum(i * (ts // 8) - 1, 0), 0)),
            full((1, SHIFT_WIDTH)), full((1, w)), full((DECAY_LORA, w)), full((1, w)),
            full((ICLR_LORA, w)), full((GATE_LORA, w)), full((1, w)), full((1, w)),
        ],
        out_specs=[out_spec] * 7,
        out_shape=[jax.ShapeDtypeStruct((b, s, w), F32)] * 7,
        compiler_params=_params(("parallel", "parallel")),
        name="rwkv_prep",
    )(rest, rest, row(shift_mu), row(decay_w0), w_decay_up, row(iclr_a0), w_iclr_up,
      w_gate_lr_up, row(k_k), row(k_a))


def _rwkv_scan_kernel(r_ref, lw_ref, k_ref, v_ref, kk_ref, a_ref, y_ref, st_ref, *, n_chunks):
    l = CHUNK
    n = HEAD_DIM

    @pl.when(pl.program_id(1) == 0)
    def _():
        st_ref[...] = jnp.zeros_like(st_ref)

    ri = lax.broadcasted_iota(I32, (l, l), 0)
    ci = lax.broadcasted_iota(I32, (l, l), 1)
    tri = jnp.where(ri >= ci, 1.0, 0.0).astype(BF16)
    eye = jnp.where(ri == ci, 1.0, 0.0).astype(F32)
    ri2 = lax.broadcasted_iota(I32, (l, 2 * l), 0)
    ci2 = lax.broadcasted_iota(I32, (l, 2 * l), 1) % l
    strict2 = ri2 > ci2
    incl2 = ri2 >= ci2
    zeros_ln = jnp.zeros((l, n), F32)

    def chunk(ch, carry):
        sl = pl.ds(pl.multiple_of(ch * l, l), l)
        r = r_ref[0, sl, :]
        lw = lw_ref[0, sl, :]
        k = k_ref[0, sl, :]
        v = v_ref[0, sl, :]
        kk = kk_ref[0, sl, :]
        a = a_ref[0, sl, :]
        cum = _dot_exact_lhs(tri, lw)
        tot = cum[l - 1:l, :]
        p_in = jnp.exp(cum)
        p_ex = jnp.exp(cum - lw)
        p_inv = jnp.exp(-cum)
        p_end = jnp.exp(tot - cum)
        p_tot = jnp.exp(tot)
        kka = kk * a
        al = -kk * p_ex
        be = kka * p_inv
        kt = k * p_inv
        rt = r * p_in
        bh = kka * p_end
        kh = k * p_end
        for h in range(RWKV_HEADS):
            hs = slice(h * n, (h + 1) * n)
            al_h, be_h, kt_h, rt_h, bh_h, kh_h, v_h = (
                t[:, hs] for t in (al, be, kt, rt, bh, kh, v))
            bk = jnp.concatenate([be_h, kt_h], axis=0)
            aa = jnp.where(strict2, _dot_nt(al_h, bk), 0.0)
            rr = jnp.where(incl2, _dot_nt(rt_h, bk), 0.0)
            a_ab = aa[:, :l]
            x = a_ab
            t_inv = eye + a_ab
            for _ in range(int(math.log2(l)) - 1):
                x = _dot(x, x)
                t_inv = t_inv + _dot(t_inv, x)
            akv = _dot(aa, jnp.concatenate([zeros_ln, v_h], axis=0))
            w_t = _dot(t_inv, al_h)
            u0 = _dot(t_inv, akv)
            r_hat = rt_h + _dot(rr, jnp.concatenate([w_t, zeros_ln], axis=0))
            y0 = _dot(rr, jnp.concatenate([u0, v_h], axis=0))
            m_mat = eye * p_tot[:, hs] + _dot_tn(bh_h, w_t)
            c_mat = _dot_tn(jnp.concatenate([bh_h, kh_h], axis=0),
                            jnp.concatenate([u0, v_h], axis=0))
            st = st_ref[h]
            y_ref[0, sl, hs] = _dot(r_hat, st) + y0
            st_ref[h] = _dot(m_mat, st) + c_mat
        return carry

    lax.fori_loop(0, n_chunks, chunk, 0)


def _rwkv_scan(r, lw, k, v, kk, a):
    b, s, w = r.shape
    tc = 512
    assert s % tc == 0
    spec = pl.BlockSpec((1, tc, w), lambda bi, i: (bi, i, 0))
    return pl.pallas_call(
        functools.partial(_rwkv_scan_kernel, n_chunks=tc // CHUNK),
        grid=(b, s // tc),
        in_specs=[spec] * 6,
        out_specs=spec,
        out_shape=jax.ShapeDtypeStruct((b, s, w), F32),
        scratch_shapes=[pltpu.VMEM((RWKV_HEADS, HEAD_DIM, HEAD_DIM), F32)],
        compiler_params=_params(("parallel", "arbitrary")),
        name="rwkv_scan",
    )(r, lw, k, v, kk, a)


def _mix_kernel(attn_ref, y_ref, r_ref, k_ref, v_ref, g_ref, x_ref, rk_ref, gnw_ref, gnb_ref,
                wo_ref, nf_ref, wr_ref, br_ref,
                h_out, xn_out, idx_out, gate_out):
    ones = _head_ones(RWKV_WIDTH)
    inv_n = 1.0 / HEAD_DIM
    y = y_ref[...]
    mu = _dot_exact_rhs(y, ones) * inv_n
    yc = y - mu
    var = _dot_exact_rhs(yc * yc, ones) * inv_n
    yn = yc * lax.rsqrt(var + GN_EPS) * gnw_ref[...] + gnb_ref[...]
    v = v_ref[...]
    bonus = _dot_exact_rhs(r_ref[...] * k_ref[...] * rk_ref[...], ones) * v
    rw = (yn + bonus) * g_ref[...]
    mixed = (jnp.dot(attn_ref[...].astype(BF16), wo_ref[:ATTN_WIDTH, :], preferred_element_type=F32)
             + jnp.dot(rw.astype(BF16), wo_ref[ATTN_WIDTH:, :], preferred_element_type=F32))
    h = x_ref[...] + mixed
    h_out[...] = h
    xn = h * lax.rsqrt(jnp.mean(h * h, axis=-1, keepdims=True) + RMS_EPS) * nf_ref[...]
    xn_out[...] = xn
    logits = _dot_hi(xn, wr_ref[...]) + br_ref[...]
    eidx = lax.broadcasted_iota(I32, logits.shape, 1).astype(F32)
    lane = lax.broadcasted_iota(I32, (logits.shape[0], LANES), 1)
    idx_pad = jnp.zeros((logits.shape[0], LANES), F32)
    val_pad = jnp.full((logits.shape[0], LANES), MASK_VALUE, F32)
    cur = logits
    for kth in range(TOP_K):
        m = jnp.max(cur, axis=-1, keepdims=True)
        sel = jnp.min(jnp.where(cur == m, eidx, float(N_EXPERTS)), axis=-1, keepdims=True)
        idx_pad = jnp.where(lane == kth, sel, idx_pad)
        val_pad = jnp.where(lane == kth, m, val_pad)
        cur = jnp.where(eidx == sel, -jnp.inf, cur)
    top = jnp.max(val_pad, axis=-1, keepdims=True)
    e = jnp.exp(val_pad - top)
    gate_out[...] = e / jnp.sum(e, axis=-1, keepdims=True)
    idx_out[...] = idx_pad.astype(I32)


def _mix_out(attn, y, r, k, v, g, xf, r_k, gn_w, gn_b, w_out, norm_ffn_g, w_router, b_router):
    n = xf.shape[0]
    tm = 256
    w = RWKV_WIDTH
    row = lambda x: x.reshape(1, -1)
    half = pl.BlockSpec((tm, w), lambda i: (i, 0))
    wide = pl.BlockSpec((tm, D_MODEL), lambda i: (i, 0))
    pad = pl.BlockSpec((tm, LANES), lambda i: (i, 0))
    full = lambda shape: pl.BlockSpec(shape, lambda i: (0, 0))
    return pl.pallas_call(
        _mix_kernel,
        grid=(n // tm,),
        in_specs=[half] * 6 + [wide, full((1, w)), full((1, w)), full((1, w)),
                               full((D_MODEL, D_MODEL)), full((1, D_MODEL)),
                               full((D_MODEL, N_EXPERTS)), full((1, N_EXPERTS))],
        out_specs=[wide, wide, pad, pad],
        out_shape=[jax.ShapeDtypeStruct((n, D_MODEL), F32), jax.ShapeDtypeStruct((n, D_MODEL), F32),
                   jax.ShapeDtypeStruct((n, LANES), I32), jax.ShapeDtypeStruct((n, LANES), F32)],
        compiler_params=_params(("parallel",)),
        name="mix_out_router",
    )(attn, y, r, k, v, g, xf, row(r_k), row(gn_w), row(gn_b), w_out.astype(BF16),
      row(norm_ffn_g), w_router, row(b_router))


def _dispatch_kernel(dest_ref, x_ref, zeros_ref, out_ref, sem, *, tm):
    del zeros_ref

    def row_copy(j, kth):
        return pltpu.make_async_copy(
            x_ref.at[pl.ds(j, 1), :],
            out_ref.at[pl.ds(dest_ref[j * TOP_K + kth], 1), :],
            sem)

    def issue(j, carry):
        for kth in range(TOP_K):
            row_copy(j, kth).start()
        return carry

    lax.fori_loop(0, tm, issue, 0)
    for kth in range(TOP_K):
        pltpu.make_async_copy(x_ref, out_ref.at[pl.ds(0, tm), :], sem).wait()


def _dispatch(xn, dest_flat, n_rows):
    n = xn.shape[0]
    tm = 256
    zeros = jnp.zeros((n_rows, D_MODEL), F32)
    return pl.pallas_call(
        functools.partial(_dispatch_kernel, tm=tm),
        grid=(n // tm,),
        in_specs=[
            pl.BlockSpec((tm * TOP_K,), lambda i: (i,), memory_space=pltpu.SMEM),
            pl.BlockSpec((tm, D_MODEL), lambda i: (i, 0)),
            pl.BlockSpec(memory_space=pl.ANY),
        ],
        out_specs=pl.BlockSpec(memory_space=pl.ANY),
        out_shape=jax.ShapeDtypeStruct((n_rows, D_MODEL), F32),
        scratch_shapes=[pltpu.SemaphoreType.DMA],
        input_output_aliases={2: 0},
        compiler_params=_params(("arbitrary",)),
        name="moe_dispatch",
    )(dest_flat, xn, zeros)


def _expert_kernel(be_ref, nvalid_ref, x_ref, wgu_ref, bgu_ref, wd_ref, bd_ref, o_ref):
    i = pl.program_id(0)

    @pl.when(i < nvalid_ref[0])
    def _():
        x = x_ref[...].astype(BF16)
        acc = jnp.zeros(o_ref.shape, F32)
        cw = 256
        for j in range(D_FF // cw):
            g = jnp.dot(x, wgu_ref[0, :, j * cw:(j + 1) * cw], preferred_element_type=F32)
            g = g + bgu_ref[0, :, j * cw:(j + 1) * cw]
            u = jnp.dot(x, wgu_ref[0, :, D_FF + j * cw:D_FF + (j + 1) * cw],
                        preferred_element_type=F32)
            u = u + bgu_ref[0, :, D_FF + j * cw:D_FF + (j + 1) * cw]
            g = jnp.minimum(g, SWIGLU_LIMIT)
            u = jnp.clip(u, -SWIGLU_LIMIT, SWIGLU_LIMIT)
            act = (u + 1.0) * (g * _sigmoid(SWIGLU_ALPHA * g))
            acc = acc + jnp.dot(act.astype(BF16), wd_ref[0, j * cw:(j + 1) * cw, :],
                                preferred_element_type=F32)
        o_ref[...] = acc + bd_ref[0]

    @pl.when(i >= nvalid_ref[0])
    def _():
        o_ref[...] = jnp.zeros_like(o_ref)


def _experts(xs, blk_expert, n_valid, w_gate_up, b_gate_up, w_down, b_down, bm):
    n_rows = xs.shape[0]
    n_blocks = n_rows // bm
    return pl.pallas_call(
        _expert_kernel,
        grid_spec=pltpu.PrefetchScalarGridSpec(
            num_scalar_prefetch=2,
            grid=(n_blocks,),
            in_specs=[
                pl.BlockSpec((bm, D_MODEL), lambda i, be, nv: (i, 0)),
                pl.BlockSpec((1, D_MODEL, 2 * D_FF), lambda i, be, nv: (be[i], 0, 0)),
                pl.BlockSpec((1, 1, 2 * D_FF), lambda i, be, nv: (be[i], 0, 0)),
                pl.BlockSpec((1, D_FF, D_MODEL), lambda i, be, nv: (be[i], 0, 0)),
                pl.BlockSpec((1, 1, D_MODEL), lambda i, be, nv: (be[i], 0, 0)),
            ],
            out_specs=pl.BlockSpec((bm, D_MODEL), lambda i, be, nv: (i, 0)),
        ),
        out_shape=jax.ShapeDtypeStruct((n_rows, D_MODEL), F32),
        compiler_params=_params(("arbitrary",)),
        name="moe_experts",
    )(blk_expert, n_valid, xs, w_gate_up.astype(BF16), b_gate_up.reshape(N_EXPERTS, 1, 2 * D_FF),
      w_down.astype(BF16), b_down.reshape(N_EXPERTS, 1, D_MODEL))


def _combine_kernel(dest_ref, h_ref, gate_ref, g_ref, ys_ref, o_ref, buf, sem, *, tm):
    def issue(j, carry):
        for kth in range(TOP_K):
            pltpu.make_async_copy(
                ys_ref.at[pl.ds(dest_ref[j * TOP_K + kth], 1), :],
                buf.at[kth, pl.ds(j, 1), :],
                sem).start()
        return carry

    lax.fori_loop(0, tm, issue, 0)
    for kth in range(TOP_K):
        pltpu.make_async_copy(ys_ref.at[pl.ds(0, tm), :], buf.at[kth], sem).wait()
    gates = gate_ref[...]
    hf = h_ref[...]
    for kth in range(TOP_K):
        hf = hf + gates[:, kth:kth + 1] * buf[kth]
    o_ref[...] = hf * lax.rsqrt(jnp.mean(hf * hf, axis=-1, keepdims=True) + RMS_EPS) * g_ref[...]


def _combine(h, gate_pad, ys, dest_flat, norm_final_g):
    n = h.shape[0]
    tm = 256
    return pl.pallas_call(
        functools.partial(_combine_kernel, tm=tm),
        grid=(n // tm,),
        in_specs=[
            pl.BlockSpec((tm * TOP_K,), lambda i: (i,), memory_space=pltpu.SMEM),
            pl.BlockSpec((tm, D_MODEL), lambda i: (i, 0)),
            pl.BlockSpec((tm, LANES), lambda i: (i, 0)),
            pl.BlockSpec((1, D_MODEL), lambda i: (0, 0)),
            pl.BlockSpec(memory_space=pl.ANY),
        ],
        out_specs=pl.BlockSpec((tm, D_MODEL), lambda i: (i, 0)),
        out_shape=jax.ShapeDtypeStruct((n, D_MODEL), F32),
        scratch_shapes=[pltpu.VMEM((TOP_K, tm, D_MODEL), F32), pltpu.SemaphoreType.DMA],
        compiler_params=_params(("arbitrary",)),
        name="moe_combine_norm",
    )(dest_flat, h, gate_pad, norm_final_g.reshape(1, D_MODEL), ys)


def _routing(top_idx, bm):
    n = top_idx.shape[0]
    onehot = jnp.sum((top_idx[:, :, None] == jnp.arange(N_EXPERTS, dtype=I32)).astype(I32), axis=1)
    csum = jnp.cumsum(onehot, axis=0)
    counts = csum[-1]
    rank = jnp.take_along_axis(csum - onehot, top_idx, axis=1)
    padded = (counts + bm - 1) // bm * bm
    pend = jnp.cumsum(padded)
    pstart = pend - padded
    dest = pstart[top_idx] + rank
    n_blocks = (n * TOP_K + N_EXPERTS * (bm - 1) + bm - 1) // bm
    blk_expert = jnp.minimum(
        jnp.searchsorted(pend, jnp.arange(n_blocks, dtype=pend.dtype) * bm, side='right'),
        N_EXPERTS - 1).astype(I32)
    n_valid = (pend[-1:] // bm).astype(I32)
    return dest.reshape(n * TOP_K).astype(I32), blk_expert, n_valid, n_blocks


def _layer(xf, b, s, p):
    qkv, rest = _in_proj(xf, p["norm_mix_g"], p["w_in"])
    slopes = jnp.exp2(-8.0 / ATTN_HEADS * jnp.arange(1, ATTN_HEADS + 1, dtype=F32))
    attn = _attention(qkv.reshape(b, s, 3 * ATTN_WIDTH), slopes)
    r, lw, k, v, kk, a, g = _rwkv_prep(
        rest.reshape(b, s, SHIFT_WIDTH), p["shift_mu"], p["decay_w0"], p["w_decay_up"],
        p["iclr_a0"], p["w_iclr_up"], p["w_gate_lr_up"], p["k_k"], p["k_a"])
    y = _rwkv_scan(r, lw, k, v, kk, a)
    flat = lambda t: t.reshape(b * s, -1)
    h, xn, idx_pad, gate_pad = _mix_out(
        flat(attn), flat(y), flat(r), flat(k), flat(v), flat(g), xf, p["r_k"].reshape(-1),
        p["gn_w"], p["gn_b"], p["w_out"], p["norm_ffn_g"], p["w_router"], p["b_router"])
    bm = 512
    dest, blk_expert, n_valid, n_blocks = _routing(idx_pad[:, :TOP_K], bm)
    xs = _dispatch(xn, dest, n_blocks * bm)
    ys = _experts(xs, blk_expert, n_valid, p["w_gate_up"], p["b_gate_up"], p["w_down"],
                  p["b_down"], bm)
    return h, gate_pad, ys, dest


def kernel(x, norm_mix_g, w_in, shift_mu, decay_w0, w_decay_up, iclr_a0, w_iclr_up, w_gate_lr_up,
           k_k, k_a, r_k, gn_w, gn_b, w_out, norm_ffn_g, w_router, b_router, w_gate_up,
           b_gate_up, w_down, b_down, norm_final_g):
    b, s, d = x.shape
    assert d == D_MODEL and w_in.shape[0] == 1, "single-layer block"
    names = ("norm_mix_g", "w_in", "shift_mu", "decay_w0", "w_decay_up", "iclr_a0", "w_iclr_up",
             "w_gate_lr_up", "k_k", "k_a", "r_k", "gn_w", "gn_b", "w_out", "norm_ffn_g",
             "w_router", "b_router", "w_gate_up", "b_gate_up", "w_down", "b_down")
    vals = (norm_mix_g, w_in, shift_mu, decay_w0, w_decay_up, iclr_a0, w_iclr_up, w_gate_lr_up,
            k_k, k_a, r_k, gn_w, gn_b, w_out, norm_ffn_g, w_router, b_router, w_gate_up,
            b_gate_up, w_down, b_down)
    p = {nm: v[0] for nm, v in zip(names, vals)}
    h, gate_pad, ys, dest = _layer(x.reshape(b * s, d), b, s, p)
    out = _combine(h, gate_pad, ys, dest, norm_final_g)
    return out.reshape(b, s, d)
```

```python
import functools
import math

import jax
import jax.numpy as jnp
from jax import lax
from jax.experimental import pallas as pl
from jax.experimental.pallas import tpu as pltpu

F32 = jnp.float32
BF16 = jnp.bfloat16
I32 = jnp.int32

D_MODEL = 1024
HEAD_DIM = 64
ATTN_WIDTH = 512
ATTN_HEADS = ATTN_WIDTH // HEAD_DIM
RWKV_WIDTH = 512
RWKV_HEADS = RWKV_WIDTH // HEAD_DIM
DILATED_PATTERNS = ((128, 1), (512, 4), (2048, 16))
ATTN_BLOCK = 128
DECAY_LORA = 64
ICLR_LORA = 64
GATE_LORA = 128
SHIFT_WIDTH = 3 * RWKV_WIDTH + DECAY_LORA + ICLR_LORA + GATE_LORA
N_EXPERTS = 32
TOP_K = 4
D_FF = D_MODEL
SWIGLU_LIMIT = 7.0
SWIGLU_ALPHA = 1.702
RMS_EPS = 1e-5
GN_EPS = 64e-5

LANES = 128
CHUNK = 64
MASK_VALUE = -1e30
VMEM_LIMIT = 56 * 1024 * 1024


def _params(semantics, vmem=VMEM_LIMIT):
    return pltpu.CompilerParams(dimension_semantics=semantics, vmem_limit_bytes=vmem)


def _dot(a, b):
    return jnp.dot(a.astype(BF16), b.astype(BF16), preferred_element_type=F32)


def _dot_nt(a, b):
    return lax.dot_general(a.astype(BF16), b.astype(BF16), (((1,), (1,)), ((), ())),
                           preferred_element_type=F32)


def _dot_tn(a, b):
    return lax.dot_general(a.astype(BF16), b.astype(BF16), (((0,), (0,)), ((), ())),
                           preferred_element_type=F32)


def _split3(x):
    h = x.astype(BF16)
    r1 = x - h.astype(F32)
    m = r1.astype(BF16)
    l = (r1 - m.astype(F32)).astype(BF16)
    return h, m, l


def _dot_exact_lhs(a_bf16, x):
    h, m, l = _split3(x)
    d = lambda y: jnp.dot(a_bf16, y, preferred_element_type=F32)
    return d(h) + d(m) + d(l)


def _dot_exact_rhs(x, b_bf16):
    h, m, l = _split3(x)
    d = lambda y: jnp.dot(y, b_bf16, preferred_element_type=F32)
    return d(h) + d(m) + d(l)


def _dot_hi(a, b):
    ah = a.astype(BF16)
    al = (a - ah.astype(F32)).astype(BF16)
    bh = b.astype(BF16)
    bl = (b - bh.astype(F32)).astype(BF16)
    d = lambda x, y: jnp.dot(x, y, preferred_element_type=F32)
    return d(ah, bh) + d(ah, bl) + d(al, bh)


def _sigmoid(x):
    return 1.0 / (1.0 + jnp.exp(-x))


def _head_ones(width):
    r = lax.broadcasted_iota(I32, (width, width), 0) // HEAD_DIM
    c = lax.broadcasted_iota(I32, (width, width), 1) // HEAD_DIM
    return jnp.where(r == c, 1.0, 0.0).astype(BF16)


def _inproj_kernel(x_ref, g_ref, wq_ref, wr_ref, qkv_ref, rest_ref):
    x = x_ref[...]
    xn = x * lax.rsqrt(jnp.mean(x * x, axis=-1, keepdims=True) + RMS_EPS) * g_ref[...]
    xb = xn.astype(BF16)
    qkv_ref[...] = jnp.dot(xb, wq_ref[...], preferred_element_type=F32)
    rest_ref[...] = jnp.dot(xb, wr_ref[...], preferred_element_type=F32)


def _in_proj(xf, g, w_in):
    n = xf.shape[0]
    tm = 256
    wq = w_in[:, :3 * ATTN_WIDTH].astype(BF16)
    wr = w_in[:, 3 * ATTN_WIDTH:].astype(BF16)
    return pl.pallas_call(
        _inproj_kernel,
        grid=(n // tm,),
        in_specs=[
            pl.BlockSpec((tm, D_MODEL), lambda i: (i, 0)),
            pl.BlockSpec((1, D_MODEL), lambda i: (0, 0)),
            pl.BlockSpec((D_MODEL, 3 * ATTN_WIDTH), lambda i: (0, 0)),
            pl.BlockSpec((D_MODEL, SHIFT_WIDTH), lambda i: (0, 0)),
        ],
        out_specs=[
            pl.BlockSpec((tm, 3 * ATTN_WIDTH), lambda i: (i, 0)),
            pl.BlockSpec((tm, SHIFT_WIDTH), lambda i: (i, 0)),
        ],
        out_shape=[
            jax.ShapeDtypeStruct((n, 3 * ATTN_WIDTH), F32),
            jax.ShapeDtypeStruct((n, SHIFT_WIDTH), F32),
        ],
        compiler_params=_params(("parallel",)),
        name="in_proj",
    )(xf, g.reshape(1, D_MODEL), wq, wr)


def _attn_kernel(slopes_ref, q_ref, k_ref, v_ref, o_ref, bias_ref, m_acc, d_acc, n_acc, *, seq):
    c = ATTN_BLOCK
    hp = pl.program_id(1)
    lane = lax.broadcasted_iota(I32, (c, LANES), 1)
    lo_half = lane < HEAD_DIM
    qi = lax.broadcasted_iota(I32, (c, 2 * c), 0)
    kj = lax.broadcasted_iota(I32, (c, 2 * c), 1)
    diff = qi + c - kj
    for p, (window, dil) in enumerate(DILATED_PATTERNS):
        steps = window // dil
        valid = (diff >= 0) & (diff <= steps)
        dist = (dil * diff).astype(F32)
        for hh in range(2):
            slope = slopes_ref[hp * 2 + hh]
            bias_ref[p * 2 + hh] = jnp.where(valid, -slope * dist, MASK_VALUE)

    scale = 1.0 / math.sqrt(HEAD_DIM)

    def rows(start, dil):
        if dil == 1:
            return pl.ds(start, c)
        return pl.ds(start, c, stride=dil)

    def block(p, dil, start_cur, start_prev):
        first = start_prev is None
        qf = q_ref[0, rows(start_cur, dil), :] * scale
        kcat = k_ref[0, rows(start_cur, dil), :].astype(BF16)
        vcat = v_ref[0, rows(start_cur, dil), :].astype(BF16)
        if not first:
            kp = k_ref[0, rows(start_prev, dil), :].astype(BF16)
            vp = v_ref[0, rows(start_prev, dil), :].astype(BF16)
            kcat = jnp.concatenate([kp, kcat], axis=0)
            vcat = jnp.concatenate([vp, vcat], axis=0)
        parts = []
        for hh in range(2):
            head_mask = lo_half if hh == 0 else jnp.logical_not(lo_half)
            qh = jnp.where(head_mask, qf, 0.0).astype(BF16)
            s = lax.dot_general(qh, kcat, (((1,), (1,)), ((), ())), preferred_element_type=F32)
            bias = bias_ref[p * 2 + hh]
            s = s + (bias[:, c:] if first else bias)
            m = jnp.max(s, axis=-1, keepdims=True)
            e = jnp.exp(s - m)
            den = jnp.sum(e, axis=-1, keepdims=True)
            o = jnp.dot(e.astype(BF16), vcat, preferred_element_type=F32)
            parts.append((m, den, o))
        m_b = jnp.where(lo_half, parts[0][0], parts[1][0])
        d_b = jnp.where(lo_half, parts[0][1], parts[1][1])
        n_b = jnp.where(lo_half, parts[0][2], parts[1][2])
        sl = rows(start_cur, dil)
        if p == 0:
            m_acc[sl, :] = m_b
            d_acc[sl, :] = d_b
            n_acc[sl, :] = n_b
        else:
            m_o = m_acc[sl, :]
            m_n = jnp.maximum(m_o, m_b)
            a_o = jnp.exp(m_o - m_n)
            a_b = jnp.exp(m_b - m_n)
            d_acc[sl, :] = d_acc[sl, :] * a_o + d_b * a_b
            n_acc[sl, :] = n_acc[sl, :] * a_o + n_b * a_b
            m_acc[sl, :] = m_n

    for p, (window, dil) in enumerate(DILATED_PATTERNS):
        nb = seq // (c * dil)

        def residue(r, carry, p=p, dil=dil, nb=nb):
            block(p, dil, r, None)

            def blk(n, carry2):
                cur = r + dil * c * n
                if dil == 1:
                    cur = pl.multiple_of(cur, c)
                block(p, dil, cur, cur - dil * c)
                return carry2

            lax.fori_loop(1, nb, blk, 0)
            return carry

        if dil == 1:
            residue(0, 0)
        else:
            lax.fori_loop(0, dil, residue, 0)

    o_ref[0] = n_acc[...] / d_acc[...]


def _attention(qkv, slopes):
    b, s, _ = qkv.shape
    assert all(s % (ATTN_BLOCK * dil) == 0 and window // dil <= ATTN_BLOCK
               for window, dil in DILATED_PATTERNS)
    n_pairs = ATTN_WIDTH // LANES
    blk = lambda off: pl.BlockSpec((1, s, LANES), lambda bi, hp, sl: (bi, 0, off + hp))
    return pl.pallas_call(
        functools.partial(_attn_kernel, seq=s),
        grid_spec=pltpu.PrefetchScalarGridSpec(
            num_scalar_prefetch=1,
            grid=(b, n_pairs),
            in_specs=[blk(0), blk(n_pairs), blk(2 * n_pairs)],
            out_specs=pl.BlockSpec((1, s, LANES), lambda bi, hp, sl: (bi, 0, hp)),
            scratch_shapes=[
                pltpu.VMEM((2 * len(DILATED_PATTERNS), ATTN_BLOCK, 2 * ATTN_BLOCK), F32),
                pltpu.VMEM((s, LANES), F32),
                pltpu.VMEM((s, LANES), F32),
                pltpu.VMEM((s, LANES), F32),
            ],
        ),
        out_shape=jax.ShapeDtypeStruct((b, s, ATTN_WIDTH), F32),
        compiler_params=_params(("parallel", "parallel")),
        name="dilated_attention",
    )(slopes, qkv, qkv, qkv)


def _rwkv_prep_kernel(rest_ref, prev_ref, mu_ref, w0_ref, wdu_ref, a0_ref, wau_ref, wg_ref,
                      kk_ref, ka_ref,
                      r_out, lw_out, k_out, v_out, kk_out, a_out, g_out):
    i = pl.program_id(1)
    z = rest_ref[0]
    ts = z.shape[0]
    prev_row = jnp.where(i == 0, 0.0, prev_ref[0, 7:8, :])
    row = lax.broadcasted_iota(I32, z.shape, 0)
    zprev = jnp.where(row == 0, prev_row, pltpu.roll(z, 1, 0))
    xs = z + (zprev - z) * mu_ref[...]
    w = RWKV_WIDTH
    r = xs[:, :w]
    k = xs[:, w:2 * w]
    v = xs[:, 2 * w:3 * w]
    wd = xs[:, 3 * w:3 * w + DECAY_LORA]
    ad = xs[:, 3 * w + DECAY_LORA:3 * w + DECAY_LORA + ICLR_LORA]
    gd = xs[:, 3 * w + DECAY_LORA + ICLR_LORA:]
    zz = w0_ref[...] + _dot_hi(jnp.tanh(wd), wdu_ref[...])
    w_raw = jnp.minimum(zz, 0.0) - jnp.log(1.0 + jnp.exp(-jnp.abs(zz))) - 0.5
    lw_out[0] = -jnp.exp(w_raw)
    a = _sigmoid(a0_ref[...] + _dot_hi(ad, wau_ref[...]))
    g_out[0] = _dot_hi(_sigmoid(gd), wg_ref[...])
    kk = k * kk_ref[...]
    ss = _dot_exact_rhs(kk * kk, _head_ones(w))
    kk_out[0] = kk / jnp.maximum(jnp.sqrt(ss), 1e-12)
    k_out[0] = k * (1.0 + (a - 1.0) * ka_ref[...])
    r_out[0] = r
    v_out[0] = v
    a_out[0] = a


def _rwkv_prep(rest, shift_mu, decay_w0, w_decay_up, iclr_a0, w_iclr_up, w_gate_lr_up, k_k, k_a):
    b, s, _ = rest.shape
    ts = 256
    row = lambda x: x.reshape(1, -1)
    full = lambda shape: pl.BlockSpec(shape, lambda bi, i: (0, 0))
    w = RWKV_WIDTH
    out_spec = pl.BlockSpec((1, ts, w), lambda bi, i: (bi, i, 0))
    return pl.pallas_call(
        _rwkv_prep_kernel,
        grid=(b, s // ts),
        in_specs=[
            pl.BlockSpec((1, ts, SHIFT_WIDTH), lambda bi, i: (bi, i, 0)),
            pl.BlockSpec((1, 8, SHIFT_WIDTH), lambda bi, i: (bi, jnp.maximum(i * (ts // 8) - 1, 0), 0)),
            full((1, SHIFT_WIDTH)), full((1, w)), full((DECAY_LORA, w)), full((1, w)),
            full((ICLR_LORA, w)), full((GATE_LORA, w)), full((1, w)), full((1, w)),
        ],
        out_specs=[out_spec] * 7,
        out_shape=[jax.ShapeDtypeStruct((b, s, w), F32)] * 7,
        compiler_params=_params(("parallel", "parallel")),
        name="rwkv_prep",
    )(rest, rest, row(shift_mu), row(decay_w0), w_decay_up, row(iclr_a0), w_iclr_up,
      w_gate_lr_up, row(k_k), row(k_a))


def _rwkv_scan_kernel(r_ref, lw_ref, k_ref, v_ref, kk_ref, a_ref, y_ref, st_ref, *, n_chunks):
    l = CHUNK
    pw = LANES
    n_pairs = RWKV_WIDTH // pw
    tc = n_chunks * l
    w = RWKV_WIDTH

    @pl.when(pl.program_id(1) == 0)
    def _():
        st_ref[...] = jnp.zeros_like(st_ref)

    r = r_ref[0]
    lw = lw_ref[0]
    k = k_ref[0]
    v = v_ref[0]
    kk = kk_ref[0]
    a = a_ref[0]
    ri = lax.broadcasted_iota(I32, (tc, tc), 0)
    ci = lax.broadcasted_iota(I32, (tc, tc), 1)
    tri = jnp.where((ri >= ci) & (ri // l == ci // l), 1.0, 0.0).astype(BF16)
    cum = _dot_exact_lhs(tri, lw)
    tot = jnp.concatenate(
        [jnp.broadcast_to(cum[c * l + l - 1:c * l + l, :], (l, w)) for c in range(n_chunks)], axis=0)
    p_in = jnp.exp(cum)
    p_ex = jnp.exp(cum - lw)
    p_inv = jnp.exp(-cum)
    p_end = jnp.exp(tot - cum)
    p_tot = jnp.exp(tot)
    kka = kk * a
    al = -kk * p_ex
    be = kka * p_inv
    kt = k * p_inv
    rt = r * p_in
    bh = kka * p_end
    kh = k * p_end

    row = lax.broadcasted_iota(I32, (l, pw), 0)
    col = lax.broadcasted_iota(I32, (l, pw), 1) % l
    strict = row > col
    incl = row >= col
    eye_ss = jnp.where(row == col, 1.0, 0.0).astype(F32)
    row_bd = lax.broadcasted_iota(I32, (2 * l, pw), 0)
    lane_bd = lax.broadcasted_iota(I32, (2 * l, pw), 1)
    diag_mask = (row_bd // l) == (lane_bd // l)
    eye_bd = row_bd == lane_bd

    def bd(x):
        return jnp.where(diag_mask, jnp.concatenate([x, x], axis=0), 0.0)

    items = [(c, p) for c in range(n_chunks) for p in range(n_pairs)]
    cut = lambda t, c, p: t[c * l:(c + 1) * l, p * pw:(p + 1) * pw]

    a_ab, a_rb, a_ak, a_rk, akv = {}, {}, {}, {}, {}
    for it in items:
        lhs = jnp.concatenate([cut(al, *it), cut(rt, *it)], axis=0)
        g1 = _dot_nt(lhs, bd(cut(be, *it)))
        g2 = _dot_nt(lhs, bd(cut(kt, *it)))
        a_ab[it] = jnp.where(strict, g1[:l], 0.0)
        a_rb[it] = jnp.where(incl, g1[l:], 0.0)
        a_ak[it] = jnp.where(strict, g2[:l], 0.0)
        a_rk[it] = jnp.where(incl, g2[l:], 0.0)
    for it in items:
        akv[it] = _dot(a_ak[it], bd(cut(v, *it)))
    xs = dict(a_ab)
    ts = {it: eye_ss + a_ab[it] for it in items}
    for _ in range(int(math.log2(l)) - 1):
        for it in items:
            xs[it] = _dot(xs[it], bd(xs[it]))
        for it in items:
            ts[it] = ts[it] + _dot(ts[it], bd(xs[it]))
    w_t, u0 = {}, {}
    for it in items:
        w_t[it] = _dot(ts[it], bd(cut(al, *it)))
        u0[it] = _dot(ts[it], bd(akv[it]))
    r_hat, y0, m_bd, c_bd = {}, {}, {}, {}
    for it in items:
        v_i = cut(v, *it)
        bh_i = cut(bh, *it)
        r_hat[it] = cut(rt, *it) + _dot(a_rb[it], bd(w_t[it]))
        y0[it] = _dot(jnp.concatenate([a_rb[it], a_rk[it]], axis=1),
                      jnp.concatenate([bd(u0[it]), bd(v_i)], axis=0))
        decay = jnp.broadcast_to(cut(p_tot, *it)[:1], (2 * l, pw))
        m_bd[it] = jnp.where(diag_mask, _dot_tn(bh_i, w_t[it]), 0.0) + jnp.where(eye_bd, decay, 0.0)
        c_bd[it] = jnp.where(
            diag_mask,
            _dot_tn(jnp.concatenate([bh_i, cut(kh, *it)], axis=0),
                    jnp.concatenate([u0[it], v_i], axis=0)),
            0.0)
    st = [st_ref[p] for p in range(n_pairs)]
    for c in range(n_chunks):
        for p in range(n_pairs):
            it = (c, p)
            y_ref[0, c * l:(c + 1) * l, p * pw:(p + 1) * pw] = _dot(r_hat[it], st[p]) + y0[it]
            st[p] = _dot(m_bd[it], st[p]) + c_bd[it]
    for p in range(n_pairs):
        st_ref[p] = st[p]


def _rwkv_scan(r, lw, k, v, kk, a):
    b, s, w = r.shape
    tc = 256
    assert s % tc == 0
    spec = pl.BlockSpec((1, tc, w), lambda bi, i: (bi, i, 0))
    return pl.pallas_call(
        functools.partial(_rwkv_scan_kernel, n_chunks=tc // CHUNK),
        grid=(b, s // tc),
        in_specs=[spec] * 6,
        out_specs=spec,
        out_shape=jax.ShapeDtypeStruct((b, s, w), F32),
        scratch_shapes=[pltpu.VMEM((w // LANES, LANES, LANES), F32)],
        compiler_params=_params(("parallel", "arbitrary")),
        name="rwkv_scan",
    )(r, lw, k, v, kk, a)


def _mix_kernel(attn_ref, y_ref, r_ref, k_ref, v_ref, g_ref, x_ref, rk_ref, gnw_ref, gnb_ref,
                wo_ref, nf_ref, wr_ref, br_ref,
                h_out, xn_out, idx_out, gate_out):
    ones = _head_ones(RWKV_WIDTH)
    inv_n = 1.0 / HEAD_DIM
    y = y_ref[...]
    mu = _dot_exact_rhs(y, ones) * inv_n
    yc = y - mu
    var = _dot_exact_rhs(yc * yc, ones) * inv_n
    yn = yc * lax.rsqrt(var + GN_EPS) * gnw_ref[...] + gnb_ref[...]
    v = v_ref[...]
    bonus = _dot_exact_rhs(r_ref[...] * k_ref[...] * rk_ref[...], ones) * v
    rw = (yn + bonus) * g_ref[...]
    mixed = (jnp.dot(attn_ref[...].astype(BF16), wo_ref[:ATTN_WIDTH, :], preferred_element_type=F32)
             + jnp.dot(rw.astype(BF16), wo_ref[ATTN_WIDTH:, :], preferred_element_type=F32))
    h = x_ref[...] + mixed
    h_out[...] = h
    xn = h * lax.rsqrt(jnp.mean(h * h, axis=-1, keepdims=True) + RMS_EPS) * nf_ref[...]
    xn_out[...] = xn
    logits = _dot_hi(xn, wr_ref[...]) + br_ref[...]
    eidx = lax.broadcasted_iota(I32, logits.shape, 1).astype(F32)
    lane = lax.broadcasted_iota(I32, (logits.shape[0], LANES), 1)
    idx_pad = jnp.zeros((logits.shape[0], LANES), F32)
    val_pad = jnp.full((logits.shape[0], LANES), MASK_VALUE, F32)
    cur = logits
    for kth in range(TOP_K):
        m = jnp.max(cur, axis=-1, keepdims=True)
        sel = jnp.min(jnp.where(cur == m, eidx, float(N_EXPERTS)), axis=-1, keepdims=True)
        idx_pad = jnp.where(lane == kth, sel, idx_pad)
        val_pad = jnp.where(lane == kth, m, val_pad)
        cur = jnp.where(eidx == sel, -jnp.inf, cur)
    top = jnp.max(val_pad, axis=-1, keepdims=True)
    e = jnp.exp(val_pad - top)
    gate_out[...] = e / jnp.sum(e, axis=-1, keepdims=True)
    idx_out[...] = idx_pad.astype(I32)


def _mix_out(attn, y, r, k, v, g, xf, r_k, gn_w, gn_b, w_out, norm_ffn_g, w_router, b_router):
    n = xf.shape[0]
    tm = 256
    w = RWKV_WIDTH
    row = lambda x: x.reshape(1, -1)
    half = pl.BlockSpec((tm, w), lambda i: (i, 0))
    wide = pl.BlockSpec((tm, D_MODEL), lambda i: (i, 0))
    pad = pl.BlockSpec((tm, LANES), lambda i: (i, 0))
    full = lambda shape: pl.BlockSpec(shape, lambda i: (0, 0))
    return pl.pallas_call(
        _mix_kernel,
        grid=(n // tm,),
        in_specs=[half] * 6 + [wide, full((1, w)), full((1, w)), full((1, w)),
                               full((D_MODEL, D_MODEL)), full((1, D_MODEL)),
                               full((D_MODEL, N_EXPERTS)), full((1, N_EXPERTS))],
        out_specs=[wide, wide, pad, pad],
        out_shape=[jax.ShapeDtypeStruct((n, D_MODEL), F32), jax.ShapeDtypeStruct((n, D_MODEL), F32),
                   jax.ShapeDtypeStruct((n, LANES), I32), jax.ShapeDtypeStruct((n, LANES), F32)],
        compiler_params=_params(("parallel",)),
        name="mix_out_router",
    )(attn, y, r, k, v, g, xf, row(r_k), row(gn_w), row(gn_b), w_out.astype(BF16),
      row(norm_ffn_g), w_router, row(b_router))


def _dispatch_kernel(dest_ref, x_ref, zeros_ref, out_ref, sem, *, tm):
    del zeros_ref

    def row_copy(j, kth):
        return pltpu.make_async_copy(
            x_ref.at[pl.ds(j, 1), :],
            out_ref.at[pl.ds(dest_ref[j * TOP_K + kth], 1), :],
            sem)

    def issue(j, carry):
        for kth in range(TOP_K):
            row_copy(j, kth).start()
        return carry

    lax.fori_loop(0, tm, issue, 0)
    for kth in range(TOP_K):
        pltpu.make_async_copy(x_ref, out_ref.at[pl.ds(0, tm), :], sem).wait()


def _dispatch(xn, dest_flat, n_rows):
    n = xn.shape[0]
    tm = 256
    zeros = jnp.zeros((n_rows, D_MODEL), F32)
    return pl.pallas_call(
        functools.partial(_dispatch_kernel, tm=tm),
        grid=(n // tm,),
        in_specs=[
            pl.BlockSpec((tm * TOP_K,), lambda i: (i,), memory_space=pltpu.SMEM),
            pl.BlockSpec((tm, D_MODEL), lambda i: (i, 0)),
            pl.BlockSpec(memory_space=pl.ANY),
        ],
        out_specs=pl.BlockSpec(memory_space=pl.ANY),
        out_shape=jax.ShapeDtypeStruct((n_rows, D_MODEL), F32),
        scratch_shapes=[pltpu.SemaphoreType.DMA],
        input_output_aliases={2: 0},
        compiler_params=_params(("arbitrary",)),
        name="moe_dispatch",
    )(dest_flat, xn, zeros)


def _expert_kernel(be_ref, nvalid_ref, x_ref, wgu_ref, bgu_ref, wd_ref, bd_ref, o_ref):
    i = pl.program_id(0)

    @pl.when(i < nvalid_ref[0])
    def _():
        x = x_ref[...].astype(BF16)
        acc = jnp.zeros(o_ref.shape, F32)
        cw = 256
        for j in range(D_FF // cw):
            g = jnp.dot(x, wgu_ref[0, :, j * cw:(j + 1) * cw], preferred_element_type=F32)
            g = g + bgu_ref[0, :, j * cw:(j + 1) * cw]
            u = jnp.dot(x, wgu_ref[0, :, D_FF + j * cw:D_FF + (j + 1) * cw],
                        preferred_element_type=F32)
            u = u + bgu_ref[0, :, D_FF + j * cw:D_FF + (j + 1) * cw]
            g = jnp.minimum(g, SWIGLU_LIMIT)
            u = jnp.clip(u, -SWIGLU_LIMIT, SWIGLU_LIMIT)
            act = (u + 1.0) * (g * _sigmoid(SWIGLU_ALPHA * g))
            acc = acc + jnp.dot(act.astype(BF16), wd_ref[0, j * cw:(j + 1) * cw, :],
                                preferred_element_type=F32)
        o_ref[...] = acc + bd_ref[0]

    @pl.when(i >= nvalid_ref[0])
    def _():
        o_ref[...] = jnp.zeros_like(o_ref)


def _experts(xs, blk_expert, n_valid, w_gate_up, b_gate_up, w_down, b_down, bm):
    n_rows = xs.shape[0]
    n_blocks = n_rows // bm
    return pl.pallas_call(
        _expert_kernel,
        grid_spec=pltpu.PrefetchScalarGridSpec(
            num_scalar_prefetch=2,
            grid=(n_blocks,),
            in_specs=[
                pl.BlockSpec((bm, D_MODEL), lambda i, be, nv: (i, 0)),
                pl.BlockSpec((1, D_MODEL, 2 * D_FF), lambda i, be, nv: (be[i], 0, 0)),
                pl.BlockSpec((1, 1, 2 * D_FF), lambda i, be, nv: (be[i], 0, 0)),
                pl.BlockSpec((1, D_FF, D_MODEL), lambda i, be, nv: (be[i], 0, 0)),
                pl.BlockSpec((1, 1, D_MODEL), lambda i, be, nv: (be[i], 0, 0)),
            ],
            out_specs=pl.BlockSpec((bm, D_MODEL), lambda i, be, nv: (i, 0)),
        ),
        out_shape=jax.ShapeDtypeStruct((n_rows, D_MODEL), F32),
        compiler_params=_params(("arbitrary",)),
        name="moe_experts",
    )(blk_expert, n_valid, xs, w_gate_up.astype(BF16), b_gate_up.reshape(N_EXPERTS, 1, 2 * D_FF),
      w_down.astype(BF16), b_down.reshape(N_EXPERTS, 1, D_MODEL))


def _combine_kernel(dest_ref, h_ref, gate_ref, g_ref, ys_ref, o_ref, buf, sem, *, tm):
    def issue(j, carry):
        for kth in range(TOP_K):
            pltpu.make_async_copy(
                ys_ref.at[pl.ds(dest_ref[j * TOP_K + kth], 1), :],
                buf.at[kth, pl.ds(j, 1), :],
                sem).start()
        return carry

    lax.fori_loop(0, tm, issue, 0)
    for kth in range(TOP_K):
        pltpu.make_async_copy(ys_ref.at[pl.ds(0, tm), :], buf.at[kth], sem).wait()
    gates = gate_ref[...]
    hf = h_ref[...]
    for kth in range(TOP_K):
        hf = hf + gates[:, kth:kth + 1] * buf[kth]
    o_ref[...] = hf * lax.rsqrt(jnp.mean(hf * hf, axis=-1, keepdims=True) + RMS_EPS) * g_ref[...]


def _combine(h, gate_pad, ys, dest_flat, norm_final_g):
    n = h.shape[0]
    tm = 256
    return pl.pallas_call(
        functools.partial(_combine_kernel, tm=tm),
        grid=(n // tm,),
        in_specs=[
            pl.BlockSpec((tm * TOP_K,), lambda i: (i,), memory_space=pltpu.SMEM),
            pl.BlockSpec((tm, D_MODEL), lambda i: (i, 0)),
            pl.BlockSpec((tm, LANES), lambda i: (i, 0)),
            pl.BlockSpec((1, D_MODEL), lambda i: (0, 0)),
            pl.BlockSpec(memory_space=pl.ANY),
        ],
        out_specs=pl.BlockSpec((tm, D_MODEL), lambda i: (i, 0)),
        out_shape=jax.ShapeDtypeStruct((n, D_MODEL), F32),
        scratch_shapes=[pltpu.VMEM((TOP_K, tm, D_MODEL), F32), pltpu.SemaphoreType.DMA],
        compiler_params=_params(("arbitrary",)),
        name="moe_combine_norm",
    )(dest_flat, h, gate_pad, norm_final_g.reshape(1, D_MODEL), ys)


def _routing(top_idx, bm):
    n = top_idx.shape[0]
    onehot = jnp.sum((top_idx[:, :, None] == jnp.arange(N_EXPERTS, dtype=I32)).astype(I32), axis=1)
    csum = jnp.cumsum(onehot, axis=0)
    counts = csum[-1]
    rank = jnp.take_along_axis(csum - onehot, top_idx, axis=1)
    padded = (counts + bm - 1) // bm * bm
    pend = jnp.cumsum(padded)
    pstart = pend - padded
    dest = pstart[top_idx] + rank
    n_blocks = (n * TOP_K + N_EXPERTS * (bm - 1) + bm - 1) // bm
    blk_expert = jnp.minimum(
        jnp.searchsorted(pend, jnp.arange(n_blocks, dtype=pend.dtype) * bm, side='right'),
        N_EXPERTS - 1).astype(I32)
    n_valid = (pend[-1:] // bm).astype(I32)
    return dest.reshape(n * TOP_K).astype(I32), blk_expert, n_valid, n_blocks


def _layer(xf, b, s, p):
    qkv, rest = _in_proj(xf, p["norm_mix_g"], p["w_in"])
    slopes = jnp.exp2(-8.0 / ATTN_HEADS * jnp.arange(1, ATTN_HEADS + 1, dtype=F32))
    attn = _attention(qkv.reshape(b, s, 3 * ATTN_WIDTH), slopes)
    r, lw, k, v, kk, a, g = _rwkv_prep(
        rest.reshape(b, s, SHIFT_WIDTH), p["shift_mu"], p["decay_w0"], p["w_decay_up"],
        p["iclr_a0"], p["w_iclr_up"], p["w_gate_lr_up"], p["k_k"], p["k_a"])
    y = _rwkv_scan(r, lw, k, v, kk, a)
    flat = lambda t: t.reshape(b * s, -1)
    h, xn, idx_pad, gate_pad = _mix_out(
        flat(attn), flat(y), flat(r), flat(k), flat(v), flat(g), xf, p["r_k"].reshape(-1),
        p["gn_w"], p["gn_b"], p["w_out"], p["norm_ffn_g"], p["w_router"], p["b_router"])
    bm = 512
    dest, blk_expert, n_valid, n_blocks = _routing(idx_pad[:, :TOP_K], bm)
    xs = _dispatch(xn, dest, n_blocks * bm)
    ys = _experts(xs, blk_expert, n_valid, p["w_gate_up"], p["b_gate_up"], p["w_down"],
                  p["b_down"], bm)
    return h, gate_pad, ys, dest


def kernel(x, norm_mix_g, w_in, shift_mu, decay_w0, w_decay_up, iclr_a0, w_iclr_up, w_gate_lr_up,
           k_k, k_a, r_k, gn_w, gn_b, w_out, norm_ffn_g, w_router, b_router, w_gate_up,
           b_gate_up, w_down, b_down, norm_final_g):
    b, s, d = x.shape
    assert d == D_MODEL and w_in.shape[0] == 1, "single-layer block"
    names = ("norm_mix_g", "w_in", "shift_mu", "decay_w0", "w_decay_up", "iclr_a0", "w_iclr_up",
             "w_gate_lr_up", "k_k", "k_a", "r_k", "gn_w", "gn_b", "w_out", "norm_ffn_g",
             "w_router", "b_router", "w_gate_up", "b_gate_up", "w_down", "b_down")
    vals = (norm_mix_g, w_in, shift_mu, decay_w0, w_decay_up, iclr_a0, w_iclr_up, w_gate_lr_up,
            k_k, k_a, r_k, gn_w, gn_b, w_out, norm_ffn_g, w_router, b_router, w_gate_up,
            b_gate_up, w_down, b_down)
    p = {nm: v[0] for nm, v in zip(names, vals)}
    h, gate_pad, ys, dest = _layer(x.reshape(b * s, d), b, s, p)
    out = _combine(h, gate_pad, ys, dest, norm_final_g)
    return out.reshape(b, s, d)
```

```python
import functools
import math

import jax
import jax.numpy as jnp
from jax import lax
from jax.experimental import pallas as pl
from jax.experimental.pallas import tpu as pltpu

F32 = jnp.float32
BF16 = jnp.bfloat16
I32 = jnp.int32

D_MODEL = 1024
HEAD_DIM = 64
ATTN_WIDTH = 512
ATTN_HEADS = ATTN_WIDTH // HEAD_DIM
RWKV_WIDTH = 512
RWKV_HEADS = RWKV_WIDTH // HEAD_DIM
DILATED_PATTERNS = ((128, 1), (512, 4), (2048, 16))
ATTN_BLOCK = 128
DECAY_LORA = 64
ICLR_LORA = 64
GATE_LORA = 128
SHIFT_WIDTH = 3 * RWKV_WIDTH + DECAY_LORA + ICLR_LORA + GATE_LORA
N_EXPERTS = 32
TOP_K = 4
D_FF = D_MODEL
SWIGLU_LIMIT = 7.0
SWIGLU_ALPHA = 1.702
RMS_EPS = 1e-5
GN_EPS = 64e-5

LANES = 128
SUBLANES = 8
CHUNK = 64
MASK_VALUE = -1e30
LOG2_E = 1.4426950408889634
ATTN_GROUP = 4
VMEM_LIMIT = 56 * 1024 * 1024


def _params(semantics, vmem=VMEM_LIMIT):
    return pltpu.CompilerParams(dimension_semantics=semantics, vmem_limit_bytes=vmem)


def _dot(a, b):
    return jnp.dot(a.astype(BF16), b.astype(BF16), preferred_element_type=F32)


def _dot_nt(a, b):
    return lax.dot_general(a.astype(BF16), b.astype(BF16), (((1,), (1,)), ((), ())),
                           preferred_element_type=F32)


def _dot_tn(a, b):
    return lax.dot_general(a.astype(BF16), b.astype(BF16), (((0,), (0,)), ((), ())),
                           preferred_element_type=F32)


def _split3(x):
    h = x.astype(BF16)
    r1 = x - h.astype(F32)
    m = r1.astype(BF16)
    l = (r1 - m.astype(F32)).astype(BF16)
    return h, m, l


def _dot_exact_lhs(a_bf16, x):
    h, m, l = _split3(x)
    d = lambda y: jnp.dot(a_bf16, y, preferred_element_type=F32)
    return d(h) + d(m) + d(l)


def _dot_exact_rhs(x, b_bf16):
    h, m, l = _split3(x)
    d = lambda y: jnp.dot(y, b_bf16, preferred_element_type=F32)
    return d(h) + d(m) + d(l)


def _dot_hi(a, b):
    ah = a.astype(BF16)
    al = (a - ah.astype(F32)).astype(BF16)
    bh = b.astype(BF16)
    bl = (b - bh.astype(F32)).astype(BF16)
    d = lambda x, y: jnp.dot(x, y, preferred_element_type=F32)
    return d(ah, bh) + d(ah, bl) + d(al, bh)


def _sigmoid(x):
    return 1.0 / (1.0 + jnp.exp(-x))


def _head_ones(width):
    r = lax.broadcasted_iota(I32, (width, width), 0) // HEAD_DIM
    c = lax.broadcasted_iota(I32, (width, width), 1) // HEAD_DIM
    return jnp.where(r == c, 1.0, 0.0).astype(BF16)


def _inproj_kernel(x_ref, g_ref, wq_ref, wr_ref, qkv_ref, rest_ref):
    x = x_ref[...]
    xn = x * lax.rsqrt(jnp.mean(x * x, axis=-1, keepdims=True) + RMS_EPS) * g_ref[...]
    xb = xn.astype(BF16)
    qkv_ref[...] = jnp.dot(xb, wq_ref[...], preferred_element_type=F32)
    rest_ref[...] = jnp.dot(xb, wr_ref[...], preferred_element_type=F32)


def _in_proj(xf, g, w_in):
    n = xf.shape[0]
    tm = 256
    wq = w_in[:, :3 * ATTN_WIDTH].astype(BF16)
    wr = w_in[:, 3 * ATTN_WIDTH:].astype(BF16)
    return pl.pallas_call(
        _inproj_kernel,
        grid=(n // tm,),
        in_specs=[
            pl.BlockSpec((tm, D_MODEL), lambda i: (i, 0)),
            pl.BlockSpec((1, D_MODEL), lambda i: (0, 0)),
            pl.BlockSpec((D_MODEL, 3 * ATTN_WIDTH), lambda i: (0, 0)),
            pl.BlockSpec((D_MODEL, SHIFT_WIDTH), lambda i: (0, 0)),
        ],
        out_specs=[
            pl.BlockSpec((tm, 3 * ATTN_WIDTH), lambda i: (i, 0)),
            pl.BlockSpec((tm, SHIFT_WIDTH), lambda i: (i, 0)),
        ],
        out_shape=[
            jax.ShapeDtypeStruct((n, 3 * ATTN_WIDTH), F32),
            jax.ShapeDtypeStruct((n, SHIFT_WIDTH), F32),
        ],
        compiler_params=_params(("parallel",)),
        name="in_proj",
    )(xf, g.reshape(1, D_MODEL), wq, wr)


def _attn_kernel(slopes_ref, q_ref, k_ref, v_ref, o_ref, bias_ref, m_acc, d_acc, n_acc, *, seq):
    c = ATTN_BLOCK
    hp = pl.program_id(1)
    lane = lax.broadcasted_iota(I32, (c, LANES), 1)
    lo_half = lane < HEAD_DIM
    qi = lax.broadcasted_iota(I32, (c, 2 * c), 0)
    kj = lax.broadcasted_iota(I32, (c, 2 * c), 1)
    diff = qi + c - kj
    for p, (window, dil) in enumerate(DILATED_PATTERNS):
        steps = window // dil
        valid = (diff >= 0) & (diff <= steps)
        dist = (dil * diff).astype(F32)
        for hh in range(2):
            slope = slopes_ref[hp * 2 + hh] * LOG2_E
            bias = jnp.where(valid, -slope * dist, MASK_VALUE)
            bias_ref[2 * (p * 2 + hh)] = bias
            bias_ref[2 * (p * 2 + hh) + 1] = jnp.where(kj >= c, bias, MASK_VALUE)

    scale = LOG2_E / math.sqrt(HEAD_DIM)
    order = sorted(range(len(DILATED_PATTERNS)), key=lambda p: -DILATED_PATTERNS[p][1])

    def rows(start, dil):
        if dil == 1:
            return pl.ds(pl.multiple_of(start, c), c)
        return pl.ds(start, c, stride=dil)

    def group(i, carry, p, dil, nb):
        blocks = []
        for g in range(ATTN_GROUP):
            j = i * ATTN_GROUP + g
            if nb >= ATTN_GROUP:
                r, n = j // nb, j % nb
                first = (n == 0) if g == 0 else False
            else:
                r, n = j // nb, g % nb
                first = n == 0
            blocks.append((r + dil * c * n, r + dil * c * jnp.maximum(n - 1, 0), first))
        qs, ks, vs = [], [], []
        for cur, prev, first in blocks:
            qs.append(q_ref[0, rows(cur, dil), :] * scale)
            kc = k_ref[0, rows(cur, dil), :].astype(BF16)
            vc = v_ref[0, rows(cur, dil), :].astype(BF16)
            if first is True:
                ks.append(kc)
                vs.append(vc)
            else:
                ks.append(jnp.concatenate([k_ref[0, rows(prev, dil), :].astype(BF16), kc], axis=0))
                vs.append(jnp.concatenate([v_ref[0, rows(prev, dil), :].astype(BF16), vc], axis=0))
        scores = []
        for (cur, prev, first), qf, kcat in zip(blocks, qs, ks):
            for hh in range(2):
                head_mask = lo_half if hh == 0 else jnp.logical_not(lo_half)
                qh = jnp.where(head_mask, qf, 0.0).astype(BF16)
                s = lax.dot_general(qh, kcat, (((1,), (1,)), ((), ())), preferred_element_type=F32)
                slot = 2 * (p * 2 + hh)
                if first is True:
                    s = s + bias_ref[slot][:, c:]
                elif first is False:
                    s = s + bias_ref[slot]
                else:
                    s = s + bias_ref[slot + first.astype(I32)]
                scores.append(s)
        probs = []
        for s in scores:
            m = jnp.max(s, axis=-1, keepdims=True)
            e = jnp.exp2(s - m)
            probs.append((m, jnp.sum(e, axis=-1, keepdims=True), e.astype(BF16)))
        for bi, (cur, prev, first) in enumerate(blocks):
            (m0, d0, e0), (m1, d1, e1) = probs[2 * bi], probs[2 * bi + 1]
            o0 = jnp.dot(e0, vs[bi], preferred_element_type=F32)
            o1 = jnp.dot(e1, vs[bi], preferred_element_type=F32)
            m_b = jnp.where(lo_half, m0, m1)
            d_b = jnp.where(lo_half, d0, d1)
            n_b = jnp.where(lo_half, o0, o1)
            sl = rows(cur, dil)
            if p == order[0]:
                m_acc[sl, :] = m_b
                d_acc[sl, :] = d_b
                n_acc[sl, :] = n_b
            else:
                m_o = m_acc[sl, :]
                m_n = jnp.maximum(m_o, m_b)
                a_o = jnp.exp2(m_o - m_n)
                a_b = jnp.exp2(m_b - m_n)
                d_n = d_acc[sl, :] * a_o + d_b * a_b
                n_n = n_acc[sl, :] * a_o + n_b * a_b
                if p == order[-1]:
                    o_ref[0, sl, :] = n_n / d_n
                else:
                    d_acc[sl, :] = d_n
                    n_acc[sl, :] = n_n
                    m_acc[sl, :] = m_n
        return carry

    for p in order:
        dil = DILATED_PATTERNS[p][1]
        nb = seq // (c * dil)
        n_groups = (nb * dil) // ATTN_GROUP
        lax.fori_loop(0, n_groups, functools.partial(group, p=p, dil=dil, nb=nb), 0)


def _attention(qkv, slopes):
    b, s, _ = qkv.shape
    for window, dil in DILATED_PATTERNS:
        nb = s // (ATTN_BLOCK * dil)
        assert s % (ATTN_BLOCK * dil) == 0 and window // dil <= ATTN_BLOCK
        assert nb % ATTN_GROUP == 0 or (ATTN_GROUP % nb == 0 and (nb * dil) % ATTN_GROUP == 0)
    n_pairs = ATTN_WIDTH // LANES
    blk = lambda off: pl.BlockSpec((1, s, LANES), lambda bi, hp, sl: (bi, 0, off + hp))
    return pl.pallas_call(
        functools.partial(_attn_kernel, seq=s),
        grid_spec=pltpu.PrefetchScalarGridSpec(
            num_scalar_prefetch=1,
            grid=(b, n_pairs),
            in_specs=[blk(0), blk(n_pairs), blk(2 * n_pairs)],
            out_specs=pl.BlockSpec((1, s, LANES), lambda bi, hp, sl: (bi, 0, hp)),
            scratch_shapes=[
                pltpu.VMEM((4 * len(DILATED_PATTERNS), ATTN_BLOCK, 2 * ATTN_BLOCK), F32),
                pltpu.VMEM((s, LANES), F32),
                pltpu.VMEM((s, LANES), F32),
                pltpu.VMEM((s, LANES), F32),
            ],
        ),
        out_shape=jax.ShapeDtypeStruct((b, s, ATTN_WIDTH), F32),
        compiler_params=_params(("parallel", "parallel")),
        name="dilated_attention",
    )(slopes, qkv, qkv, qkv)


def _seg_sum(x, ones_ref):
    hw = ones_ref.shape[0]
    h = x.astype(BF16)
    l = (x - h.astype(F32)).astype(BF16)
    parts = []
    for j in range(x.shape[1] // hw):
        sl = slice(j * hw, (j + 1) * hw)
        parts.append(jnp.dot(h[:, sl], ones_ref[...], preferred_element_type=F32)
                     + jnp.dot(l[:, sl], ones_ref[...], preferred_element_type=F32))
    return jnp.concatenate(parts, axis=1)


def _rwkv_kernel(rest_ref, mu_ref, w0_ref, wdu_ref, a0_ref, wau_ref, wg_ref, kk_ref, ka_ref,
                 rk_ref, gnw_ref, gnb_ref, o_ref, st_ref, prev_ref, tri_ref, ones_ref, *, n_chunks):
    l = CHUNK
    pw = LANES
    n_pairs = RWKV_WIDTH // pw
    tc = n_chunks * l
    w = RWKV_WIDTH

    @pl.when(pl.program_id(1) == 0)
    def _():
        st_ref[...] = jnp.zeros_like(st_ref)
        prev_ref[...] = jnp.zeros_like(prev_ref)
        ri = lax.broadcasted_iota(I32, (tc, tc), 0)
        ci = lax.broadcasted_iota(I32, (tc, tc), 1)
        tri_ref[...] = jnp.where((ri >= ci) & (ri // l == ci // l), 1.0, 0.0).astype(BF16)
        hr = lax.broadcasted_iota(I32, ones_ref.shape, 0) // HEAD_DIM
        hc = lax.broadcasted_iota(I32, ones_ref.shape, 1) // HEAD_DIM
        ones_ref[...] = jnp.where(hr == hc, 1.0, 0.0).astype(BF16)

    z = rest_ref[0]
    zrow = lax.broadcasted_iota(I32, z.shape, 0)
    zprev = jnp.where(zrow == 0, prev_ref[...], pltpu.roll(z, 1, 0))
    prev_ref[...] = z[tc - 1:tc, :]
    xs = z + (zprev - z) * mu_ref[...]
    r = xs[:, :w]
    k_in = xs[:, w:2 * w]
    v = xs[:, 2 * w:3 * w]
    wd = xs[:, 3 * w:3 * w + DECAY_LORA]
    ad = xs[:, 3 * w + DECAY_LORA:3 * w + DECAY_LORA + ICLR_LORA]
    gd = xs[:, 3 * w + DECAY_LORA + ICLR_LORA:]
    zz = w0_ref[...] + _dot_hi(jnp.tanh(wd), wdu_ref[...])
    lw = -jnp.exp(jnp.minimum(zz, 0.0) - jnp.log(1.0 + jnp.exp(-jnp.abs(zz))) - 0.5)
    a = _sigmoid(a0_ref[...] + _dot(ad, wau_ref[...]))
    gate = _dot(_sigmoid(gd), wg_ref[...])
    kk = k_in * kk_ref[...]
    kk = kk / jnp.maximum(jnp.sqrt(_seg_sum(kk * kk, ones_ref)), 1e-12)
    k = k_in * (1.0 + (a - 1.0) * ka_ref[...])
    cum = _dot_exact_lhs(tri_ref[...], lw)
    tot = jnp.concatenate(
        [jnp.broadcast_to(cum[c * l + l - 1:c * l + l, :], (l, w)) for c in range(n_chunks)], axis=0)
    p_in = jnp.exp(cum)
    p_ex = jnp.exp(cum - lw)
    p_inv = jnp.exp(-cum)
    p_end = jnp.exp(tot - cum)
    p_tot = jnp.exp(tot)
    kka = kk * a
    al = -kk * p_ex
    be = kka * p_inv
    kt = k * p_inv
    rt = r * p_in
    bh = kka * p_end
    kh = k * p_end

    row = lax.broadcasted_iota(I32, (l, pw), 0)
    col = lax.broadcasted_iota(I32, (l, pw), 1) % l
    strict = row > col
    incl = row >= col
    eye_ss = jnp.where(row == col, 1.0, 0.0).astype(F32)
    row_bd = lax.broadcasted_iota(I32, (2 * l, pw), 0)
    lane_bd = lax.broadcasted_iota(I32, (2 * l, pw), 1)
    diag_mask = (row_bd // l) == (lane_bd // l)
    eye_bd = row_bd == lane_bd

    def bd(x):
        return jnp.where(diag_mask, jnp.concatenate([x, x], axis=0), 0.0)

    items = [(c, p) for c in range(n_chunks) for p in range(n_pairs)]
    cut = lambda t, c, p: t[c * l:(c + 1) * l, p * pw:(p + 1) * pw]

    a_ab, a_rb, a_ak, a_rk, akv = {}, {}, {}, {}, {}
    for it in items:
        lhs = jnp.concatenate([cut(al, *it), cut(rt, *it)], axis=0)
        g1 = _dot_nt(lhs, bd(cut(be, *it)))
        g2 = _dot_nt(lhs, bd(cut(kt, *it)))
        a_ab[it] = jnp.where(strict, g1[:l], 0.0)
        a_rb[it] = jnp.where(incl, g1[l:], 0.0)
        a_ak[it] = jnp.where(strict, g2[:l], 0.0)
        a_rk[it] = jnp.where(incl, g2[l:], 0.0)
    for it in items:
        akv[it] = _dot(a_ak[it], bd(cut(v, *it)))
    xs = dict(a_ab)
    ts = {it: eye_ss + a_ab[it] for it in items}
    for _ in range(int(math.log2(l)) - 1):
        for it in items:
            xs[it] = _dot(xs[it], bd(xs[it]))
        for it in items:
            ts[it] = ts[it] + _dot(ts[it], bd(xs[it]))
    w_t, u0 = {}, {}
    for it in items:
        w_t[it] = _dot(ts[it], bd(cut(al, *it)))
        u0[it] = _dot(ts[it], bd(akv[it]))
    r_hat, y0, m_bd, c_bd = {}, {}, {}, {}
    for it in items:
        v_i = cut(v, *it)
        bh_i = cut(bh, *it)
        r_hat[it] = cut(rt, *it) + _dot(a_rb[it], bd(w_t[it]))
        y0[it] = _dot(jnp.concatenate([a_rb[it], a_rk[it]], axis=1),
                      jnp.concatenate([bd(u0[it]), bd(v_i)], axis=0))
        decay = jnp.broadcast_to(cut(p_tot, *it)[:1], (2 * l, pw))
        m_bd[it] = jnp.where(diag_mask, _dot_tn(bh_i, w_t[it]), 0.0) + jnp.where(eye_bd, decay, 0.0)
        c_bd[it] = jnp.where(
            diag_mask,
            _dot_tn(jnp.concatenate([bh_i, cut(kh, *it)], axis=0),
                    jnp.concatenate([u0[it], v_i], axis=0)),
            0.0)
    st = [st_ref[p] for p in range(n_pairs)]
    for c in range(n_chunks):
        for p in range(n_pairs):
            it = (c, p)
            o_ref[0, c * l:(c + 1) * l, p * pw:(p + 1) * pw] = _dot(r_hat[it], st[p]) + y0[it]
            st[p] = _dot(m_bd[it], st[p]) + c_bd[it]
    for p in range(n_pairs):
        st_ref[p] = st[p]

    y = o_ref[0]
    inv_n = 1.0 / HEAD_DIM
    mean = _seg_sum(y, ones_ref) * inv_n
    yc = y - mean
    var = _seg_sum(yc * yc, ones_ref) * inv_n
    yn = yc * lax.rsqrt(var + GN_EPS) * gnw_ref[...] + gnb_ref[...]
    bonus = _seg_sum(r * k * rk_ref[...], ones_ref) * v
    o_ref[0] = (yn + bonus) * gate


def _rwkv_mix(rest, shift_mu, decay_w0, w_decay_up, iclr_a0, w_iclr_up, w_gate_lr_up, k_k, k_a,
              r_k, gn_w, gn_b):
    b, s, _ = rest.shape
    w = RWKV_WIDTH
    tc = 256
    assert s % tc == 0
    row = lambda x: x.reshape(1, -1)
    full = lambda shape: pl.BlockSpec(shape, lambda bi, i: (0, 0))
    return pl.pallas_call(
        functools.partial(_rwkv_kernel, n_chunks=tc // CHUNK),
        grid=(b, s // tc),
        in_specs=[
            pl.BlockSpec((1, tc, SHIFT_WIDTH), lambda bi, i: (bi, i, 0)),
            full((1, SHIFT_WIDTH)), full((1, w)), full((DECAY_LORA, w)), full((1, w)),
            full((ICLR_LORA, w)), full((GATE_LORA, w)), full((1, w)), full((1, w)),
            full((1, w)), full((1, w)), full((1, w)),
        ],
        out_specs=pl.BlockSpec((1, tc, w), lambda bi, i: (bi, i, 0)),
        out_shape=jax.ShapeDtypeStruct((b, s, w), F32),
        scratch_shapes=[
            pltpu.VMEM((w // LANES, LANES, LANES), F32),
            pltpu.VMEM((1, SHIFT_WIDTH), F32),
            pltpu.VMEM((tc, tc), BF16),
            pltpu.VMEM((2 * LANES, 2 * LANES), BF16),
        ],
        compiler_params=_params(("parallel", "arbitrary")),
        name="rwkv7_mix",
    )(rest, row(shift_mu), row(decay_w0), w_decay_up, row(iclr_a0), w_iclr_up, w_gate_lr_up,
      row(k_k), row(k_a), row(r_k), row(gn_w), row(gn_b))


def _mix_kernel(attn_ref, rw_ref, x_ref, wo_ref, nf_ref, wr_ref, br_ref,
                h_out, xn_out, idx_out, gate_out):
    mixed = (jnp.dot(attn_ref[...].astype(BF16), wo_ref[:ATTN_WIDTH, :], preferred_element_type=F32)
             + jnp.dot(rw_ref[...].astype(BF16), wo_ref[ATTN_WIDTH:, :], preferred_element_type=F32))
    h = x_ref[...] + mixed
    h_out[...] = h
    xn = h * lax.rsqrt(jnp.mean(h * h, axis=-1, keepdims=True) + RMS_EPS) * nf_ref[...]
    xn_out[...] = xn
    logits = _dot_hi(xn, wr_ref[...]) + br_ref[...]
    eidx = lax.broadcasted_iota(I32, logits.shape, 1).astype(F32)
    lane = lax.broadcasted_iota(I32, (logits.shape[0], LANES), 1)
    idx_pad = jnp.zeros((logits.shape[0], LANES), F32)
    val_pad = jnp.full((logits.shape[0], LANES), MASK_VALUE, F32)
    cur = logits
    for kth in range(TOP_K):
        m = jnp.max(cur, axis=-1, keepdims=True)
        sel = jnp.min(jnp.where(cur == m, eidx, float(N_EXPERTS)), axis=-1, keepdims=True)
        idx_pad = jnp.where(lane == kth, sel, idx_pad)
        val_pad = jnp.where(lane == kth, m, val_pad)
        cur = jnp.where(eidx == sel, -jnp.inf, cur)
    top = jnp.max(val_pad, axis=-1, keepdims=True)
    e = jnp.exp(val_pad - top)
    gate_out[...] = e / jnp.sum(e, axis=-1, keepdims=True)
    idx_out[...] = idx_pad.astype(I32)


def _mix_out(attn, rw, xf, w_out, norm_ffn_g, w_router, b_router):
    n = xf.shape[0]
    tm = 256
    row = lambda x: x.reshape(1, -1)
    half = pl.BlockSpec((tm, RWKV_WIDTH), lambda i: (i, 0))
    wide = pl.BlockSpec((tm, D_MODEL), lambda i: (i, 0))
    pad = pl.BlockSpec((tm, LANES), lambda i: (i, 0))
    full = lambda shape: pl.BlockSpec(shape, lambda i: (0, 0))
    return pl.pallas_call(
        _mix_kernel,
        grid=(n // tm,),
        in_specs=[half, half, wide, full((D_MODEL, D_MODEL)), full((1, D_MODEL)),
                  full((D_MODEL, N_EXPERTS)), full((1, N_EXPERTS))],
        out_specs=[wide, wide, pad, pad],
        out_shape=[jax.ShapeDtypeStruct((n, D_MODEL), F32), jax.ShapeDtypeStruct((n, D_MODEL), F32),
                   jax.ShapeDtypeStruct((n, LANES), I32), jax.ShapeDtypeStruct((n, LANES), F32)],
        compiler_params=_params(("parallel",)),
        name="mix_out_router",
    )(attn, rw, xf, w_out.astype(BF16), row(norm_ffn_g), w_router, row(b_router))


def _dispatch_kernel(pad_ref, nvalid_ref, dest_ref, x_ref, out_ref, zbuf, sem, zsem,
                     *, tm, bm, n_blocks):
    @pl.when(pl.program_id(0) == 0)
    def _():
        zbuf[...] = jnp.zeros_like(zbuf)

        def fill(start):
            cp = pltpu.make_async_copy(zbuf, out_ref.at[pl.ds(start, bm), :], zsem)
            cp.start()
            cp.wait()

        def pad_fill(e, carry):
            start = pad_ref[e]
            for q in range(SUBLANES - 1):
                cp = pltpu.make_async_copy(zbuf.at[pl.ds(0, 1), :],
                                           out_ref.at[pl.ds(start + q, 1), :], zsem)
                cp.start()
                cp.wait()
            aligned = (start + (SUBLANES - 1)) // SUBLANES * SUBLANES
            fill(pl.multiple_of(aligned, SUBLANES))
            return carry

        def tail_fill(blk, carry):
            fill(pl.multiple_of(blk * bm, bm))
            return carry

        lax.fori_loop(0, N_EXPERTS, pad_fill, 0)
        lax.fori_loop(nvalid_ref[0], n_blocks + 1, tail_fill, 0)

    def issue(jj, carry):
        base = pl.multiple_of(jj * SUBLANES, SUBLANES)
        for q in range(SUBLANES):
            for kth in range(TOP_K):
                pltpu.make_async_copy(
                    x_ref.at[pl.ds(base + q, 1), :],
                    out_ref.at[pl.ds(dest_ref[(base + q) * TOP_K + kth], 1), :],
                    sem).start(priority=kth % 2)
        return carry

    lax.fori_loop(0, tm // SUBLANES, issue, 0)
    for kth in range(TOP_K):
        pltpu.make_async_copy(x_ref, out_ref.at[pl.ds(0, tm), :], sem).wait()


def _dispatch(xn, dest_flat, pad_start, n_valid, n_blocks, bm):
    n = xn.shape[0]
    tm = 256
    return pl.pallas_call(
        functools.partial(_dispatch_kernel, tm=tm, bm=bm, n_blocks=n_blocks),
        grid_spec=pltpu.PrefetchScalarGridSpec(
            num_scalar_prefetch=2,
            grid=(n // tm,),
            in_specs=[
                pl.BlockSpec((tm * TOP_K,), lambda i, ps, nv: (i,), memory_space=pltpu.SMEM),
                pl.BlockSpec((tm, D_MODEL), lambda i, ps, nv: (i, 0)),
            ],
            out_specs=pl.BlockSpec(memory_space=pl.ANY),
            scratch_shapes=[pltpu.VMEM((bm, D_MODEL), F32), pltpu.SemaphoreType.DMA,
                            pltpu.SemaphoreType.DMA],
        ),
        out_shape=jax.ShapeDtypeStruct(((n_blocks + 1) * bm, D_MODEL), F32),
        compiler_params=_params(("arbitrary",)),
        name="moe_dispatch",
    )(pad_start, n_valid, dest_flat, xn)


def _expert_kernel(be_ref, nvalid_ref, x_ref, wgu_ref, bgu_ref, wd_ref, bd_ref, o_ref):
    i = pl.program_id(0)

    @pl.when(i < nvalid_ref[0])
    def _():
        x = x_ref[...].astype(BF16)
        acc = jnp.zeros(o_ref.shape, F32)
        cw = 256
        for j in range(D_FF // cw):
            g = jnp.dot(x, wgu_ref[0, :, j * cw:(j + 1) * cw], preferred_element_type=F32)
            g = g + bgu_ref[0, :, j * cw:(j + 1) * cw]
            u = jnp.dot(x, wgu_ref[0, :, D_FF + j * cw:D_FF + (j + 1) * cw],
                        preferred_element_type=F32)
            u = u + bgu_ref[0, :, D_FF + j * cw:D_FF + (j + 1) * cw]
            g = jnp.minimum(g, SWIGLU_LIMIT)
            u = jnp.clip(u, -SWIGLU_LIMIT, SWIGLU_LIMIT)
            act = (u + 1.0) * (g * _sigmoid(SWIGLU_ALPHA * g))
            acc = acc + jnp.dot(act.astype(BF16), wd_ref[0, j * cw:(j + 1) * cw, :],
                                preferred_element_type=F32)
        o_ref[...] = acc + bd_ref[0]

    @pl.when(i >= nvalid_ref[0])
    def _():
        o_ref[...] = jnp.zeros_like(o_ref)


def _experts(xs, blk_expert, n_valid, w_gate_up, b_gate_up, w_down, b_down, n_blocks, bm):
    n_rows = n_blocks * bm
    return pl.pallas_call(
        _expert_kernel,
        grid_spec=pltpu.PrefetchScalarGridSpec(
            num_scalar_prefetch=2,
            grid=(n_blocks,),
            in_specs=[
                pl.BlockSpec((bm, D_MODEL), lambda i, be, nv: (i, 0)),
                pl.BlockSpec((1, D_MODEL, 2 * D_FF), lambda i, be, nv: (be[i], 0, 0)),
                pl.BlockSpec((1, 1, 2 * D_FF), lambda i, be, nv: (be[i], 0, 0)),
                pl.BlockSpec((1, D_FF, D_MODEL), lambda i, be, nv: (be[i], 0, 0)),
                pl.BlockSpec((1, 1, D_MODEL), lambda i, be, nv: (be[i], 0, 0)),
            ],
            out_specs=pl.BlockSpec((bm, D_MODEL), lambda i, be, nv: (i, 0)),
        ),
        out_shape=jax.ShapeDtypeStruct((n_rows, D_MODEL), F32),
        compiler_params=_params(("arbitrary",)),
        name="moe_experts",
    )(blk_expert, n_valid, xs, w_gate_up.astype(BF16), b_gate_up.reshape(N_EXPERTS, 1, 2 * D_FF),
      w_down.astype(BF16), b_down.reshape(N_EXPERTS, 1, D_MODEL))


def _combine_kernel(dest_ref, dest_next_ref, h_ref, gate_ref, g_ref, ys_ref, o_ref, buf, sem,
                    *, tm, n_tiles):
    i = pl.program_id(0)
    slot = i % 2

    def issue(dref, s):
        def body(jj, carry):
            base = pl.multiple_of(jj * SUBLANES, SUBLANES)
            for q in range(SUBLANES):
                for kth in range(TOP_K):
                    pltpu.make_async_copy(
                        ys_ref.at[pl.ds(dref[(base + q) * TOP_K + kth], 1), :],
                        buf.at[s, kth, pl.ds(base + q, 1), :],
                        sem.at[s]).start(priority=kth % 2)
            return carry

        lax.fori_loop(0, tm // SUBLANES, body, 0)

    @pl.when(i == 0)
    def _():
        issue(dest_ref, slot)

    @pl.when(i + 1 < n_tiles)
    def _():
        issue(dest_next_ref, 1 - slot)

    for kth in range(TOP_K):
        pltpu.make_async_copy(ys_ref.at[pl.ds(0, tm), :], buf.at[slot, kth], sem.at[slot]).wait()
    gates = gate_ref[...]
    hf = h_ref[...]
    for kth in range(TOP_K):
        hf = hf + gates[:, kth:kth + 1] * buf[slot, kth]
    o_ref[...] = hf * lax.rsqrt(jnp.mean(hf * hf, axis=-1, keepdims=True) + RMS_EPS) * g_ref[...]


def _combine(h, gate_pad, ys, dest_flat, norm_final_g):
    n = h.shape[0]
    tm = 256
    n_tiles = n // tm
    return pl.pallas_call(
        functools.partial(_combine_kernel, tm=tm, n_tiles=n_tiles),
        grid=(n_tiles,),
        in_specs=[
            pl.BlockSpec((tm * TOP_K,), lambda i: (i,), memory_space=pltpu.SMEM),
            pl.BlockSpec((tm * TOP_K,), lambda i: (jnp.minimum(i + 1, n_tiles - 1),),
                         memory_space=pltpu.SMEM),
            pl.BlockSpec((tm, D_MODEL), lambda i: (i, 0)),
            pl.BlockSpec((tm, LANES), lambda i: (i, 0)),
            pl.BlockSpec((1, D_MODEL), lambda i: (0, 0)),
            pl.BlockSpec(memory_space=pl.ANY),
        ],
        out_specs=pl.BlockSpec((tm, D_MODEL), lambda i: (i, 0)),
        out_shape=jax.ShapeDtypeStruct((n, D_MODEL), F32),
        scratch_shapes=[pltpu.VMEM((2, TOP_K, tm, D_MODEL), F32), pltpu.SemaphoreType.DMA((2,))],
        compiler_params=_params(("arbitrary",)),
        name="moe_combine_norm",
    )(dest_flat, dest_flat, h, gate_pad, norm_final_g.reshape(1, D_MODEL), ys)


def _routing(top_idx, bm):
    n = top_idx.shape[0]
    onehot = jnp.sum((top_idx[:, :, None] == jnp.arange(N_EXPERTS, dtype=I32)).astype(I32), axis=1)
    csum = jnp.cumsum(onehot, axis=0)
    counts = csum[-1]
    rank = jnp.take_along_axis(csum - onehot, top_idx, axis=1)
    padded = (counts + bm - 1) // bm * bm
    pend = jnp.cumsum(padded)
    pstart = pend - padded
    dest = pstart[top_idx] + rank
    n_blocks = (n * TOP_K + N_EXPERTS * (bm - 1) + bm - 1) // bm
    blk_expert = jnp.minimum(
        jnp.searchsorted(pend, jnp.arange(n_blocks, dtype=pend.dtype) * bm, side='right'),
        N_EXPERTS - 1).astype(I32)
    n_valid = (pend[-1:] // bm).astype(I32)
    pad_start = (pstart + counts).astype(I32)
    return dest.reshape(n * TOP_K).astype(I32), blk_expert, n_valid, pad_start, n_blocks


def _layer(xf, b, s, p):
    qkv, rest = _in_proj(xf, p["norm_mix_g"], p["w_in"])
    slopes = jnp.exp2(-8.0 / ATTN_HEADS * jnp.arange(1, ATTN_HEADS + 1, dtype=F32))
    attn = _attention(qkv.reshape(b, s, 3 * ATTN_WIDTH), slopes)
    rw = _rwkv_mix(
        rest.reshape(b, s, SHIFT_WIDTH), p["shift_mu"], p["decay_w0"], p["w_decay_up"],
        p["iclr_a0"], p["w_iclr_up"], p["w_gate_lr_up"], p["k_k"], p["k_a"],
        p["r_k"].reshape(-1), p["gn_w"], p["gn_b"])
    flat = lambda t: t.reshape(b * s, -1)
    h, xn, idx_pad, gate_pad = _mix_out(
        flat(attn), flat(rw), xf, p["w_out"], p["norm_ffn_g"], p["w_router"], p["b_router"])
    bm = 512
    dest, blk_expert, n_valid, pad_start, n_blocks = _routing(idx_pad[:, :TOP_K], bm)
    xs = _dispatch(xn, dest, pad_start, n_valid, n_blocks, bm)
    ys = _experts(xs, blk_expert, n_valid, p["w_gate_up"], p["b_gate_up"], p["w_down"],
                  p["b_down"], n_blocks, bm)
    return h, gate_pad, ys, dest


def kernel(x, norm_mix_g, w_in, shift_mu, decay_w0, w_decay_up, iclr_a0, w_iclr_up, w_gate_lr_up,
           k_k, k_a, r_k, gn_w, gn_b, w_out, norm_ffn_g, w_router, b_router, w_gate_up,
           b_gate_up, w_down, b_down, norm_final_g):
    b, s, d = x.shape
    assert d == D_MODEL and w_in.shape[0] == 1, "single-layer block"
    names = ("norm_mix_g", "w_in", "shift_mu", "decay_w0", "w_decay_up", "iclr_a0", "w_iclr_up",
             "w_gate_lr_up", "k_k", "k_a", "r_k", "gn_w", "gn_b", "w_out", "norm_ffn_g",
             "w_router", "b_router", "w_gate_up", "b_gate_up", "w_down", "b_down")
    vals = (norm_mix_g, w_in, shift_mu, decay_w0, w_decay_up, iclr_a0, w_iclr_up, w_gate_lr_up,
            k_k, k_a, r_k, gn_w, gn_b, w_out, norm_ffn_g, w_router, b_router, w_gate_up,
            b_gate_up, w_down, b_down)
    p = {nm: v[0] for nm, v in zip(names, vals)}
    h, gate_pad, ys, dest = _layer(x.reshape(b * s, d), b, s, p)
    out = _combine(h, gate_pad, ys, dest, norm_final_g)
    return out.reshape(b, s, d)
```

```python
import functools
import math

import jax
import jax.numpy as jnp
from jax import lax
from jax.experimental import pallas as pl
from jax.experimental.pallas import tpu as pltpu

F32 = jnp.float32
BF16 = jnp.bfloat16
I32 = jnp.int32

D_MODEL = 1024
HEAD_DIM = 64
ATTN_WIDTH = 512
ATTN_HEADS = ATTN_WIDTH // HEAD_DIM
RWKV_WIDTH = 512
RWKV_HEADS = RWKV_WIDTH // HEAD_DIM
DILATED_PATTERNS = ((128, 1), (512, 4), (2048, 16))
ATTN_BLOCK = 128
DECAY_LORA = 64
ICLR_LORA = 64
GATE_LORA = 128
SHIFT_WIDTH = 3 * RWKV_WIDTH + DECAY_LORA + ICLR_LORA + GATE_LORA
N_EXPERTS = 32
TOP_K = 4
D_FF = D_MODEL
SWIGLU_LIMIT = 7.0
SWIGLU_ALPHA = 1.702
RMS_EPS = 1e-5
GN_EPS = 64e-5

LANES = 128
SUBLANES = 8
CHUNK = 64
MASK_VALUE = -1e30
LOG2_E = 1.4426950408889634
ATTN_GROUP = 4
VMEM_LIMIT = 56 * 1024 * 1024


def _params(semantics, vmem=VMEM_LIMIT):
    return pltpu.CompilerParams(dimension_semantics=semantics, vmem_limit_bytes=vmem)


def _dot(a, b):
    return jnp.dot(a.astype(BF16), b.astype(BF16), preferred_element_type=F32)


def _dot_nt(a, b):
    return lax.dot_general(a.astype(BF16), b.astype(BF16), (((1,), (1,)), ((), ())),
                           preferred_element_type=F32)


def _dot_tn(a, b):
    return lax.dot_general(a.astype(BF16), b.astype(BF16), (((0,), (0,)), ((), ())),
                           preferred_element_type=F32)


def _split3(x):
    h = x.astype(BF16)
    r1 = x - h.astype(F32)
    m = r1.astype(BF16)
    l = (r1 - m.astype(F32)).astype(BF16)
    return h, m, l


def _dot_exact_lhs(a_bf16, x):
    h, m, l = _split3(x)
    d = lambda y: jnp.dot(a_bf16, y, preferred_element_type=F32)
    return d(h) + d(m) + d(l)


def _dot_exact_rhs(x, b_bf16):
    h, m, l = _split3(x)
    d = lambda y: jnp.dot(y, b_bf16, preferred_element_type=F32)
    return d(h) + d(m) + d(l)


def _dot_hi(a, b):
    ah = a.astype(BF16)
    al = (a - ah.astype(F32)).astype(BF16)
    bh = b.astype(BF16)
    bl = (b - bh.astype(F32)).astype(BF16)
    d = lambda x, y: jnp.dot(x, y, preferred_element_type=F32)
    return d(ah, bh) + d(ah, bl) + d(al, bh)


def _sigmoid(x):
    return 1.0 / (1.0 + jnp.exp(-x))


def _head_ones(width):
    r = lax.broadcasted_iota(I32, (width, width), 0) // HEAD_DIM
    c = lax.broadcasted_iota(I32, (width, width), 1) // HEAD_DIM
    return jnp.where(r == c, 1.0, 0.0).astype(BF16)


def _inproj_kernel(x_ref, g_ref, wq_ref, wr_ref, qkv_ref, rest_ref):
    x = x_ref[...]
    xn = x * lax.rsqrt(jnp.mean(x * x, axis=-1, keepdims=True) + RMS_EPS) * g_ref[...]
    xb = xn.astype(BF16)
    qkv_ref[...] = jnp.dot(xb, wq_ref[...], preferred_element_type=F32)
    rest_ref[...] = jnp.dot(xb, wr_ref[...], preferred_element_type=F32)


def _in_proj(xf, g, w_in):
    n = xf.shape[0]
    tm = 256
    wq = w_in[:, :3 * ATTN_WIDTH].astype(BF16)
    wr = w_in[:, 3 * ATTN_WIDTH:].astype(BF16)
    return pl.pallas_call(
        _inproj_kernel,
        grid=(n // tm,),
        in_specs=[
            pl.BlockSpec((tm, D_MODEL), lambda i: (i, 0)),
            pl.BlockSpec((1, D_MODEL), lambda i: (0, 0)),
            pl.BlockSpec((D_MODEL, 3 * ATTN_WIDTH), lambda i: (0, 0)),
            pl.BlockSpec((D_MODEL, SHIFT_WIDTH), lambda i: (0, 0)),
        ],
        out_specs=[
            pl.BlockSpec((tm, 3 * ATTN_WIDTH), lambda i: (i, 0)),
            pl.BlockSpec((tm, SHIFT_WIDTH), lambda i: (i, 0)),
        ],
        out_shape=[
            jax.ShapeDtypeStruct((n, 3 * ATTN_WIDTH), F32),
            jax.ShapeDtypeStruct((n, SHIFT_WIDTH), F32),
        ],
        compiler_params=_params(("parallel",)),
        name="in_proj",
    )(xf, g.reshape(1, D_MODEL), wq, wr)


def _attn_kernel(slopes_ref, q_ref, k_ref, v_ref, o_ref, bias_ref, m_acc, d_acc, n_acc, *, seq):
    c = ATTN_BLOCK
    hp = pl.program_id(1)
    lane = lax.broadcasted_iota(I32, (c, LANES), 1)
    lo_half = lane < HEAD_DIM
    qi = lax.broadcasted_iota(I32, (c, 2 * c), 0)
    kj = lax.broadcasted_iota(I32, (c, 2 * c), 1)
    diff = qi + c - kj
    for p, (window, dil) in enumerate(DILATED_PATTERNS):
        steps = window // dil
        valid = (diff >= 0) & (diff <= steps)
        dist = (dil * diff).astype(F32)
        for hh in range(2):
            slope = slopes_ref[hp * 2 + hh] * LOG2_E
            bias = jnp.where(valid, -slope * dist, MASK_VALUE)
            bias_ref[2 * (p * 2 + hh)] = bias
            bias_ref[2 * (p * 2 + hh) + 1] = jnp.where(kj >= c, bias, MASK_VALUE)

    scale = LOG2_E / math.sqrt(HEAD_DIM)
    order = sorted(range(len(DILATED_PATTERNS)), key=lambda p: -DILATED_PATTERNS[p][1])

    def rows(start, dil):
        if dil == 1:
            return pl.ds(pl.multiple_of(start, c), c)
        return pl.ds(start, c, stride=dil)

    def group(i, carry, p, dil, nb):
        blocks = []
        for g in range(ATTN_GROUP):
            j = i * ATTN_GROUP + g
            if nb >= ATTN_GROUP:
                r, n = j // nb, j % nb
                first = (n == 0) if g == 0 else False
            else:
                r, n = j // nb, g % nb
                first = n == 0
            blocks.append((r + dil * c * n, r + dil * c * jnp.maximum(n - 1, 0), first))
        qs, ks, vs = [], [], []
        for cur, prev, first in blocks:
            qs.append(q_ref[0, rows(cur, dil), :] * scale)
            kc = k_ref[0, rows(cur, dil), :].astype(BF16)
            vc = v_ref[0, rows(cur, dil), :].astype(BF16)
            if first is True:
                ks.append(kc)
                vs.append(vc)
            else:
                ks.append(jnp.concatenate([k_ref[0, rows(prev, dil), :].astype(BF16), kc], axis=0))
                vs.append(jnp.concatenate([v_ref[0, rows(prev, dil), :].astype(BF16), vc], axis=0))
        scores = []
        for (cur, prev, first), qf, kcat in zip(blocks, qs, ks):
            for hh in range(2):
                head_mask = lo_half if hh == 0 else jnp.logical_not(lo_half)
                qh = jnp.where(head_mask, qf, 0.0).astype(BF16)
                s = lax.dot_general(qh, kcat, (((1,), (1,)), ((), ())), preferred_element_type=F32)
                slot = 2 * (p * 2 + hh)
                if first is True:
                    s = s + bias_ref[slot][:, c:]
                elif first is False:
                    s = s + bias_ref[slot]
                else:
                    s = s + bias_ref[slot + first.astype(I32)]
                scores.append(s)
        probs = []
        for s in scores:
            m = jnp.max(s, axis=-1, keepdims=True)
            e = jnp.exp2(s - m)
            probs.append((m, jnp.sum(e, axis=-1, keepdims=True), e.astype(BF16)))
        for bi, (cur, prev, first) in enumerate(blocks):
            (m0, d0, e0), (m1, d1, e1) = probs[2 * bi], probs[2 * bi + 1]
            o0 = jnp.dot(e0, vs[bi], preferred_element_type=F32)
            o1 = jnp.dot(e1, vs[bi], preferred_element_type=F32)
            m_b = jnp.where(lo_half, m0, m1)
            d_b = jnp.where(lo_half, d0, d1)
            n_b = jnp.where(lo_half, o0, o1)
            sl = rows(cur, dil)
            if p == order[0]:
                m_acc[sl, :] = m_b
                d_acc[sl, :] = d_b
                n_acc[sl, :] = n_b
            else:
                m_o = m_acc[sl, :]
                m_n = jnp.maximum(m_o, m_b)
                a_o = jnp.exp2(m_o - m_n)
                a_b = jnp.exp2(m_b - m_n)
                d_n = d_acc[sl, :] * a_o + d_b * a_b
                n_n = n_acc[sl, :] * a_o + n_b * a_b
                if p == order[-1]:
                    o_ref[0, sl, :] = n_n / d_n
                else:
                    d_acc[sl, :] = d_n
                    n_acc[sl, :] = n_n
                    m_acc[sl, :] = m_n
        return carry

    for p in order:
        dil = DILATED_PATTERNS[p][1]
        nb = seq // (c * dil)
        n_groups = (nb * dil) // ATTN_GROUP
        lax.fori_loop(0, n_groups, functools.partial(group, p=p, dil=dil, nb=nb), 0)


def _attention(qkv, slopes):
    b, s, _ = qkv.shape
    for window, dil in DILATED_PATTERNS:
        nb = s // (ATTN_BLOCK * dil)
        assert s % (ATTN_BLOCK * dil) == 0 and window // dil <= ATTN_BLOCK
        assert nb % ATTN_GROUP == 0 or (ATTN_GROUP % nb == 0 and (nb * dil) % ATTN_GROUP == 0)
    n_pairs = ATTN_WIDTH // LANES
    blk = lambda off: pl.BlockSpec((1, s, LANES), lambda bi, hp, sl: (bi, 0, off + hp))
    return pl.pallas_call(
        functools.partial(_attn_kernel, seq=s),
        grid_spec=pltpu.PrefetchScalarGridSpec(
            num_scalar_prefetch=1,
            grid=(b, n_pairs),
            in_specs=[blk(0), blk(n_pairs), blk(2 * n_pairs)],
            out_specs=pl.BlockSpec((1, s, LANES), lambda bi, hp, sl: (bi, 0, hp)),
            scratch_shapes=[
                pltpu.VMEM((4 * len(DILATED_PATTERNS), ATTN_BLOCK, 2 * ATTN_BLOCK), F32),
                pltpu.VMEM((s, LANES), F32),
                pltpu.VMEM((s, LANES), F32),
                pltpu.VMEM((s, LANES), F32),
            ],
        ),
        out_shape=jax.ShapeDtypeStruct((b, s, ATTN_WIDTH), F32),
        compiler_params=_params(("parallel", "parallel")),
        name="dilated_attention",
    )(slopes, qkv, qkv, qkv)


def _seg_sum(x, ones_ref, passes):
    hw = ones_ref.shape[0]
    terms = [x.astype(BF16)]
    if passes == 2:
        terms.append((x - terms[0].astype(F32)).astype(BF16))
    parts = []
    for j in range(x.shape[1] // hw):
        sl = slice(j * hw, (j + 1) * hw)
        parts.append(sum(jnp.dot(t[:, sl], ones_ref[...], preferred_element_type=F32) for t in terms))
    return jnp.concatenate(parts, axis=1)


def _rwkv_kernel(rest_ref, mu_ref, w0_ref, wdu_ref, a0_ref, wau_ref, wg_ref, kk_ref, ka_ref,
                 rk_ref, gnw_ref, gnb_ref, o_ref, st_ref, prev_ref, tri_ref, ones_ref, *, n_chunks):
    l = CHUNK
    pw = LANES
    n_pairs = RWKV_WIDTH // pw
    tc = n_chunks * l
    w = RWKV_WIDTH

    @pl.when(pl.program_id(1) == 0)
    def _():
        st_ref[...] = jnp.zeros_like(st_ref)
        prev_ref[...] = jnp.zeros_like(prev_ref)
        ri = lax.broadcasted_iota(I32, (tc, tc), 0)
        ci = lax.broadcasted_iota(I32, (tc, tc), 1)
        tri_ref[...] = jnp.where((ri >= ci) & (ri // l == ci // l), 1.0, 0.0).astype(BF16)
        hr = lax.broadcasted_iota(I32, ones_ref.shape, 0) // HEAD_DIM
        hc = lax.broadcasted_iota(I32, ones_ref.shape, 1) // HEAD_DIM
        ones_ref[...] = jnp.where(hr == hc, 1.0, 0.0).astype(BF16)

    z = rest_ref[0]
    zrow = lax.broadcasted_iota(I32, z.shape, 0)
    zprev = jnp.where(zrow == 0, prev_ref[...], pltpu.roll(z, 1, 0))
    prev_ref[...] = z[tc - 1:tc, :]
    xs = z + (zprev - z) * mu_ref[...]
    r = xs[:, :w]
    k_in = xs[:, w:2 * w]
    v = xs[:, 2 * w:3 * w]
    wd = xs[:, 3 * w:3 * w + DECAY_LORA]
    ad = xs[:, 3 * w + DECAY_LORA:3 * w + DECAY_LORA + ICLR_LORA]
    gd = xs[:, 3 * w + DECAY_LORA + ICLR_LORA:]
    zz = w0_ref[...] + _dot_hi(jnp.tanh(wd), wdu_ref[...])
    lw = -math.exp(-0.5) * _sigmoid(zz)
    a = _sigmoid(a0_ref[...] + _dot(ad, wau_ref[...]))
    gate = _dot(_sigmoid(gd), wg_ref[...])
    kk = k_in * kk_ref[...]
    kk = kk * lax.rsqrt(jnp.maximum(_seg_sum(kk * kk, ones_ref, 2), 1e-24))
    k = k_in * (1.0 + (a - 1.0) * ka_ref[...])
    cum = _dot_exact_lhs(tri_ref[...], lw)
    tot = jnp.concatenate(
        [jnp.broadcast_to(cum[c * l + l - 1:c * l + l, :], (l, w)) for c in range(n_chunks)], axis=0)
    p_in = jnp.exp(cum)
    p_ex = jnp.exp(cum - lw)
    p_inv = jnp.exp(-cum)
    p_end = jnp.exp(tot - cum)
    p_tot = jnp.exp(tot)
    kka = kk * a
    al = -kk * p_ex
    be = kka * p_inv
    kt = k * p_inv
    rt = r * p_in
    bh = kka * p_end
    kh = k * p_end

    row = lax.broadcasted_iota(I32, (l, pw), 0)
    col = lax.broadcasted_iota(I32, (l, pw), 1) % l
    strict = row > col
    incl = row >= col
    eye_ss = jnp.where(row == col, 1.0, 0.0).astype(F32)
    row_bd = lax.broadcasted_iota(I32, (2 * l, pw), 0)
    lane_bd = lax.broadcasted_iota(I32, (2 * l, pw), 1)
    diag_mask = (row_bd // l) == (lane_bd // l)
    eye_bd = row_bd == lane_bd

    def bd(x):
        return jnp.where(diag_mask, jnp.concatenate([x, x], axis=0), 0.0)

    items = [(c, p) for c in range(n_chunks) for p in range(n_pairs)]
    cut = lambda t, c, p: t[c * l:(c + 1) * l, p * pw:(p + 1) * pw]

    zeros_bd = jnp.zeros((2 * l, pw), F32)
    zeros_ss = jnp.zeros((l, pw), F32)
    a_ab, a_rb, a_ak, a_rk, akv = {}, {}, {}, {}, {}
    for it in items:
        lhs = jnp.concatenate([cut(al, *it), cut(rt, *it)], axis=0)
        g = _dot_nt(lhs, jnp.concatenate([bd(cut(be, *it)), bd(cut(kt, *it))], axis=0))
        a_ab[it] = jnp.where(strict, g[:l, :pw], 0.0)
        a_ak[it] = jnp.where(strict, g[:l, pw:], 0.0)
        a_rb[it] = jnp.where(incl, g[l:, :pw], 0.0)
        a_rk[it] = jnp.where(incl, g[l:, pw:], 0.0)
    for it in items:
        akv[it] = _dot(a_ak[it], bd(cut(v, *it)))
    xs = {it: _dot(a_ab[it], bd(a_ab[it])) for it in items}
    ts = {it: eye_ss + a_ab[it] for it in items}
    for _ in range(int(math.log2(l)) - 2):
        for it in items:
            both = _dot(jnp.concatenate([xs[it], ts[it]], axis=0), bd(xs[it]))
            xs[it] = both[:l]
            ts[it] = ts[it] + both[l:]
    for it in items:
        ts[it] = ts[it] + _dot(ts[it], bd(xs[it]))
    w_t, u0 = {}, {}
    for it in items:
        wu = _dot(ts[it], jnp.concatenate([bd(cut(al, *it)), bd(akv[it])], axis=1))
        w_t[it] = wu[:, :pw]
        u0[it] = wu[:, pw:]
    r_hat, y0, m_bd, c_bd = {}, {}, {}, {}
    for it in items:
        v_i = cut(v, *it)
        ry = _dot(jnp.concatenate([a_rb[it], a_rk[it]], axis=1),
                  jnp.concatenate([jnp.concatenate([bd(w_t[it]), bd(u0[it])], axis=1),
                                   jnp.concatenate([zeros_bd, bd(v_i)], axis=1)], axis=0))
        r_hat[it] = cut(rt, *it) + ry[:, :pw]
        y0[it] = ry[:, pw:]
        mc = _dot_tn(jnp.concatenate([cut(bh, *it), cut(kh, *it)], axis=0),
                     jnp.concatenate([jnp.concatenate([w_t[it], u0[it]], axis=1),
                                      jnp.concatenate([zeros_ss, v_i], axis=1)], axis=0))
        decay = jnp.broadcast_to(cut(p_tot, *it)[:1], (2 * l, pw))
        m_bd[it] = jnp.where(diag_mask, mc[:, :pw], 0.0) + jnp.where(eye_bd, decay, 0.0)
        c_bd[it] = jnp.where(diag_mask, mc[:, pw:], 0.0)
    st = [st_ref[p] for p in range(n_pairs)]
    for c in range(n_chunks):
        for p in range(n_pairs):
            it = (c, p)
            both = _dot(jnp.concatenate([r_hat[it], m_bd[it]], axis=0), st[p])
            o_ref[0, c * l:(c + 1) * l, p * pw:(p + 1) * pw] = both[:l] + y0[it]
            st[p] = both[l:] + c_bd[it]
    for p in range(n_pairs):
        st_ref[p] = st[p]

    y = o_ref[0]
    inv_n = 1.0 / HEAD_DIM
    mean = _seg_sum(y, ones_ref, 1) * inv_n
    yc = y - mean
    var = _seg_sum(yc * yc, ones_ref, 1) * inv_n
    yn = yc * lax.rsqrt(var + GN_EPS) * gnw_ref[...] + gnb_ref[...]
    bonus = _seg_sum(r * k * rk_ref[...], ones_ref, 1) * v
    o_ref[0] = (yn + bonus) * gate


def _rwkv_mix(rest, shift_mu, decay_w0, w_decay_up, iclr_a0, w_iclr_up, w_gate_lr_up, k_k, k_a,
              r_k, gn_w, gn_b):
    b, s, _ = rest.shape
    w = RWKV_WIDTH
    tc = 256
    assert s % tc == 0
    row = lambda x: x.reshape(1, -1)
    full = lambda shape: pl.BlockSpec(shape, lambda bi, i: (0, 0))
    return pl.pallas_call(
        functools.partial(_rwkv_kernel, n_chunks=tc // CHUNK),
        grid=(b, s // tc),
        in_specs=[
            pl.BlockSpec((1, tc, SHIFT_WIDTH), lambda bi, i: (bi, i, 0)),
            full((1, SHIFT_WIDTH)), full((1, w)), full((DECAY_LORA, w)), full((1, w)),
            full((ICLR_LORA, w)), full((GATE_LORA, w)), full((1, w)), full((1, w)),
            full((1, w)), full((1, w)), full((1, w)),
        ],
        out_specs=pl.BlockSpec((1, tc, w), lambda bi, i: (bi, i, 0)),
        out_shape=jax.ShapeDtypeStruct((b, s, w), F32),
        scratch_shapes=[
            pltpu.VMEM((w // LANES, LANES, LANES), F32),
            pltpu.VMEM((1, SHIFT_WIDTH), F32),
            pltpu.VMEM((tc, tc), BF16),
            pltpu.VMEM((2 * LANES, 2 * LANES), BF16),
        ],
        compiler_params=_params(("parallel", "arbitrary")),
        name="rwkv7_mix",
    )(rest, row(shift_mu), row(decay_w0), w_decay_up, row(iclr_a0), w_iclr_up, w_gate_lr_up,
      row(k_k), row(k_a), row(r_k), row(gn_w), row(gn_b))


def _mix_kernel(attn_ref, rw_ref, x_ref, wo_ref, nf_ref, wr_ref, br_ref,
                h_out, xn_out, idx_out, gate_out, rank_out, cnt_out, cnt_ref):
    @pl.when(pl.program_id(0) == 0)
    def _():
        cnt_ref[...] = jnp.zeros_like(cnt_ref)

    mixed = (jnp.dot(attn_ref[...].astype(BF16), wo_ref[:ATTN_WIDTH, :], preferred_element_type=F32)
             + jnp.dot(rw_ref[...].astype(BF16), wo_ref[ATTN_WIDTH:, :], preferred_element_type=F32))
    h = x_ref[...] + mixed
    h_out[...] = h
    xn = h * lax.rsqrt(jnp.mean(h * h, axis=-1, keepdims=True) + RMS_EPS) * nf_ref[...]
    xn_out[...] = xn
    logits = _dot_hi(xn, wr_ref[...]) + br_ref[...]
    eidx = lax.broadcasted_iota(I32, logits.shape, 1).astype(F32)
    lane = lax.broadcasted_iota(I32, (logits.shape[0], LANES), 1)
    idx_pad = jnp.zeros((logits.shape[0], LANES), F32)
    val_pad = jnp.full((logits.shape[0], LANES), MASK_VALUE, F32)
    cur = logits
    sels = []
    for kth in range(TOP_K):
        m = jnp.max(cur, axis=-1, keepdims=True)
        sel = jnp.min(jnp.where(cur == m, eidx, float(N_EXPERTS)), axis=-1, keepdims=True)
        sels.append(sel)
        idx_pad = jnp.where(lane == kth, sel, idx_pad)
        val_pad = jnp.where(lane == kth, m, val_pad)
        cur = jnp.where(eidx == sel, -jnp.inf, cur)
    top = jnp.max(val_pad, axis=-1, keepdims=True)
    e = jnp.exp(val_pad - top)
    gate_out[...] = e / jnp.sum(e, axis=-1, keepdims=True)
    idx_out[...] = idx_pad.astype(I32)

    tm = logits.shape[0]
    lane_f = lane.astype(F32)
    picked = [lane_f == sel for sel in sels]
    onehot = sum(jnp.where(pk, 1.0, 0.0) for pk in picked)
    tr = lax.broadcasted_iota(I32, (tm, tm), 0)
    tc = lax.broadcasted_iota(I32, (tm, tm), 1)
    earlier = jnp.where(tr > tc, 1.0, 0.0).astype(BF16)
    before = jnp.dot(earlier, onehot.astype(BF16), preferred_element_type=F32) + cnt_ref[...]
    rank_pad = jnp.zeros((tm, LANES), F32)
    for kth, pk in enumerate(picked):
        rank_k = jnp.sum(jnp.where(pk, before, 0.0), axis=-1, keepdims=True)
        rank_pad = jnp.where(lane == kth, rank_k, rank_pad)
    rank_out[...] = rank_pad.astype(I32)
    counts = cnt_ref[...] + jnp.sum(onehot, axis=0, keepdims=True)
    cnt_ref[...] = counts
    cnt_out[...] = counts


def _mix_out(attn, rw, xf, w_out, norm_ffn_g, w_router, b_router):
    n = xf.shape[0]
    tm = 256
    row = lambda x: x.reshape(1, -1)
    half = pl.BlockSpec((tm, RWKV_WIDTH), lambda i: (i, 0))
    wide = pl.BlockSpec((tm, D_MODEL), lambda i: (i, 0))
    pad = pl.BlockSpec((tm, LANES), lambda i: (i, 0))
    full = lambda shape: pl.BlockSpec(shape, lambda i: (0, 0))
    return pl.pallas_call(
        _mix_kernel,
        grid=(n // tm,),
        in_specs=[half, half, wide, full((D_MODEL, D_MODEL)), full((1, D_MODEL)),
                  full((D_MODEL, N_EXPERTS)), full((1, N_EXPERTS))],
        out_specs=[wide, wide, pad, pad, pad, full((1, LANES))],
        out_shape=[jax.ShapeDtypeStruct((n, D_MODEL), F32), jax.ShapeDtypeStruct((n, D_MODEL), F32),
                   jax.ShapeDtypeStruct((n, LANES), I32), jax.ShapeDtypeStruct((n, LANES), F32),
                   jax.ShapeDtypeStruct((n, LANES), I32), jax.ShapeDtypeStruct((1, LANES), F32)],
        scratch_shapes=[pltpu.VMEM((1, LANES), F32)],
        compiler_params=_params(("arbitrary",)),
        name="mix_out_router",
    )(attn, rw, xf, w_out.astype(BF16), row(norm_ffn_g), w_router, row(b_router))


def _dispatch_kernel(pad_ref, nvalid_ref, dest_ref, x_ref, out_ref, zbuf, sem, zsem,
                     *, tm, bm, n_blocks):
    @pl.when(pl.program_id(0) == 0)
    def _():
        zbuf[...] = jnp.zeros_like(zbuf)

        def fill(start):
            cp = pltpu.make_async_copy(zbuf, out_ref.at[pl.ds(start, bm), :], zsem)
            cp.start()
            cp.wait()

        def pad_fill(e, carry):
            start = pad_ref[e]
            for q in range(SUBLANES - 1):
                cp = pltpu.make_async_copy(zbuf.at[pl.ds(0, 1), :],
                                           out_ref.at[pl.ds(start + q, 1), :], zsem)
                cp.start()
                cp.wait()
            aligned = (start + (SUBLANES - 1)) // SUBLANES * SUBLANES
            fill(pl.multiple_of(aligned, SUBLANES))
            return carry

        def tail_fill(blk, carry):
            fill(pl.multiple_of(blk * bm, bm))
            return carry

        lax.fori_loop(0, N_EXPERTS, pad_fill, 0)
        lax.fori_loop(nvalid_ref[0], n_blocks + 1, tail_fill, 0)

    def issue(jj, carry):
        base = pl.multiple_of(jj * SUBLANES, SUBLANES)
        for q in range(SUBLANES):
            for kth in range(TOP_K):
                pltpu.make_async_copy(
                    x_ref.at[pl.ds(base + q, 1), :],
                    out_ref.at[pl.ds(dest_ref[(base + q) * TOP_K + kth], 1), :],
                    sem).start(priority=kth % 2)
        return carry

    lax.fori_loop(0, tm // SUBLANES, issue, 0)
    for kth in range(TOP_K):
        pltpu.make_async_copy(x_ref, out_ref.at[pl.ds(0, tm), :], sem).wait()


def _dispatch(xn, dest_flat, pad_start, n_valid, n_blocks, bm):
    n = xn.shape[0]
    tm = 256
    return pl.pallas_call(
        functools.partial(_dispatch_kernel, tm=tm, bm=bm, n_blocks=n_blocks),
        grid_spec=pltpu.PrefetchScalarGridSpec(
            num_scalar_prefetch=2,
            grid=(n // tm,),
            in_specs=[
                pl.BlockSpec((tm * TOP_K,), lambda i, ps, nv: (i,), memory_space=pltpu.SMEM),
                pl.BlockSpec((tm, D_MODEL), lambda i, ps, nv: (i, 0)),
            ],
            out_specs=pl.BlockSpec(memory_space=pl.ANY),
            scratch_shapes=[pltpu.VMEM((bm, D_MODEL), F32), pltpu.SemaphoreType.DMA,
                            pltpu.SemaphoreType.DMA],
        ),
        out_shape=jax.ShapeDtypeStruct(((n_blocks + 1) * bm, D_MODEL), F32),
        compiler_params=_params(("arbitrary",)),
        name="moe_dispatch",
    )(pad_start, n_valid, dest_flat, xn)


def _expert_kernel(be_ref, nvalid_ref, x_ref, wgu_ref, bgu_ref, wd_ref, bd_ref, o_ref):
    i = pl.program_id(0)

    @pl.when(i < nvalid_ref[0])
    def _():
        x = x_ref[...].astype(BF16)
        acc = jnp.zeros(o_ref.shape, F32)
        cw = 256
        for j in range(D_FF // cw):
            g = jnp.dot(x, wgu_ref[0, :, j * cw:(j + 1) * cw], preferred_element_type=F32)
            g = g + bgu_ref[0, :, j * cw:(j + 1) * cw]
            u = jnp.dot(x, wgu_ref[0, :, D_FF + j * cw:D_FF + (j + 1) * cw],
                        preferred_element_type=F32)
            u = u + bgu_ref[0, :, D_FF + j * cw:D_FF + (j + 1) * cw]
            g = jnp.minimum(g, SWIGLU_LIMIT)
            u = jnp.clip(u, -SWIGLU_LIMIT, SWIGLU_LIMIT)
            act = (u + 1.0) * (g * _sigmoid(SWIGLU_ALPHA * g))
            acc = acc + jnp.dot(act.astype(BF16), wd_ref[0, j * cw:(j + 1) * cw, :],
                                preferred_element_type=F32)
        o_ref[...] = acc + bd_ref[0]

    @pl.when(i >= nvalid_ref[0])
    def _():
        o_ref[...] = jnp.zeros_like(o_ref)


def _experts(xs, blk_expert, n_valid, w_gate_up, b_gate_up, w_down, b_down, n_blocks, bm):
    n_rows = n_blocks * bm
    return pl.pallas_call(
        _expert_kernel,
        grid_spec=pltpu.PrefetchScalarGridSpec(
            num_scalar_prefetch=2,
            grid=(n_blocks,),
            in_specs=[
                pl.BlockSpec((bm, D_MODEL), lambda i, be, nv: (i, 0)),
                pl.BlockSpec((1, D_MODEL, 2 * D_FF), lambda i, be, nv: (be[i], 0, 0)),
                pl.BlockSpec((1, 1, 2 * D_FF), lambda i, be, nv: (be[i], 0, 0)),
                pl.BlockSpec((1, D_FF, D_MODEL), lambda i, be, nv: (be[i], 0, 0)),
                pl.BlockSpec((1, 1, D_MODEL), lambda i, be, nv: (be[i], 0, 0)),
            ],
            out_specs=pl.BlockSpec((bm, D_MODEL), lambda i, be, nv: (i, 0)),
        ),
        out_shape=jax.ShapeDtypeStruct((n_rows, D_MODEL), F32),
        compiler_params=_params(("arbitrary",)),
        name="moe_experts",
    )(blk_expert, n_valid, xs, w_gate_up.astype(BF16), b_gate_up.reshape(N_EXPERTS, 1, 2 * D_FF),
      w_down.astype(BF16), b_down.reshape(N_EXPERTS, 1, D_MODEL))


def _combine_kernel(dest_ref, dest_next_ref, h_ref, gate_ref, g_ref, ys_ref, o_ref, buf, sem,
                    *, tm, n_tiles):
    i = pl.program_id(0)
    slot = i % 2

    def issue(dref, s):
        def body(jj, carry):
            base = pl.multiple_of(jj * SUBLANES, SUBLANES)
            for q in range(SUBLANES):
                for kth in range(TOP_K):
                    pltpu.make_async_copy(
                        ys_ref.at[pl.ds(dref[(base + q) * TOP_K + kth], 1), :],
                        buf.at[s, kth, pl.ds(base + q, 1), :],
                        sem.at[s]).start(priority=kth % 2)
            return carry

        lax.fori_loop(0, tm // SUBLANES, body, 0)

    @pl.when(i == 0)
    def _():
        issue(dest_ref, slot)

    @pl.when(i + 1 < n_tiles)
    def _():
        issue(dest_next_ref, 1 - slot)

    for kth in range(TOP_K):
        pltpu.make_async_copy(ys_ref.at[pl.ds(0, tm), :], buf.at[slot, kth], sem.at[slot]).wait()
    gates = gate_ref[...]
    hf = h_ref[...]
    for kth in range(TOP_K):
        hf = hf + gates[:, kth:kth + 1] * buf[slot, kth]
    o_ref[...] = hf * lax.rsqrt(jnp.mean(hf * hf, axis=-1, keepdims=True) + RMS_EPS) * g_ref[...]


def _combine(h, gate_pad, ys, dest_flat, norm_final_g):
    n = h.shape[0]
    tm = 256
    n_tiles = n // tm
    return pl.pallas_call(
        functools.partial(_combine_kernel, tm=tm, n_tiles=n_tiles),
        grid=(n_tiles,),
        in_specs=[
            pl.BlockSpec((tm * TOP_K,), lambda i: (i,), memory_space=pltpu.SMEM),
            pl.BlockSpec((tm * TOP_K,), lambda i: (jnp.minimum(i + 1, n_tiles - 1),),
                         memory_space=pltpu.SMEM),
            pl.BlockSpec((tm, D_MODEL), lambda i: (i, 0)),
            pl.BlockSpec((tm, LANES), lambda i: (i, 0)),
            pl.BlockSpec((1, D_MODEL), lambda i: (0, 0)),
            pl.BlockSpec(memory_space=pl.ANY),
        ],
        out_specs=pl.BlockSpec((tm, D_MODEL), lambda i: (i, 0)),
        out_shape=jax.ShapeDtypeStruct((n, D_MODEL), F32),
        scratch_shapes=[pltpu.VMEM((2, TOP_K, tm, D_MODEL), F32), pltpu.SemaphoreType.DMA((2,))],
        compiler_params=_params(("arbitrary",)),
        name="moe_combine_norm",
    )(dest_flat, dest_flat, h, gate_pad, norm_final_g.reshape(1, D_MODEL), ys)


def _routing(top_idx, rank, counts, bm):
    n = top_idx.shape[0]
    padded = (counts + bm - 1) // bm * bm
    pend = jnp.cumsum(padded)
    pstart = pend - padded
    dest = pstart[top_idx] + rank
    n_blocks = (n * TOP_K + N_EXPERTS * (bm - 1) + bm - 1) // bm
    blk_expert = jnp.minimum(
        jnp.searchsorted(pend, jnp.arange(n_blocks, dtype=pend.dtype) * bm, side='right'),
        N_EXPERTS - 1).astype(I32)
    n_valid = (pend[-1:] // bm).astype(I32)
    pad_start = (pstart + counts).astype(I32)
    return dest.reshape(n * TOP_K).astype(I32), blk_expert, n_valid, pad_start, n_blocks


def _layer(xf, b, s, p):
    qkv, rest = _in_proj(xf, p["norm_mix_g"], p["w_in"])
    slopes = jnp.exp2(-8.0 / ATTN_HEADS * jnp.arange(1, ATTN_HEADS + 1, dtype=F32))
    attn = _attention(qkv.reshape(b, s, 3 * ATTN_WIDTH), slopes)
    rw = _rwkv_mix(
        rest.reshape(b, s, SHIFT_WIDTH), p["shift_mu"], p["decay_w0"], p["w_decay_up"],
        p["iclr_a0"], p["w_iclr_up"], p["w_gate_lr_up"], p["k_k"], p["k_a"],
        p["r_k"].reshape(-1), p["gn_w"], p["gn_b"])
    flat = lambda t: t.reshape(b * s, -1)
    h, xn, idx_pad, gate_pad, rank_pad, counts = _mix_out(
        flat(attn), flat(rw), xf, p["w_out"], p["norm_ffn_g"], p["w_router"], p["b_router"])
    bm = 512
    dest, blk_expert, n_valid, pad_start, n_blocks = _routing(
        idx_pad[:, :TOP_K], rank_pad[:, :TOP_K], counts[0, :N_EXPERTS].astype(I32), bm)
    xs = _dispatch(xn, dest, pad_start, n_valid, n_blocks, bm)
    ys = _experts(xs, blk_expert, n_valid, p["w_gate_up"], p["b_gate_up"], p["w_down"],
                  p["b_down"], n_blocks, bm)
    return h, gate_pad, ys, dest


def kernel(x, norm_mix_g, w_in, shift_mu, decay_w0, w_decay_up, iclr_a0, w_iclr_up, w_gate_lr_up,
           k_k, k_a, r_k, gn_w, gn_b, w_out, norm_ffn_g, w_router, b_router, w_gate_up,
           b_gate_up, w_down, b_down, norm_final_g):
    b, s, d = x.shape
    assert d == D_MODEL and w_in.shape[0] == 1, "single-layer block"
    names = ("norm_mix_g", "w_in", "shift_mu", "decay_w0", "w_decay_up", "iclr_a0", "w_iclr_up",
             "w_gate_lr_up", "k_k", "k_a", "r_k", "gn_w", "gn_b", "w_out", "norm_ffn_g",
             "w_router", "b_router", "w_gate_up", "b_gate_up", "w_down", "b_down")
    vals = (norm_mix_g, w_in, shift_mu, decay_w0, w_decay_up, iclr_a0, w_iclr_up, w_gate_lr_up,
            k_k, k_a, r_k, gn_w, gn_b, w_out, norm_ffn_g, w_router, b_router, w_gate_up,
            b_gate_up, w_down, b_down)
    p = {nm: v[0] for nm, v in zip(names, vals)}
    h, gate_pad, ys, dest = _layer(x.reshape(b * s, d), b, s, p)
    out = _combine(h, gate_pad, ys, dest, norm_final_g)
    return out.reshape(b, s, d)
```

```python
import functools
import math

import jax
import jax.numpy as jnp
from jax import lax
from jax.experimental import pallas as pl
from jax.experimental.pallas import tpu as pltpu

F32 = jnp.float32
BF16 = jnp.bfloat16
I32 = jnp.int32

D_MODEL = 1024
HEAD_DIM = 64
ATTN_WIDTH = 512
ATTN_HEADS = ATTN_WIDTH // HEAD_DIM
RWKV_WIDTH = 512
RWKV_HEADS = RWKV_WIDTH // HEAD_DIM
DILATED_PATTERNS = ((128, 1), (512, 4), (2048, 16))
ATTN_BLOCK = 128
DECAY_LORA = 64
ICLR_LORA = 64
GATE_LORA = 128
SHIFT_WIDTH = 3 * RWKV_WIDTH + DECAY_LORA + ICLR_LORA + GATE_LORA
N_EXPERTS = 32
TOP_K = 4
D_FF = D_MODEL
SWIGLU_LIMIT = 7.0
SWIGLU_ALPHA = 1.702
RMS_EPS = 1e-5
GN_EPS = 64e-5

LANES = 128
SUBLANES = 8
ROW_TILE = D_MODEL // LANES
assert ROW_TILE == SUBLANES, "a D_MODEL-wide f32 row must fill exactly one (8,128) tile"
CHUNK = 64
MASK_VALUE = -1e30
LOG2_E = 1.4426950408889634
ATTN_GROUP = 4
VMEM_LIMIT = 56 * 1024 * 1024


def _params(semantics, vmem=VMEM_LIMIT):
    return pltpu.CompilerParams(dimension_semantics=semantics, vmem_limit_bytes=vmem)


def _dot(a, b):
    return jnp.dot(a.astype(BF16), b.astype(BF16), preferred_element_type=F32)


def _dot_nt(a, b):
    return lax.dot_general(a.astype(BF16), b.astype(BF16), (((1,), (1,)), ((), ())),
                           preferred_element_type=F32)


def _dot_tn(a, b):
    return lax.dot_general(a.astype(BF16), b.astype(BF16), (((0,), (0,)), ((), ())),
                           preferred_element_type=F32)


def _split3(x):
    h = x.astype(BF16)
    r1 = x - h.astype(F32)
    m = r1.astype(BF16)
    l = (r1 - m.astype(F32)).astype(BF16)
    return h, m, l


def _dot_exact_lhs(a_bf16, x):
    h, m, l = _split3(x)
    d = lambda y: jnp.dot(a_bf16, y, preferred_element_type=F32)
    return d(h) + d(m) + d(l)


def _dot_exact_rhs(x, b_bf16):
    h, m, l = _split3(x)
    d = lambda y: jnp.dot(y, b_bf16, preferred_element_type=F32)
    return d(h) + d(m) + d(l)


def _dot_hi(a, b):
    ah = a.astype(BF16)
    al = (a - ah.astype(F32)).astype(BF16)
    bh = b.astype(BF16)
    bl = (b - bh.astype(F32)).astype(BF16)
    d = lambda x, y: jnp.dot(x, y, preferred_element_type=F32)
    return d(ah, bh) + d(ah, bl) + d(al, bh)


def _store_row_tiles(ref, x):
    rows = x.shape[0]
    for s in range(ROW_TILE):
        ref[pl.ds(s, rows, stride=ROW_TILE), :] = x[:, s * LANES:(s + 1) * LANES]


def _load_row_tiles(ref, rows):
    return jnp.concatenate(
        [ref[pl.ds(s, rows, stride=ROW_TILE), :] for s in range(ROW_TILE)], axis=1)


def _sigmoid(x):
    return 1.0 / (1.0 + jnp.exp(-x))


def _head_ones(width):
    r = lax.broadcasted_iota(I32, (width, width), 0) // HEAD_DIM
    c = lax.broadcasted_iota(I32, (width, width), 1) // HEAD_DIM
    return jnp.where(r == c, 1.0, 0.0).astype(BF16)


def _inproj_kernel(x_ref, g_ref, wq_ref, wr_ref, qkv_ref, rest_ref):
    x = x_ref[...]
    xn = x * lax.rsqrt(jnp.mean(x * x, axis=-1, keepdims=True) + RMS_EPS) * g_ref[...]
    xb = xn.astype(BF16)
    qkv_ref[...] = jnp.dot(xb, wq_ref[...], preferred_element_type=F32)
    rest_ref[...] = jnp.dot(xb, wr_ref[...], preferred_element_type=F32)


def _in_proj(xf, g, w_in):
    n = xf.shape[0]
    tm = 256
    wq = w_in[:, :3 * ATTN_WIDTH].astype(BF16)
    wr = w_in[:, 3 * ATTN_WIDTH:].astype(BF16)
    return pl.pallas_call(
        _inproj_kernel,
        grid=(n // tm,),
        in_specs=[
            pl.BlockSpec((tm, D_MODEL), lambda i: (i, 0)),
            pl.BlockSpec((1, D_MODEL), lambda i: (0, 0)),
            pl.BlockSpec((D_MODEL, 3 * ATTN_WIDTH), lambda i: (0, 0)),
            pl.BlockSpec((D_MODEL, SHIFT_WIDTH), lambda i: (0, 0)),
        ],
        out_specs=[
            pl.BlockSpec((tm, 3 * ATTN_WIDTH), lambda i: (i, 0)),
            pl.BlockSpec((tm, SHIFT_WIDTH), lambda i: (i, 0)),
        ],
        out_shape=[
            jax.ShapeDtypeStruct((n, 3 * ATTN_WIDTH), F32),
            jax.ShapeDtypeStruct((n, SHIFT_WIDTH), F32),
        ],
        compiler_params=_params(("parallel",)),
        name="in_proj",
    )(xf, g.reshape(1, D_MODEL), wq, wr)


def _attn_kernel(slopes_ref, q_ref, k_ref, v_ref, o_ref, bias_ref, m_acc, d_acc, n_acc, *, seq):
    c = ATTN_BLOCK
    hp = pl.program_id(1)
    lane = lax.broadcasted_iota(I32, (c, LANES), 1)
    lo_half = lane < HEAD_DIM
    qi = lax.broadcasted_iota(I32, (c, 2 * c), 0)
    kj = lax.broadcasted_iota(I32, (c, 2 * c), 1)
    diff = qi + c - kj
    for p, (window, dil) in enumerate(DILATED_PATTERNS):
        steps = window // dil
        valid = (diff >= 0) & (diff <= steps)
        dist = (dil * diff).astype(F32)
        for hh in range(2):
            slope = slopes_ref[hp * 2 + hh] * LOG2_E
            bias = jnp.where(valid, -slope * dist, MASK_VALUE)
            bias_ref[2 * (p * 2 + hh)] = bias
            bias_ref[2 * (p * 2 + hh) + 1] = jnp.where(kj >= c, bias, MASK_VALUE)

    scale = LOG2_E / math.sqrt(HEAD_DIM)
    order = sorted(range(len(DILATED_PATTERNS)), key=lambda p: -DILATED_PATTERNS[p][1])

    def rows(start, dil):
        if dil == 1:
            return pl.ds(pl.multiple_of(start, c), c)
        return pl.ds(start, c, stride=dil)

    def group(i, carry, p, dil, nb):
        blocks = []
        for g in range(ATTN_GROUP):
            j = i * ATTN_GROUP + g
            if nb >= ATTN_GROUP:
                r, n = j // nb, j % nb
                first = (n == 0) if g == 0 else False
            else:
                r, n = j // nb, g % nb
                first = n == 0
            blocks.append((r + dil * c * n, r + dil * c * jnp.maximum(n - 1, 0), first))
        qs, ks, vs = [], [], []
        for cur, prev, first in blocks:
            qs.append(q_ref[0, rows(cur, dil), :] * scale)
            kc = k_ref[0, rows(cur, dil), :].astype(BF16)
            vc = v_ref[0, rows(cur, dil), :].astype(BF16)
            if first is True:
                ks.append(kc)
                vs.append(vc)
            else:
                ks.append(jnp.concatenate([k_ref[0, rows(prev, dil), :].astype(BF16), kc], axis=0))
                vs.append(jnp.concatenate([v_ref[0, rows(prev, dil), :].astype(BF16), vc], axis=0))
        scores = []
        for (cur, prev, first), qf, kcat in zip(blocks, qs, ks):
            for hh in range(2):
                head_mask = lo_half if hh == 0 else jnp.logical_not(lo_half)
                qh = jnp.where(head_mask, qf, 0.0).astype(BF16)
                s = lax.dot_general(qh, kcat, (((1,), (1,)), ((), ())), preferred_element_type=F32)
                slot = 2 * (p * 2 + hh)
                if first is True:
                    s = s + bias_ref[slot][:, c:]
                elif first is False:
                    s = s + bias_ref[slot]
                else:
                    s = s + bias_ref[slot + first.astype(I32)]
                scores.append(s)
        probs = []
        for s in scores:
            m = jnp.max(s, axis=-1, keepdims=True)
            e = jnp.exp2(s - m)
            probs.append((m, jnp.sum(e, axis=-1, keepdims=True), e.astype(BF16)))
        for bi, (cur, prev, first) in enumerate(blocks):
            (m0, d0, e0), (m1, d1, e1) = probs[2 * bi], probs[2 * bi + 1]
            o0 = jnp.dot(e0, vs[bi], preferred_element_type=F32)
            o1 = jnp.dot(e1, vs[bi], preferred_element_type=F32)
            m_b = jnp.where(lo_half, m0, m1)
            d_b = jnp.where(lo_half, d0, d1)
            n_b = jnp.where(lo_half, o0, o1)
            sl = rows(cur, dil)
            if p == order[0]:
                m_acc[sl, :] = m_b
                d_acc[sl, :] = d_b
                n_acc[sl, :] = n_b
            else:
                m_o = m_acc[sl, :]
                m_n = jnp.maximum(m_o, m_b)
                a_o = jnp.exp2(m_o - m_n)
                a_b = jnp.exp2(m_b - m_n)
                d_n = d_acc[sl, :] * a_o + d_b * a_b
                n_n = n_acc[sl, :] * a_o + n_b * a_b
                if p == order[-1]:
                    o_ref[0, sl, :] = n_n / d_n
                else:
                    d_acc[sl, :] = d_n
                    n_acc[sl, :] = n_n
                    m_acc[sl, :] = m_n
        return carry

    for p in order:
        dil = DILATED_PATTERNS[p][1]
        nb = seq // (c * dil)
        n_groups = (nb * dil) // ATTN_GROUP
        lax.fori_loop(0, n_groups, functools.partial(group, p=p, dil=dil, nb=nb), 0)


def _attention(qkv, slopes):
    b, s, _ = qkv.shape
    for window, dil in DILATED_PATTERNS:
        nb = s // (ATTN_BLOCK * dil)
        assert s % (ATTN_BLOCK * dil) == 0 and window // dil <= ATTN_BLOCK
        assert nb % ATTN_GROUP == 0 or (ATTN_GROUP % nb == 0 and (nb * dil) % ATTN_GROUP == 0)
    n_pairs = ATTN_WIDTH // LANES
    blk = lambda off: pl.BlockSpec((1, s, LANES), lambda bi, hp, sl: (bi, 0, off + hp))
    return pl.pallas_call(
        functools.partial(_attn_kernel, seq=s),
        grid_spec=pltpu.PrefetchScalarGridSpec(
            num_scalar_prefetch=1,
            grid=(b, n_pairs),
            in_specs=[blk(0), blk(n_pairs), blk(2 * n_pairs)],
            out_specs=pl.BlockSpec((1, s, LANES), lambda bi, hp, sl: (bi, 0, hp)),
            scratch_shapes=[
                pltpu.VMEM((4 * len(DILATED_PATTERNS), ATTN_BLOCK, 2 * ATTN_BLOCK), F32),
                pltpu.VMEM((s, LANES), F32),
                pltpu.VMEM((s, LANES), F32),
                pltpu.VMEM((s, LANES), F32),
            ],
        ),
        out_shape=jax.ShapeDtypeStruct((b, s, ATTN_WIDTH), F32),
        compiler_params=_params(("parallel", "parallel")),
        name="dilated_attention",
    )(slopes, qkv, qkv, qkv)


def _seg_sum(x, ones_ref, passes):
    hw = ones_ref.shape[0]
    terms = [x.astype(BF16)]
    if passes == 2:
        terms.append((x - terms[0].astype(F32)).astype(BF16))
    parts = []
    for j in range(x.shape[1] // hw):
        sl = slice(j * hw, (j + 1) * hw)
        parts.append(sum(jnp.dot(t[:, sl], ones_ref[...], preferred_element_type=F32) for t in terms))
    return jnp.concatenate(parts, axis=1)


def _rwkv_kernel(rest_ref, mu_ref, w0_ref, wdu_ref, a0_ref, wau_ref, wg_ref, kk_ref, ka_ref,
                 rk_ref, gnw_ref, gnb_ref, o_ref, st_ref, prev_ref, tri_ref, ones_ref, *, n_chunks):
    l = CHUNK
    pw = LANES
    n_pairs = RWKV_WIDTH // pw
    tc = n_chunks * l
    w = RWKV_WIDTH

    @pl.when(pl.program_id(1) == 0)
    def _():
        st_ref[...] = jnp.zeros_like(st_ref)
        prev_ref[...] = jnp.zeros_like(prev_ref)
        ri = lax.broadcasted_iota(I32, (tc, tc), 0)
        ci = lax.broadcasted_iota(I32, (tc, tc), 1)
        tri_ref[...] = jnp.where((ri >= ci) & (ri // l == ci // l), 1.0, 0.0).astype(BF16)
        hr = lax.broadcasted_iota(I32, ones_ref.shape, 0) // HEAD_DIM
        hc = lax.broadcasted_iota(I32, ones_ref.shape, 1) // HEAD_DIM
        ones_ref[...] = jnp.where(hr == hc, 1.0, 0.0).astype(BF16)

    z = rest_ref[0]
    zrow = lax.broadcasted_iota(I32, z.shape, 0)
    zprev = jnp.where(zrow == 0, prev_ref[...], pltpu.roll(z, 1, 0))
    prev_ref[...] = z[tc - 1:tc, :]
    xs = z + (zprev - z) * mu_ref[...]
    r = xs[:, :w]
    k_in = xs[:, w:2 * w]
    v = xs[:, 2 * w:3 * w]
    wd = xs[:, 3 * w:3 * w + DECAY_LORA]
    ad = xs[:, 3 * w + DECAY_LORA:3 * w + DECAY_LORA + ICLR_LORA]
    gd = xs[:, 3 * w + DECAY_LORA + ICLR_LORA:]
    zz = w0_ref[...] + _dot_hi(jnp.tanh(wd), wdu_ref[...])
    lw = -math.exp(-0.5) * _sigmoid(zz)
    a = _sigmoid(a0_ref[...] + _dot(ad, wau_ref[...]))
    gate = _dot(_sigmoid(gd), wg_ref[...])
    kk = k_in * kk_ref[...]
    kk = kk * lax.rsqrt(jnp.maximum(_seg_sum(kk * kk, ones_ref, 2), 1e-24))
    k = k_in * (1.0 + (a - 1.0) * ka_ref[...])
    cum = _dot_exact_lhs(tri_ref[...], lw)
    tot = jnp.concatenate(
        [jnp.broadcast_to(cum[c * l + l - 1:c * l + l, :], (l, w)) for c in range(n_chunks)], axis=0)
    p_in = jnp.exp(cum)
    p_ex = jnp.exp(cum - lw)
    p_inv = jnp.exp(-cum)
    p_end = jnp.exp(tot - cum)
    p_tot = jnp.exp(tot)
    kka = kk * a
    al = -kk * p_ex
    be = kka * p_inv
    kt = k * p_inv
    rt = r * p_in
    bh = kka * p_end
    kh = k * p_end

    row = lax.broadcasted_iota(I32, (l, pw), 0)
    col = lax.broadcasted_iota(I32, (l, pw), 1) % l
    strict = row > col
    incl = row >= col
    eye_ss = jnp.where(row == col, 1.0, 0.0).astype(F32)
    row_bd = lax.broadcasted_iota(I32, (2 * l, pw), 0)
    lane_bd = lax.broadcasted_iota(I32, (2 * l, pw), 1)
    diag_mask = (row_bd // l) == (lane_bd // l)
    eye_bd = row_bd == lane_bd

    def bd(x):
        return jnp.where(diag_mask, jnp.concatenate([x, x], axis=0), 0.0)

    items = [(c, p) for c in range(n_chunks) for p in range(n_pairs)]
    cut = lambda t, c, p: t[c * l:(c + 1) * l, p * pw:(p + 1) * pw]

    zeros_bd = jnp.zeros((2 * l, pw), F32)
    zeros_ss = jnp.zeros((l, pw), F32)
    a_ab, a_rb, a_ak, a_rk, akv = {}, {}, {}, {}, {}
    for it in items:
        lhs = jnp.concatenate([cut(al, *it), cut(rt, *it)], axis=0)
        g = _dot_nt(lhs, jnp.concatenate([bd(cut(be, *it)), bd(cut(kt, *it))], axis=0))
        a_ab[it] = jnp.where(strict, g[:l, :pw], 0.0)
        a_ak[it] = jnp.where(strict, g[:l, pw:], 0.0)
        a_rb[it] = jnp.where(incl, g[l:, :pw], 0.0)
        a_rk[it] = jnp.where(incl, g[l:, pw:], 0.0)
    for it in items:
        akv[it] = _dot(a_ak[it], bd(cut(v, *it)))
    xs = {it: _dot(a_ab[it], bd(a_ab[it])) for it in items}
    ts = {it: eye_ss + a_ab[it] for it in items}
    for _ in range(int(math.log2(l)) - 2):
        for it in items:
            both = _dot(jnp.concatenate([xs[it], ts[it]], axis=0), bd(xs[it]))
            xs[it] = both[:l]
            ts[it] = ts[it] + both[l:]
    for it in items:
        ts[it] = ts[it] + _dot(ts[it], bd(xs[it]))
    w_t, u0 = {}, {}
    for it in items:
        wu = _dot(ts[it], jnp.concatenate([bd(cut(al, *it)), bd(akv[it])], axis=1))
        w_t[it] = wu[:, :pw]
        u0[it] = wu[:, pw:]
    r_hat, y0, m_bd, c_bd = {}, {}, {}, {}
    for it in items:
        v_i = cut(v, *it)
        ry = _dot(jnp.concatenate([a_rb[it], a_rk[it]], axis=1),
                  jnp.concatenate([jnp.concatenate([bd(w_t[it]), bd(u0[it])], axis=1),
                                   jnp.concatenate([zeros_bd, bd(v_i)], axis=1)], axis=0))
        r_hat[it] = cut(rt, *it) + ry[:, :pw]
        y0[it] = ry[:, pw:]
        mc = _dot_tn(jnp.concatenate([cut(bh, *it), cut(kh, *it)], axis=0),
                     jnp.concatenate([jnp.concatenate([w_t[it], u0[it]], axis=1),
                                      jnp.concatenate([zeros_ss, v_i], axis=1)], axis=0))
        decay = jnp.broadcast_to(cut(p_tot, *it)[:1], (2 * l, pw))
        m_bd[it] = jnp.where(diag_mask, mc[:, :pw], 0.0) + jnp.where(eye_bd, decay, 0.0)
        c_bd[it] = jnp.where(diag_mask, mc[:, pw:], 0.0)
    st = [st_ref[p] for p in range(n_pairs)]
    for c in range(n_chunks):
        for p in range(n_pairs):
            it = (c, p)
            both = _dot(jnp.concatenate([r_hat[it], m_bd[it]], axis=0), st[p])
            o_ref[0, c * l:(c + 1) * l, p * pw:(p + 1) * pw] = both[:l] + y0[it]
            st[p] = both[l:] + c_bd[it]
    for p in range(n_pairs):
        st_ref[p] = st[p]

    y = o_ref[0]
    inv_n = 1.0 / HEAD_DIM
    mean = _seg_sum(y, ones_ref, 1) * inv_n
    yc = y - mean
    var = _seg_sum(yc * yc, ones_ref, 1) * inv_n
    yn = yc * lax.rsqrt(var + GN_EPS) * gnw_ref[...] + gnb_ref[...]
    bonus = _seg_sum(r * k * rk_ref[...], ones_ref, 1) * v
    o_ref[0] = (yn + bonus) * gate


def _rwkv_mix(rest, shift_mu, decay_w0, w_decay_up, iclr_a0, w_iclr_up, w_gate_lr_up, k_k, k_a,
              r_k, gn_w, gn_b):
    b, s, _ = rest.shape
    w = RWKV_WIDTH
    tc = 256
    assert s % tc == 0
    row = lambda x: x.reshape(1, -1)
    full = lambda shape: pl.BlockSpec(shape, lambda bi, i: (0, 0))
    return pl.pallas_call(
        functools.partial(_rwkv_kernel, n_chunks=tc // CHUNK),
        grid=(b, s // tc),
        in_specs=[
            pl.BlockSpec((1, tc, SHIFT_WIDTH), lambda bi, i: (bi, i, 0)),
            full((1, SHIFT_WIDTH)), full((1, w)), full((DECAY_LORA, w)), full((1, w)),
            full((ICLR_LORA, w)), full((GATE_LORA, w)), full((1, w)), full((1, w)),
            full((1, w)), full((1, w)), full((1, w)),
        ],
        out_specs=pl.BlockSpec((1, tc, w), lambda bi, i: (bi, i, 0)),
        out_shape=jax.ShapeDtypeStruct((b, s, w), F32),
        scratch_shapes=[
            pltpu.VMEM((w // LANES, LANES, LANES), F32),
            pltpu.VMEM((1, SHIFT_WIDTH), F32),
            pltpu.VMEM((tc, tc), BF16),
            pltpu.VMEM((2 * LANES, 2 * LANES), BF16),
        ],
        compiler_params=_params(("parallel", "arbitrary")),
        name="rwkv7_mix",
    )(rest, row(shift_mu), row(decay_w0), w_decay_up, row(iclr_a0), w_iclr_up, w_gate_lr_up,
      row(k_k), row(k_a), row(r_k), row(gn_w), row(gn_b))


def _mix_kernel(attn_ref, rw_ref, x_ref, wo_ref, nf_ref, wr_ref, br_ref,
                h_out, xn_out, idx_out, gate_out, rank_out, cnt_out, cnt_ref):
    @pl.when(pl.program_id(0) == 0)
    def _():
        cnt_ref[...] = jnp.zeros_like(cnt_ref)

    mixed = (jnp.dot(attn_ref[...].astype(BF16), wo_ref[:ATTN_WIDTH, :], preferred_element_type=F32)
             + jnp.dot(rw_ref[...].astype(BF16), wo_ref[ATTN_WIDTH:, :], preferred_element_type=F32))
    h = x_ref[...] + mixed
    h_out[...] = h
    xn = h * lax.rsqrt(jnp.mean(h * h, axis=-1, keepdims=True) + RMS_EPS) * nf_ref[...]
    _store_row_tiles(xn_out, xn)
    logits = _dot_hi(xn, wr_ref[...]) + br_ref[...]
    eidx = lax.broadcasted_iota(I32, logits.shape, 1).astype(F32)
    lane = lax.broadcasted_iota(I32, (logits.shape[0], LANES), 1)
    idx_pad = jnp.zeros((logits.shape[0], LANES), F32)
    val_pad = jnp.full((logits.shape[0], LANES), MASK_VALUE, F32)
    cur = logits
    sels = []
    for kth in range(TOP_K):
        m = jnp.max(cur, axis=-1, keepdims=True)
        sel = jnp.min(jnp.where(cur == m, eidx, float(N_EXPERTS)), axis=-1, keepdims=True)
        sels.append(sel)
        idx_pad = jnp.where(lane == kth, sel, idx_pad)
        val_pad = jnp.where(lane == kth, m, val_pad)
        cur = jnp.where(eidx == sel, -jnp.inf, cur)
    top = jnp.max(val_pad, axis=-1, keepdims=True)
    e = jnp.exp(val_pad - top)
    gate_out[...] = e / jnp.sum(e, axis=-1, keepdims=True)
    idx_out[...] = idx_pad.astype(I32)

    tm = logits.shape[0]
    lane_f = lane.astype(F32)
    picked = [lane_f == sel for sel in sels]
    onehot = sum(jnp.where(pk, 1.0, 0.0) for pk in picked)
    tr = lax.broadcasted_iota(I32, (tm, tm), 0)
    tc = lax.broadcasted_iota(I32, (tm, tm), 1)
    earlier = jnp.where(tr > tc, 1.0, 0.0).astype(BF16)
    before = jnp.dot(earlier, onehot.astype(BF16), preferred_element_type=F32) + cnt_ref[...]
    rank_pad = jnp.zeros((tm, LANES), F32)
    for kth, pk in enumerate(picked):
        rank_k = jnp.sum(jnp.where(pk, before, 0.0), axis=-1, keepdims=True)
        rank_pad = jnp.where(lane == kth, rank_k, rank_pad)
    rank_out[...] = rank_pad.astype(I32)
    counts = cnt_ref[...] + jnp.sum(onehot, axis=0, keepdims=True)
    cnt_ref[...] = counts
    cnt_out[...] = counts


def _mix_out(attn, rw, xf, w_out, norm_ffn_g, w_router, b_router):
    n = xf.shape[0]
    tm = 256
    row = lambda x: x.reshape(1, -1)
    half = pl.BlockSpec((tm, RWKV_WIDTH), lambda i: (i, 0))
    wide = pl.BlockSpec((tm, D_MODEL), lambda i: (i, 0))
    pad = pl.BlockSpec((tm, LANES), lambda i: (i, 0))
    full = lambda shape: pl.BlockSpec(shape, lambda i: (0, 0))
    return pl.pallas_call(
        _mix_kernel,
        grid=(n // tm,),
        in_specs=[half, half, wide, full((D_MODEL, D_MODEL)), full((1, D_MODEL)),
                  full((D_MODEL, N_EXPERTS)), full((1, N_EXPERTS))],
        out_specs=[wide, pl.BlockSpec((tm * ROW_TILE, LANES), lambda i: (i, 0)), pad, pad, pad,
                   full((1, LANES))],
        out_shape=[jax.ShapeDtypeStruct((n, D_MODEL), F32),
                   jax.ShapeDtypeStruct((n * ROW_TILE, LANES), F32),
                   jax.ShapeDtypeStruct((n, LANES), I32), jax.ShapeDtypeStruct((n, LANES), F32),
                   jax.ShapeDtypeStruct((n, LANES), I32), jax.ShapeDtypeStruct((1, LANES), F32)],
        scratch_shapes=[pltpu.VMEM((1, LANES), F32)],
        compiler_params=_params(("arbitrary",)),
        name="mix_out_router",
    )(attn, rw, xf, w_out.astype(BF16), row(norm_ffn_g), w_router, row(b_router))


def _dispatch_kernel(pad_ref, nvalid_ref, dest_ref, x_ref, out_ref, zbuf, sem, zsem,
                     *, tm, bm, n_blocks):
    tile = lambda row, n_rows: pl.ds(pl.multiple_of(row * ROW_TILE, ROW_TILE), n_rows * ROW_TILE)

    @pl.when(pl.program_id(0) == 0)
    def _():
        zbuf[...] = jnp.zeros_like(zbuf)

        def fill(start):
            cp = pltpu.make_async_copy(zbuf, out_ref.at[tile(start, bm), :], zsem)
            cp.start()
            cp.wait()

        def pad_fill(e, carry):
            fill(pad_ref[e])
            return carry

        def tail_fill(blk, carry):
            fill(blk * bm)
            return carry

        lax.fori_loop(0, N_EXPERTS, pad_fill, 0)
        lax.fori_loop(nvalid_ref[0], n_blocks + 1, tail_fill, 0)

    def issue(j, carry):
        for kth in range(TOP_K):
            pltpu.make_async_copy(
                x_ref.at[tile(j, 1), :],
                out_ref.at[tile(dest_ref[j * TOP_K + kth], 1), :],
                sem).start(priority=kth % 2)
        return carry

    lax.fori_loop(0, tm, issue, 0, unroll=8)
    for kth in range(TOP_K):
        pltpu.make_async_copy(x_ref, out_ref.at[tile(0, tm), :], sem).wait()


def _dispatch(xn_tiles, dest_flat, pad_start, n_valid, n_blocks, bm):
    n = xn_tiles.shape[0] // ROW_TILE
    tm = 256
    return pl.pallas_call(
        functools.partial(_dispatch_kernel, tm=tm, bm=bm, n_blocks=n_blocks),
        grid_spec=pltpu.PrefetchScalarGridSpec(
            num_scalar_prefetch=2,
            grid=(n // tm,),
            in_specs=[
                pl.BlockSpec((tm * TOP_K,), lambda i, ps, nv: (i,), memory_space=pltpu.SMEM),
                pl.BlockSpec((tm * ROW_TILE, LANES), lambda i, ps, nv: (i, 0)),
            ],
            out_specs=pl.BlockSpec(memory_space=pl.ANY),
            scratch_shapes=[pltpu.VMEM((bm * ROW_TILE, LANES), F32), pltpu.SemaphoreType.DMA,
                            pltpu.SemaphoreType.DMA],
        ),
        out_shape=jax.ShapeDtypeStruct(((n_blocks + 1) * bm * ROW_TILE, LANES), F32),
        compiler_params=_params(("arbitrary",)),
        name="moe_dispatch",
    )(pad_start, n_valid, dest_flat, xn_tiles)


def _expert_kernel(be_ref, nvalid_ref, x_ref, wgu_ref, bgu_ref, wd_ref, bd_ref, o_ref,
                   wgu_bf, wd_bf):
    i = pl.program_id(0)
    cw = 256

    @pl.when(jnp.logical_or(i == 0, be_ref[i] != be_ref[jnp.maximum(i - 1, 0)]))
    def _():
        for j in range(2 * D_FF // cw):
            wgu_bf[:, j * cw:(j + 1) * cw] = wgu_ref[0, :, j * cw:(j + 1) * cw].astype(BF16)
        for j in range(D_MODEL // cw):
            wd_bf[:, j * cw:(j + 1) * cw] = wd_ref[0, :, j * cw:(j + 1) * cw].astype(BF16)

    @pl.when(i < nvalid_ref[0])
    def _():
        bm = x_ref.shape[0] // ROW_TILE
        x = _load_row_tiles(x_ref, bm).astype(BF16)
        acc = jnp.zeros((bm, D_MODEL), F32)
        for j in range(D_FF // cw):
            g = jnp.dot(x, wgu_bf[:, j * cw:(j + 1) * cw], preferred_element_type=F32)
            g = g + bgu_ref[0, :, j * cw:(j + 1) * cw]
            u = jnp.dot(x, wgu_bf[:, D_FF + j * cw:D_FF + (j + 1) * cw],
                        preferred_element_type=F32)
            u = u + bgu_ref[0, :, D_FF + j * cw:D_FF + (j + 1) * cw]
            g = jnp.minimum(g, SWIGLU_LIMIT)
            u = jnp.clip(u, -SWIGLU_LIMIT, SWIGLU_LIMIT)
            act = (u + 1.0) * (g * _sigmoid(SWIGLU_ALPHA * g))
            acc = acc + jnp.dot(act.astype(BF16), wd_bf[j * cw:(j + 1) * cw, :],
                                preferred_element_type=F32)
        _store_row_tiles(o_ref, acc + bd_ref[0])

    @pl.when(i >= nvalid_ref[0])
    def _():
        o_ref[...] = jnp.zeros_like(o_ref)


def _experts(xs, blk_expert, n_valid, w_gate_up, b_gate_up, w_down, b_down, n_blocks, bm):
    n_rows = n_blocks * bm
    return pl.pallas_call(
        _expert_kernel,
        grid_spec=pltpu.PrefetchScalarGridSpec(
            num_scalar_prefetch=2,
            grid=(n_blocks,),
            in_specs=[
                pl.BlockSpec((bm * ROW_TILE, LANES), lambda i, be, nv: (i, 0)),
                pl.BlockSpec((1, D_MODEL, 2 * D_FF), lambda i, be, nv: (be[i], 0, 0)),
                pl.BlockSpec((1, 1, 2 * D_FF), lambda i, be, nv: (be[i], 0, 0)),
                pl.BlockSpec((1, D_FF, D_MODEL), lambda i, be, nv: (be[i], 0, 0)),
                pl.BlockSpec((1, 1, D_MODEL), lambda i, be, nv: (be[i], 0, 0)),
            ],
            out_specs=pl.BlockSpec((bm * ROW_TILE, LANES), lambda i, be, nv: (i, 0)),
            scratch_shapes=[pltpu.VMEM((D_MODEL, 2 * D_FF), BF16), pltpu.VMEM((D_FF, D_MODEL), BF16)],
        ),
        out_shape=jax.ShapeDtypeStruct((n_rows * ROW_TILE, LANES), F32),
        compiler_params=_params(("arbitrary",)),
        name="moe_experts",
    )(blk_expert, n_valid, xs, w_gate_up, b_gate_up.reshape(N_EXPERTS, 1, 2 * D_FF),
      w_down, b_down.reshape(N_EXPERTS, 1, D_MODEL))


def _combine_kernel(dest_ref, dest_next_ref, h_ref, gate_ref, g_ref, ys_ref, o_ref, buf, sem,
                    *, tm, n_tiles):
    i = pl.program_id(0)
    slot = i % 2
    tile = lambda row, n_rows: pl.ds(pl.multiple_of(row * ROW_TILE, ROW_TILE), n_rows * ROW_TILE)

    def issue(dref, s):
        def body(j, carry):
            for kth in range(TOP_K):
                pltpu.make_async_copy(
                    ys_ref.at[tile(dref[j * TOP_K + kth], 1), :],
                    buf.at[s, kth, tile(j, 1), :],
                    sem.at[s]).start(priority=kth % 2)
            return carry

        lax.fori_loop(0, tm, body, 0, unroll=8)

    @pl.when(i == 0)
    def _():
        issue(dest_ref, 0)

    def step(s):
        @pl.when(i + 1 < n_tiles)
        def _():
            issue(dest_next_ref, 1 - s)

        for kth in range(TOP_K):
            pltpu.make_async_copy(ys_ref.at[tile(0, tm), :], buf.at[s, kth], sem.at[s]).wait()
        gates = gate_ref[...]
        hf = h_ref[...]
        for kth in range(TOP_K):
            hf = hf + gates[:, kth:kth + 1] * _load_row_tiles(buf.at[s, kth], tm)
        o_ref[...] = hf * lax.rsqrt(jnp.mean(hf * hf, axis=-1, keepdims=True) + RMS_EPS) * g_ref[...]

    for s in range(2):
        pl.when(slot == s)(functools.partial(step, s))


def _combine(h, gate_pad, ys, dest_flat, norm_final_g):
    n = h.shape[0]
    tm = 256
    n_tiles = n // tm
    return pl.pallas_call(
        functools.partial(_combine_kernel, tm=tm, n_tiles=n_tiles),
        grid=(n_tiles,),
        in_specs=[
            pl.BlockSpec((tm * TOP_K,), lambda i: (i,), memory_space=pltpu.SMEM),
            pl.BlockSpec((tm * TOP_K,), lambda i: (jnp.minimum(i + 1, n_tiles - 1),),
                         memory_space=pltpu.SMEM),
            pl.BlockSpec((tm, D_MODEL), lambda i: (i, 0)),
            pl.BlockSpec((tm, LANES), lambda i: (i, 0)),
            pl.BlockSpec((1, D_MODEL), lambda i: (0, 0)),
            pl.BlockSpec(memory_space=pl.ANY),
        ],
        out_specs=pl.BlockSpec((tm, D_MODEL), lambda i: (i, 0)),
        out_shape=jax.ShapeDtypeStruct((n, D_MODEL), F32),
        scratch_shapes=[pltpu.VMEM((2, TOP_K, tm * ROW_TILE, LANES), F32),
                        pltpu.SemaphoreType.DMA((2,))],
        compiler_params=_params(("arbitrary",)),
        name="moe_combine_norm",
    )(dest_flat, dest_flat, h, gate_pad, norm_final_g.reshape(1, D_MODEL), ys)


def _routing(top_idx, rank, counts, bm):
    n = top_idx.shape[0]
    padded = (counts + bm - 1) // bm * bm
    pend = jnp.cumsum(padded)
    pstart = pend - padded
    dest = pstart[top_idx] + rank
    n_blocks = (n * TOP_K + N_EXPERTS * (bm - 1) + bm - 1) // bm
    blk_first_row = jnp.arange(n_blocks, dtype=pend.dtype) * bm
    blk_expert = jnp.minimum(
        jnp.sum((pend[None, :] <= blk_first_row[:, None]).astype(I32), axis=1),
        N_EXPERTS - 1).astype(I32)
    n_valid = (pend[-1:] // bm).astype(I32)
    pad_start = (pstart + counts).astype(I32)
    return dest.reshape(n * TOP_K).astype(I32), blk_expert, n_valid, pad_start, n_blocks


def _layer(xf, b, s, p):
    qkv, rest = _in_proj(xf, p["norm_mix_g"], p["w_in"])
    slopes = jnp.exp2(-8.0 / ATTN_HEADS * jnp.arange(1, ATTN_HEADS + 1, dtype=F32))
    attn = _attention(qkv.reshape(b, s, 3 * ATTN_WIDTH), slopes)
    rw = _rwkv_mix(
        rest.reshape(b, s, SHIFT_WIDTH), p["shift_mu"], p["decay_w0"], p["w_decay_up"],
        p["iclr_a0"], p["w_iclr_up"], p["w_gate_lr_up"], p["k_k"], p["k_a"],
        p["r_k"].reshape(-1), p["gn_w"], p["gn_b"])
    flat = lambda t: t.reshape(b * s, -1)
    h, xn, idx_pad, gate_pad, rank_pad, counts = _mix_out(
        flat(attn), flat(rw), xf, p["w_out"], p["norm_ffn_g"], p["w_router"], p["b_router"])
    bm = 512
    dest, blk_expert, n_valid, pad_start, n_blocks = _routing(
        idx_pad[:, :TOP_K], rank_pad[:, :TOP_K], counts[0, :N_EXPERTS].astype(I32), bm)
    xs = _dispatch(xn, dest, pad_start, n_valid, n_blocks, bm)
    ys = _experts(xs, blk_expert, n_valid, p["w_gate_up"], p["b_gate_up"], p["w_down"],
                  p["b_down"], n_blocks, bm)
    return h, gate_pad, ys, dest


def kernel(x, norm_mix_g, w_in, shift_mu, decay_w0, w_decay_up, iclr_a0, w_iclr_up, w_gate_lr_up,
           k_k, k_a, r_k, gn_w, gn_b, w_out, norm_ffn_g, w_router, b_router, w_gate_up,
           b_gate_up, w_down, b_down, norm_final_g):
    b, s, d = x.shape
    assert d == D_MODEL and w_in.shape[0] == 1, "single-layer block"
    names = ("norm_mix_g", "w_in", "shift_mu", "decay_w0", "w_decay_up", "iclr_a0", "w_iclr_up",
             "w_gate_lr_up", "k_k", "k_a", "r_k", "gn_w", "gn_b", "w_out", "norm_ffn_g",
             "w_router", "b_router", "w_gate_up", "b_gate_up", "w_down", "b_down")
    vals = (norm_mix_g, w_in, shift_mu, decay_w0, w_decay_up, iclr_a0, w_iclr_up, w_gate_lr_up,
            k_k, k_a, r_k, gn_w, gn_b, w_out, norm_ffn_g, w_router, b_router, w_gate_up,
            b_gate_up, w_down, b_down)
    p = {nm: v[0] for nm, v in zip(names, vals)}
    h, gate_pad, ys, dest = _layer(x.reshape(b * s, d), b, s, p)
    out = _combine(h, gate_pad, ys, dest, norm_final_g)
    return out.reshape(b, s, d)
```

```python
import functools
import math

import jax
import jax.numpy as jnp
from jax import lax
from jax.experimental import pallas as pl
from jax.experimental.pallas import tpu as pltpu

F32 = jnp.float32
BF16 = jnp.bfloat16
I32 = jnp.int32

D_MODEL = 1024
HEAD_DIM = 64
ATTN_WIDTH = 512
ATTN_HEADS = ATTN_WIDTH // HEAD_DIM
RWKV_WIDTH = 512
RWKV_HEADS = RWKV_WIDTH // HEAD_DIM
DILATED_PATTERNS = ((128, 1), (512, 4), (2048, 16))
ATTN_BLOCK = 128
DECAY_LORA = 64
ICLR_LORA = 64
GATE_LORA = 128
SHIFT_WIDTH = 3 * RWKV_WIDTH + DECAY_LORA + ICLR_LORA + GATE_LORA
N_EXPERTS = 32
TOP_K = 4
D_FF = D_MODEL
SWIGLU_LIMIT = 7.0
SWIGLU_ALPHA = 1.702
RMS_EPS = 1e-5
GN_EPS = 64e-5

LANES = 128
SUBLANES = 8
ROW_TILE = D_MODEL // LANES
assert ROW_TILE == SUBLANES, "a D_MODEL-wide f32 row must fill exactly one (8,128) tile"
CHUNK = 64
MASK_VALUE = -1e30
LOG2_E = 1.4426950408889634
ATTN_GROUP = 4
VMEM_LIMIT = 56 * 1024 * 1024

TOKEN_TILE = 512
RWKV_TILE = 256
EXPERT_BLOCK = 512


def _params(semantics, vmem=VMEM_LIMIT):
    return pltpu.CompilerParams(dimension_semantics=semantics, vmem_limit_bytes=vmem)


def _dot(a, b):
    return jnp.dot(a.astype(BF16), b.astype(BF16), preferred_element_type=F32)


def _dot_nt(a, b):
    return lax.dot_general(a.astype(BF16), b.astype(BF16), (((1,), (1,)), ((), ())),
                           preferred_element_type=F32)


def _dot_tn(a, b):
    return lax.dot_general(a.astype(BF16), b.astype(BF16), (((0,), (0,)), ((), ())),
                           preferred_element_type=F32)


def _split3(x):
    h = x.astype(BF16)
    r1 = x - h.astype(F32)
    m = r1.astype(BF16)
    l = (r1 - m.astype(F32)).astype(BF16)
    return h, m, l


def _dot_exact_lhs(a_bf16, x):
    h, m, l = _split3(x)
    d = lambda y: jnp.dot(a_bf16, y, preferred_element_type=F32)
    return d(h) + d(m) + d(l)


def _dot_exact_rhs(x, b_bf16):
    h, m, l = _split3(x)
    d = lambda y: jnp.dot(y, b_bf16, preferred_element_type=F32)
    return d(h) + d(m) + d(l)


def _dot_hi(a, b):
    ah = a.astype(BF16)
    al = (a - ah.astype(F32)).astype(BF16)
    bh = b.astype(BF16)
    bl = (b - bh.astype(F32)).astype(BF16)
    d = lambda x, y: jnp.dot(x, y, preferred_element_type=F32)
    return d(ah, bh) + d(ah, bl) + d(al, bh)


def _store_row_tiles(ref, x):
    rows = x.shape[0]
    for s in range(ROW_TILE):
        ref[pl.ds(s, rows, stride=ROW_TILE), :] = x[:, s * LANES:(s + 1) * LANES]


def _load_row_tiles(ref, rows):
    return jnp.concatenate(
        [ref[pl.ds(s, rows, stride=ROW_TILE), :] for s in range(ROW_TILE)], axis=1)


def _sigmoid(x):
    return 1.0 / (1.0 + jnp.exp(-x))


def _head_ones(width):
    r = lax.broadcasted_iota(I32, (width, width), 0) // HEAD_DIM
    c = lax.broadcasted_iota(I32, (width, width), 1) // HEAD_DIM
    return jnp.where(r == c, 1.0, 0.0).astype(BF16)


def _inproj_kernel(x_ref, g_ref, wq_ref, wr_ref, qkv_ref, rest_ref):
    x = x_ref[...]
    xn = x * lax.rsqrt(jnp.mean(x * x, axis=-1, keepdims=True) + RMS_EPS) * g_ref[...]
    xb = xn.astype(BF16)
    qkv_ref[...] = jnp.dot(xb, wq_ref[...], preferred_element_type=F32)
    rest_ref[...] = jnp.dot(xb, wr_ref[...], preferred_element_type=F32)


def _in_proj(xf, g, w_in):
    n = xf.shape[0]
    tm = TOKEN_TILE
    wq = w_in[:, :3 * ATTN_WIDTH].astype(BF16)
    wr = w_in[:, 3 * ATTN_WIDTH:].astype(BF16)
    return pl.pallas_call(
        _inproj_kernel,
        grid=(n // tm,),
        in_specs=[
            pl.BlockSpec((tm, D_MODEL), lambda i: (i, 0)),
            pl.BlockSpec((1, D_MODEL), lambda i: (0, 0)),
            pl.BlockSpec((D_MODEL, 3 * ATTN_WIDTH), lambda i: (0, 0)),
            pl.BlockSpec((D_MODEL, SHIFT_WIDTH), lambda i: (0, 0)),
        ],
        out_specs=[
            pl.BlockSpec((tm, 3 * ATTN_WIDTH), lambda i: (i, 0)),
            pl.BlockSpec((tm, SHIFT_WIDTH), lambda i: (i, 0)),
        ],
        out_shape=[
            jax.ShapeDtypeStruct((n, 3 * ATTN_WIDTH), F32),
            jax.ShapeDtypeStruct((n, SHIFT_WIDTH), F32),
        ],
        compiler_params=_params(("parallel",)),
        name="in_proj",
    )(xf, g.reshape(1, D_MODEL), wq, wr)


def _attn_kernel(slopes_ref, q_ref, k_ref, v_ref, o_ref, bias_ref, m_acc, d_acc, n_acc, *, seq):
    c = ATTN_BLOCK
    hp = pl.program_id(1)
    lane = lax.broadcasted_iota(I32, (c, LANES), 1)
    lo_half = lane < HEAD_DIM
    qi = lax.broadcasted_iota(I32, (c, 2 * c), 0)
    kj = lax.broadcasted_iota(I32, (c, 2 * c), 1)
    diff = qi + c - kj
    for p, (window, dil) in enumerate(DILATED_PATTERNS):
        steps = window // dil
        valid = (diff >= 0) & (diff <= steps)
        dist = (dil * diff).astype(F32)
        for hh in range(2):
            slope = slopes_ref[hp * 2 + hh] * LOG2_E
            bias = jnp.where(valid, -slope * dist, MASK_VALUE)
            bias_ref[2 * (p * 2 + hh)] = bias
            bias_ref[2 * (p * 2 + hh) + 1] = jnp.where(kj >= c, bias, MASK_VALUE)

    scale = LOG2_E / math.sqrt(HEAD_DIM)
    order = sorted(range(len(DILATED_PATTERNS)), key=lambda p: -DILATED_PATTERNS[p][1])

    def rows(start, dil):
        if dil == 1:
            return pl.ds(pl.multiple_of(start, c), c)
        return pl.ds(start, c, stride=dil)

    def group(i, carry, p, dil, nb):
        blocks = []
        for g in range(ATTN_GROUP):
            j = i * ATTN_GROUP + g
            if nb >= ATTN_GROUP:
                r, n = j // nb, j % nb
                first = (n == 0) if g == 0 else False
            else:
                r, n = j // nb, g % nb
                first = n == 0
            blocks.append((r + dil * c * n, r + dil * c * jnp.maximum(n - 1, 0), first))
        qs, ks, vs = [], [], []
        for cur, prev, first in blocks:
            qs.append(q_ref[0, rows(cur, dil), :] * scale)
            kc = k_ref[0, rows(cur, dil), :].astype(BF16)
            vc = v_ref[0, rows(cur, dil), :].astype(BF16)
            if first is True:
                ks.append(kc)
                vs.append(vc)
            else:
                ks.append(jnp.concatenate([k_ref[0, rows(prev, dil), :].astype(BF16), kc], axis=0))
                vs.append(jnp.concatenate([v_ref[0, rows(prev, dil), :].astype(BF16), vc], axis=0))
        scores = []
        for (cur, prev, first), qf, kcat in zip(blocks, qs, ks):
            for hh in range(2):
                head_mask = lo_half if hh == 0 else jnp.logical_not(lo_half)
                qh = jnp.where(head_mask, qf, 0.0).astype(BF16)
                s = lax.dot_general(qh, kcat, (((1,), (1,)), ((), ())), preferred_element_type=F32)
                slot = 2 * (p * 2 + hh)
                if first is True:
                    s = s + bias_ref[slot][:, c:]
                elif first is False:
                    s = s + bias_ref[slot]
                else:
                    s = s + bias_ref[slot + first.astype(I32)]
                scores.append(s)
        probs = []
        for s in scores:
            m = jnp.max(s, axis=-1, keepdims=True)
            e = jnp.exp2(s - m)
            probs.append((m, jnp.sum(e, axis=-1, keepdims=True), e.astype(BF16)))
        for bi, (cur, prev, first) in enumerate(blocks):
            (m0, d0, e0), (m1, d1, e1) = probs[2 * bi], probs[2 * bi + 1]
            o0 = jnp.dot(e0, vs[bi], preferred_element_type=F32)
            o1 = jnp.dot(e1, vs[bi], preferred_element_type=F32)
            m_b = jnp.where(lo_half, m0, m1)
            d_b = jnp.where(lo_half, d0, d1)
            n_b = jnp.where(lo_half, o0, o1)
            sl = rows(cur, dil)
            if p == order[0]:
                m_acc[sl, :] = m_b
                d_acc[sl, :] = d_b
                n_acc[sl, :] = n_b
            else:
                m_o = m_acc[sl, :]
                m_n = jnp.maximum(m_o, m_b)
                a_o = jnp.exp2(m_o - m_n)
                a_b = jnp.exp2(m_b - m_n)
                d_n = d_acc[sl, :] * a_o + d_b * a_b
                n_n = n_acc[sl, :] * a_o + n_b * a_b
                if p == order[-1]:
                    o_ref[0, sl, :] = n_n / d_n
                else:
                    d_acc[sl, :] = d_n
                    n_acc[sl, :] = n_n
                    m_acc[sl, :] = m_n
        return carry

    for p in order:
        dil = DILATED_PATTERNS[p][1]
        nb = seq // (c * dil)
        n_groups = (nb * dil) // ATTN_GROUP
        lax.fori_loop(0, n_groups, functools.partial(group, p=p, dil=dil, nb=nb), 0)


def _attention(qkv, slopes):
    b, s, _ = qkv.shape
    for window, dil in DILATED_PATTERNS:
        nb = s // (ATTN_BLOCK * dil)
        assert s % (ATTN_BLOCK * dil) == 0 and window // dil <= ATTN_BLOCK
        assert nb % ATTN_GROUP == 0 or (ATTN_GROUP % nb == 0 and (nb * dil) % ATTN_GROUP == 0)
    n_pairs = ATTN_WIDTH // LANES
    blk = lambda off: pl.BlockSpec((1, s, LANES), lambda bi, hp, sl: (bi, 0, off + hp))
    return pl.pallas_call(
        functools.partial(_attn_kernel, seq=s),
        grid_spec=pltpu.PrefetchScalarGridSpec(
            num_scalar_prefetch=1,
            grid=(b, n_pairs),
            in_specs=[blk(0), blk(n_pairs), blk(2 * n_pairs)],
            out_specs=pl.BlockSpec((1, s, LANES), lambda bi, hp, sl: (bi, 0, hp)),
            scratch_shapes=[
                pltpu.VMEM((4 * len(DILATED_PATTERNS), ATTN_BLOCK, 2 * ATTN_BLOCK), F32),
                pltpu.VMEM((s, LANES), F32),
                pltpu.VMEM((s, LANES), F32),
                pltpu.VMEM((s, LANES), F32),
            ],
        ),
        out_shape=jax.ShapeDtypeStruct((b, s, ATTN_WIDTH), F32),
        compiler_params=_params(("parallel", "parallel")),
        name="dilated_attention",
    )(slopes, qkv, qkv, qkv)


def _seg_sum(x, ones_ref, passes):
    hw = ones_ref.shape[0]
    terms = [x.astype(BF16)]
    if passes == 2:
        terms.append((x - terms[0].astype(F32)).astype(BF16))
    parts = []
    for j in range(x.shape[1] // hw):
        sl = slice(j * hw, (j + 1) * hw)
        parts.append(sum(jnp.dot(t[:, sl], ones_ref[...], preferred_element_type=F32) for t in terms))
    return jnp.concatenate(parts, axis=1)


def _rwkv_kernel(rest_ref, mu_ref, w0_ref, wdu_ref, a0_ref, wau_ref, wg_ref, kk_ref, ka_ref,
                 rk_ref, gnw_ref, gnb_ref, o_ref, st_ref, prev_ref, tri_ref, ones_ref, *, n_chunks):
    l = CHUNK
    pw = LANES
    n_pairs = RWKV_WIDTH // pw
    tc = n_chunks * l
    w = RWKV_WIDTH

    @pl.when(pl.program_id(1) == 0)
    def _():
        st_ref[...] = jnp.zeros_like(st_ref)
        prev_ref[...] = jnp.zeros_like(prev_ref)
        ri = lax.broadcasted_iota(I32, (tc, tc), 0)
        ci = lax.broadcasted_iota(I32, (tc, tc), 1)
        tri_ref[...] = jnp.where((ri >= ci) & (ri // l == ci // l), 1.0, 0.0).astype(BF16)
        hr = lax.broadcasted_iota(I32, ones_ref.shape, 0) // HEAD_DIM
        hc = lax.broadcasted_iota(I32, ones_ref.shape, 1) // HEAD_DIM
        ones_ref[...] = jnp.where(hr == hc, 1.0, 0.0).astype(BF16)

    z = rest_ref[0]
    zrow = lax.broadcasted_iota(I32, z.shape, 0)
    zprev = jnp.where(zrow == 0, prev_ref[...], pltpu.roll(z, 1, 0))
    prev_ref[...] = z[tc - 1:tc, :]
    xs = z + (zprev - z) * mu_ref[...]
    r = xs[:, :w]
    k_in = xs[:, w:2 * w]
    v = xs[:, 2 * w:3 * w]
    wd = xs[:, 3 * w:3 * w + DECAY_LORA]
    ad = xs[:, 3 * w + DECAY_LORA:3 * w + DECAY_LORA + ICLR_LORA]
    gd = xs[:, 3 * w + DECAY_LORA + ICLR_LORA:]
    zz = w0_ref[...] + _dot_hi(jnp.tanh(wd), wdu_ref[...])
    lw = -math.exp(-0.5) * _sigmoid(zz)
    a = _sigmoid(a0_ref[...] + _dot(ad, wau_ref[...]))
    gate = _dot(_sigmoid(gd), wg_ref[...])
    kk = k_in * kk_ref[...]
    kk = kk * lax.rsqrt(jnp.maximum(_seg_sum(kk * kk, ones_ref, 2), 1e-24))
    k = k_in * (1.0 + (a - 1.0) * ka_ref[...])
    cum = _dot_exact_lhs(tri_ref[...], lw)
    tot = jnp.concatenate(
        [jnp.broadcast_to(cum[c * l + l - 1:c * l + l, :], (l, w)) for c in range(n_chunks)], axis=0)
    p_in = jnp.exp(cum)
    p_ex = jnp.exp(cum - lw)
    p_inv = jnp.exp(-cum)
    p_end = jnp.exp(tot - cum)
    p_tot = jnp.exp(tot)
    kka = kk * a
    al = -kk * p_ex
    be = kka * p_inv
    kt = k * p_inv
    rt = r * p_in
    bh = kka * p_end
    kh = k * p_end

    row = lax.broadcasted_iota(I32, (l, pw), 0)
    col = lax.broadcasted_iota(I32, (l, pw), 1) % l
    strict = row > col
    incl = row >= col
    eye_ss = jnp.where(row == col, 1.0, 0.0).astype(F32)
    row_bd = lax.broadcasted_iota(I32, (2 * l, pw), 0)
    lane_bd = lax.broadcasted_iota(I32, (2 * l, pw), 1)
    diag_mask = (row_bd // l) == (lane_bd // l)
    eye_bd = row_bd == lane_bd

    def bd(x):
        return jnp.where(diag_mask, jnp.concatenate([x, x], axis=0), 0.0)

    items = [(c, p) for c in range(n_chunks) for p in range(n_pairs)]
    cut = lambda t, c, p: t[c * l:(c + 1) * l, p * pw:(p + 1) * pw]

    zeros_bd = jnp.zeros((2 * l, pw), F32)
    zeros_ss = jnp.zeros((l, pw), F32)
    a_ab, a_rb, a_ak, a_rk, akv = {}, {}, {}, {}, {}
    for it in items:
        lhs = jnp.concatenate([cut(al, *it), cut(rt, *it)], axis=0)
        g = _dot_nt(lhs, jnp.concatenate([bd(cut(be, *it)), bd(cut(kt, *it))], axis=0))
        a_ab[it] = jnp.where(strict, g[:l, :pw], 0.0)
        a_ak[it] = jnp.where(strict, g[:l, pw:], 0.0)
        a_rb[it] = jnp.where(incl, g[l:, :pw], 0.0)
        a_rk[it] = jnp.where(incl, g[l:, pw:], 0.0)
    for it in items:
        akv[it] = _dot(a_ak[it], bd(cut(v, *it)))
    xs = {it: _dot(a_ab[it], bd(a_ab[it])) for it in items}
    ts = {it: eye_ss + a_ab[it] for it in items}
    for _ in range(int(math.log2(l)) - 2):
        for it in items:
            both = _dot(jnp.concatenate([xs[it], ts[it]], axis=0), bd(xs[it]))
            xs[it] = both[:l]
            ts[it] = ts[it] + both[l:]
    for it in items:
        ts[it] = ts[it] + _dot(ts[it], bd(xs[it]))
    w_t, u0 = {}, {}
    for it in items:
        wu = _dot(ts[it], jnp.concatenate([bd(cut(al, *it)), bd(akv[it])], axis=1))
        w_t[it] = wu[:, :pw]
        u0[it] = wu[:, pw:]
    r_hat, y0, m_bd, c_bd = {}, {}, {}, {}
    for it in items:
        v_i = cut(v, *it)
        ry = _dot(jnp.concatenate([a_rb[it], a_rk[it]], axis=1),
                  jnp.concatenate([jnp.concatenate([bd(w_t[it]), bd(u0[it])], axis=1),
                                   jnp.concatenate([zeros_bd, bd(v_i)], axis=1)], axis=0))
        r_hat[it] = cut(rt, *it) + ry[:, :pw]
        y0[it] = ry[:, pw:]
        mc = _dot_tn(jnp.concatenate([cut(bh, *it), cut(kh, *it)], axis=0),
                     jnp.concatenate([jnp.concatenate([w_t[it], u0[it]], axis=1),
                                      jnp.concatenate([zeros_ss, v_i], axis=1)], axis=0))
        decay = jnp.broadcast_to(cut(p_tot, *it)[:1], (2 * l, pw))
        m_bd[it] = jnp.where(diag_mask, mc[:, :pw], 0.0) + jnp.where(eye_bd, decay, 0.0)
        c_bd[it] = jnp.where(diag_mask, mc[:, pw:], 0.0)
    st = [st_ref[p] for p in range(n_pairs)]
    for c in range(n_chunks):
        for p in range(n_pairs):
            it = (c, p)
            both = _dot(jnp.concatenate([r_hat[it], m_bd[it]], axis=0), st[p])
            o_ref[0, c * l:(c + 1) * l, p * pw:(p + 1) * pw] = both[:l] + y0[it]
            st[p] = both[l:] + c_bd[it]
    for p in range(n_pairs):
        st_ref[p] = st[p]

    y = o_ref[0]
    inv_n = 1.0 / HEAD_DIM
    mean = _seg_sum(y, ones_ref, 1) * inv_n
    yc = y - mean
    var = _seg_sum(yc * yc, ones_ref, 1) * inv_n
    yn = yc * lax.rsqrt(var + GN_EPS) * gnw_ref[...] + gnb_ref[...]
    bonus = _seg_sum(r * k * rk_ref[...], ones_ref, 1) * v
    o_ref[0] = (yn + bonus) * gate


def _rwkv_mix(rest, shift_mu, decay_w0, w_decay_up, iclr_a0, w_iclr_up, w_gate_lr_up, k_k, k_a,
              r_k, gn_w, gn_b):
    b, s, _ = rest.shape
    w = RWKV_WIDTH
    tc = RWKV_TILE
    assert s % tc == 0
    row = lambda x: x.reshape(1, -1)
    full = lambda shape: pl.BlockSpec(shape, lambda bi, i: (0, 0))
    return pl.pallas_call(
        functools.partial(_rwkv_kernel, n_chunks=tc // CHUNK),
        grid=(b, s // tc),
        in_specs=[
            pl.BlockSpec((1, tc, SHIFT_WIDTH), lambda bi, i: (bi, i, 0)),
            full((1, SHIFT_WIDTH)), full((1, w)), full((DECAY_LORA, w)), full((1, w)),
            full((ICLR_LORA, w)), full((GATE_LORA, w)), full((1, w)), full((1, w)),
            full((1, w)), full((1, w)), full((1, w)),
        ],
        out_specs=pl.BlockSpec((1, tc, w), lambda bi, i: (bi, i, 0)),
        out_shape=jax.ShapeDtypeStruct((b, s, w), F32),
        scratch_shapes=[
            pltpu.VMEM((w // LANES, LANES, LANES), F32),
            pltpu.VMEM((1, SHIFT_WIDTH), F32),
            pltpu.VMEM((tc, tc), BF16),
            pltpu.VMEM((2 * LANES, 2 * LANES), BF16),
        ],
        compiler_params=_params(("parallel", "arbitrary")),
        name="rwkv7_mix",
    )(rest, row(shift_mu), row(decay_w0), w_decay_up, row(iclr_a0), w_iclr_up, w_gate_lr_up,
      row(k_k), row(k_a), row(r_k), row(gn_w), row(gn_b))


def _mix_kernel(attn_ref, rw_ref, x_ref, wo_ref, nf_ref, wr_ref, br_ref,
                h_out, xn_out, idx_out, gate_out, rank_out, cnt_out, cnt_ref):
    @pl.when(pl.program_id(0) == 0)
    def _():
        cnt_ref[...] = jnp.zeros_like(cnt_ref)

    mixed = (jnp.dot(attn_ref[...].astype(BF16), wo_ref[:ATTN_WIDTH, :], preferred_element_type=F32)
             + jnp.dot(rw_ref[...].astype(BF16), wo_ref[ATTN_WIDTH:, :], preferred_element_type=F32))
    h = x_ref[...] + mixed
    h_out[...] = h
    xn = h * lax.rsqrt(jnp.mean(h * h, axis=-1, keepdims=True) + RMS_EPS) * nf_ref[...]
    _store_row_tiles(xn_out, xn)
    logits = _dot_hi(xn, wr_ref[...]) + br_ref[...]
    eidx = lax.broadcasted_iota(I32, logits.shape, 1).astype(F32)
    lane = lax.broadcasted_iota(I32, (logits.shape[0], LANES), 1)
    idx_pad = jnp.zeros((logits.shape[0], LANES), F32)
    val_pad = jnp.full((logits.shape[0], LANES), MASK_VALUE, F32)
    cur = logits
    sels = []
    for kth in range(TOP_K):
        m = jnp.max(cur, axis=-1, keepdims=True)
        sel = jnp.min(jnp.where(cur == m, eidx, float(N_EXPERTS)), axis=-1, keepdims=True)
        sels.append(sel)
        idx_pad = jnp.where(lane == kth, sel, idx_pad)
        val_pad = jnp.where(lane == kth, m, val_pad)
        cur = jnp.where(eidx == sel, -jnp.inf, cur)
    top = jnp.max(val_pad, axis=-1, keepdims=True)
    e = jnp.exp(val_pad - top)
    gate_out[...] = e / jnp.sum(e, axis=-1, keepdims=True)
    idx_out[...] = idx_pad.astype(I32)

    tm = logits.shape[0]
    lane_f = lane.astype(F32)
    picked = [lane_f == sel for sel in sels]
    onehot = sum(jnp.where(pk, 1.0, 0.0) for pk in picked)
    tr = lax.broadcasted_iota(I32, (tm, tm), 0)
    tc = lax.broadcasted_iota(I32, (tm, tm), 1)
    earlier = jnp.where(tr > tc, 1.0, 0.0).astype(BF16)
    before = jnp.dot(earlier, onehot.astype(BF16), preferred_element_type=F32) + cnt_ref[...]
    rank_pad = jnp.zeros((tm, LANES), F32)
    for kth, pk in enumerate(picked):
        rank_k = jnp.sum(jnp.where(pk, before, 0.0), axis=-1, keepdims=True)
        rank_pad = jnp.where(lane == kth, rank_k, rank_pad)
    rank_out[...] = rank_pad.astype(I32)
    counts = cnt_ref[...] + jnp.sum(onehot, axis=0, keepdims=True)
    cnt_ref[...] = counts
    cnt_out[...] = counts


def _mix_out(attn, rw, xf, w_out, norm_ffn_g, w_router, b_router):
    n = xf.shape[0]
    tm = TOKEN_TILE
    row = lambda x: x.reshape(1, -1)
    half = pl.BlockSpec((tm, RWKV_WIDTH), lambda i: (i, 0))
    wide = pl.BlockSpec((tm, D_MODEL), lambda i: (i, 0))
    pad = pl.BlockSpec((tm, LANES), lambda i: (i, 0))
    full = lambda shape: pl.BlockSpec(shape, lambda i: (0, 0))
    return pl.pallas_call(
        _mix_kernel,
        grid=(n // tm,),
        in_specs=[half, half, wide, full((D_MODEL, D_MODEL)), full((1, D_MODEL)),
                  full((D_MODEL, N_EXPERTS)), full((1, N_EXPERTS))],
        out_specs=[wide, pl.BlockSpec((tm * ROW_TILE, LANES), lambda i: (i, 0)), pad, pad, pad,
                   full((1, LANES))],
        out_shape=[jax.ShapeDtypeStruct((n, D_MODEL), F32),
                   jax.ShapeDtypeStruct((n * ROW_TILE, LANES), F32),
                   jax.ShapeDtypeStruct((n, LANES), I32), jax.ShapeDtypeStruct((n, LANES), F32),
                   jax.ShapeDtypeStruct((n, LANES), I32), jax.ShapeDtypeStruct((1, LANES), F32)],
        scratch_shapes=[pltpu.VMEM((1, LANES), F32)],
        compiler_params=_params(("arbitrary",)),
        name="mix_out_router",
    )(attn, rw, xf, w_out.astype(BF16), row(norm_ffn_g), w_router, row(b_router))


def _dispatch_kernel(pad_ref, nvalid_ref, dest_ref, x_ref, out_ref, zbuf, sem, zsem,
                     *, tm, bm, n_blocks):
    tile = lambda row, n_rows: pl.ds(pl.multiple_of(row * ROW_TILE, ROW_TILE), n_rows * ROW_TILE)

    @pl.when(pl.program_id(0) == 0)
    def _():
        zbuf[...] = jnp.zeros_like(zbuf)

        def fill(start):
            cp = pltpu.make_async_copy(zbuf, out_ref.at[tile(start, bm), :], zsem)
            cp.start()
            cp.wait()

        def pad_fill(e, carry):
            fill(pad_ref[e])
            return carry

        def tail_fill(blk, carry):
            fill(blk * bm)
            return carry

        lax.fori_loop(0, N_EXPERTS, pad_fill, 0)
        lax.fori_loop(nvalid_ref[0], n_blocks + 1, tail_fill, 0)

    def issue(j, carry):
        for kth in range(TOP_K):
            pltpu.make_async_copy(
                x_ref.at[tile(j, 1), :],
                out_ref.at[tile(dest_ref[j * TOP_K + kth], 1), :],
                sem).start(priority=kth % 2)
        return carry

    lax.fori_loop(0, tm, issue, 0, unroll=8)
    for kth in range(TOP_K):
        pltpu.make_async_copy(x_ref, out_ref.at[tile(0, tm), :], sem).wait()


def _dispatch(xn_tiles, dest_flat, pad_start, n_valid, n_blocks, bm):
    n = xn_tiles.shape[0] // ROW_TILE
    tm = TOKEN_TILE
    return pl.pallas_call(
        functools.partial(_dispatch_kernel, tm=tm, bm=bm, n_blocks=n_blocks),
        grid_spec=pltpu.PrefetchScalarGridSpec(
            num_scalar_prefetch=2,
            grid=(n // tm,),
            in_specs=[
                pl.BlockSpec((tm * TOP_K,), lambda i, ps, nv: (i,), memory_space=pltpu.SMEM),
                pl.BlockSpec((tm * ROW_TILE, LANES), lambda i, ps, nv: (i, 0)),
            ],
            out_specs=pl.BlockSpec(memory_space=pl.ANY),
            scratch_shapes=[pltpu.VMEM((bm * ROW_TILE, LANES), F32), pltpu.SemaphoreType.DMA,
                            pltpu.SemaphoreType.DMA],
        ),
        out_shape=jax.ShapeDtypeStruct(((n_blocks + 1) * bm * ROW_TILE, LANES), F32),
        compiler_params=_params(("arbitrary",)),
        name="moe_dispatch",
    )(pad_start, n_valid, dest_flat, xn_tiles)


def _expert_kernel(be_ref, nvalid_ref, x_ref, wgu_ref, bgu_ref, wd_ref, bd_ref, o_ref,
                   wgu_bf, wd_bf):
    i = pl.program_id(0)
    cw = 256

    @pl.when(jnp.logical_or(i == 0, be_ref[i] != be_ref[jnp.maximum(i - 1, 0)]))
    def _():
        for j in range(2 * D_FF // cw):
            wgu_bf[:, j * cw:(j + 1) * cw] = wgu_ref[0, :, j * cw:(j + 1) * cw].astype(BF16)
        for j in range(D_MODEL // cw):
            wd_bf[:, j * cw:(j + 1) * cw] = wd_ref[0, :, j * cw:(j + 1) * cw].astype(BF16)

    @pl.when(i < nvalid_ref[0])
    def _():
        bm = x_ref.shape[0] // ROW_TILE
        x = _load_row_tiles(x_ref, bm).astype(BF16)
        acts = []
        gus = []
        for j in range(D_FF // cw):
            g = jnp.dot(x, wgu_bf[:, j * cw:(j + 1) * cw], preferred_element_type=F32)
            u = jnp.dot(x, wgu_bf[:, D_FF + j * cw:D_FF + (j + 1) * cw],
                        preferred_element_type=F32)
            gus.append((g + bgu_ref[0, :, j * cw:(j + 1) * cw],
                        u + bgu_ref[0, :, D_FF + j * cw:D_FF + (j + 1) * cw]))
        for g, u in gus:
            g = jnp.minimum(g, SWIGLU_LIMIT)
            u = jnp.clip(u, -SWIGLU_LIMIT, SWIGLU_LIMIT)
            acts.append(((u + 1.0) * (g * _sigmoid(SWIGLU_ALPHA * g))).astype(BF16))
        out = jnp.dot(jnp.concatenate(acts, axis=1), wd_bf[...], preferred_element_type=F32)
        _store_row_tiles(o_ref, out + bd_ref[0])

    @pl.when(i >= nvalid_ref[0])
    def _():
        o_ref[...] = jnp.zeros_like(o_ref)


def _experts(xs, blk_expert, n_valid, w_gate_up, b_gate_up, w_down, b_down, n_blocks, bm):
    n_rows = n_blocks * bm
    return pl.pallas_call(
        _expert_kernel,
        grid_spec=pltpu.PrefetchScalarGridSpec(
            num_scalar_prefetch=2,
            grid=(n_blocks,),
            in_specs=[
                pl.BlockSpec((bm * ROW_TILE, LANES), lambda i, be, nv: (i, 0)),
                pl.BlockSpec((1, D_MODEL, 2 * D_FF), lambda i, be, nv: (be[i], 0, 0)),
                pl.BlockSpec((1, 1, 2 * D_FF), lambda i, be, nv: (be[i], 0, 0)),
                pl.BlockSpec((1, D_FF, D_MODEL), lambda i, be, nv: (be[i], 0, 0)),
                pl.BlockSpec((1, 1, D_MODEL), lambda i, be, nv: (be[i], 0, 0)),
            ],
            out_specs=pl.BlockSpec((bm * ROW_TILE, LANES), lambda i, be, nv: (i, 0)),
            scratch_shapes=[pltpu.VMEM((D_MODEL, 2 * D_FF), BF16), pltpu.VMEM((D_FF, D_MODEL), BF16)],
        ),
        out_shape=jax.ShapeDtypeStruct((n_rows * ROW_TILE, LANES), F32),
        compiler_params=_params(("arbitrary",)),
        name="moe_experts",
    )(blk_expert, n_valid, xs, w_gate_up, b_gate_up.reshape(N_EXPERTS, 1, 2 * D_FF),
      w_down, b_down.reshape(N_EXPERTS, 1, D_MODEL))


def _combine_kernel(dest_ref, dest_next_ref, h_ref, gate_ref, g_ref, ys_ref, o_ref, buf, sem,
                    *, tm, n_tiles):
    i = pl.program_id(0)
    slot = i % 2
    tile = lambda row, n_rows: pl.ds(pl.multiple_of(row * ROW_TILE, ROW_TILE), n_rows * ROW_TILE)

    def issue(dref, s):
        def body(j, carry):
            for kth in range(TOP_K):
                pltpu.make_async_copy(
                    ys_ref.at[tile(dref[j * TOP_K + kth], 1), :],
                    buf.at[s, kth, tile(j, 1), :],
                    sem.at[s]).start(priority=kth % 2)
            return carry

        lax.fori_loop(0, tm, body, 0, unroll=8)

    @pl.when(i == 0)
    def _():
        issue(dest_ref, 0)

    def step(s):
        @pl.when(i + 1 < n_tiles)
        def _():
            issue(dest_next_ref, 1 - s)

        for kth in range(TOP_K):
            pltpu.make_async_copy(ys_ref.at[tile(0, tm), :], buf.at[s, kth], sem.at[s]).wait()
        gates = gate_ref[...]
        hf = h_ref[...]
        for kth in range(TOP_K):
            hf = hf + gates[:, kth:kth + 1] * _load_row_tiles(buf.at[s, kth], tm)
        o_ref[...] = hf * lax.rsqrt(jnp.mean(hf * hf, axis=-1, keepdims=True) + RMS_EPS) * g_ref[...]

    for s in range(2):
        pl.when(slot == s)(functools.partial(step, s))


def _combine(h, gate_pad, ys, dest_flat, norm_final_g):
    n = h.shape[0]
    tm = TOKEN_TILE
    n_tiles = n // tm
    return pl.pallas_call(
        functools.partial(_combine_kernel, tm=tm, n_tiles=n_tiles),
        grid=(n_tiles,),
        in_specs=[
            pl.BlockSpec((tm * TOP_K,), lambda i: (i,), memory_space=pltpu.SMEM),
            pl.BlockSpec((tm * TOP_K,), lambda i: (jnp.minimum(i + 1, n_tiles - 1),),
                         memory_space=pltpu.SMEM),
            pl.BlockSpec((tm, D_MODEL), lambda i: (i, 0)),
            pl.BlockSpec((tm, LANES), lambda i: (i, 0)),
            pl.BlockSpec((1, D_MODEL), lambda i: (0, 0)),
            pl.BlockSpec(memory_space=pl.ANY),
        ],
        out_specs=pl.BlockSpec((tm, D_MODEL), lambda i: (i, 0)),
        out_shape=jax.ShapeDtypeStruct((n, D_MODEL), F32),
        scratch_shapes=[pltpu.VMEM((2, TOP_K, tm * ROW_TILE, LANES), F32),
                        pltpu.SemaphoreType.DMA((2,))],
        compiler_params=_params(("arbitrary",)),
        name="moe_combine_norm",
    )(dest_flat, dest_flat, h, gate_pad, norm_final_g.reshape(1, D_MODEL), ys)


def _routing(top_idx, rank, counts, bm):
    n = top_idx.shape[0]
    padded = (counts + bm - 1) // bm * bm
    pend = jnp.cumsum(padded)
    pstart = pend - padded
    dest = pstart[top_idx] + rank
    n_blocks = (n * TOP_K + N_EXPERTS * (bm - 1) + bm - 1) // bm
    blk_first_row = jnp.arange(n_blocks, dtype=pend.dtype) * bm
    blk_expert = jnp.minimum(
        jnp.sum((pend[None, :] <= blk_first_row[:, None]).astype(I32), axis=1),
        N_EXPERTS - 1).astype(I32)
    n_valid = (pend[-1:] // bm).astype(I32)
    pad_start = (pstart + counts).astype(I32)
    return dest.reshape(n * TOP_K).astype(I32), blk_expert, n_valid, pad_start, n_blocks


def _layer(xf, b, s, p):
    qkv, rest = _in_proj(xf, p["norm_mix_g"], p["w_in"])
    slopes = jnp.exp2(-8.0 / ATTN_HEADS * jnp.arange(1, ATTN_HEADS + 1, dtype=F32))
    attn = _attention(qkv.reshape(b, s, 3 * ATTN_WIDTH), slopes)
    rw = _rwkv_mix(
        rest.reshape(b, s, SHIFT_WIDTH), p["shift_mu"], p["decay_w0"], p["w_decay_up"],
        p["iclr_a0"], p["w_iclr_up"], p["w_gate_lr_up"], p["k_k"], p["k_a"],
        p["r_k"].reshape(-1), p["gn_w"], p["gn_b"])
    flat = lambda t: t.reshape(b * s, -1)
    h, xn, idx_pad, gate_pad, rank_pad, counts = _mix_out(
        flat(attn), flat(rw), xf, p["w_out"], p["norm_ffn_g"], p["w_router"], p["b_router"])
    bm = EXPERT_BLOCK
    dest, blk_expert, n_valid, pad_start, n_blocks = _routing(
        idx_pad[:, :TOP_K], rank_pad[:, :TOP_K], counts[0, :N_EXPERTS].astype(I32), bm)
    xs = _dispatch(xn, dest, pad_start, n_valid, n_blocks, bm)
    ys = _experts(xs, blk_expert, n_valid, p["w_gate_up"], p["b_gate_up"], p["w_down"],
                  p["b_down"], n_blocks, bm)
    return h, gate_pad, ys, dest


def kernel(x, norm_mix_g, w_in, shift_mu, decay_w0, w_decay_up, iclr_a0, w_iclr_up, w_gate_lr_up,
           k_k, k_a, r_k, gn_w, gn_b, w_out, norm_ffn_g, w_router, b_router, w_gate_up,
           b_gate_up, w_down, b_down, norm_final_g):
    b, s, d = x.shape
    assert d == D_MODEL and w_in.shape[0] == 1, "single-layer block"
    names = ("norm_mix_g", "w_in", "shift_mu", "decay_w0", "w_decay_up", "iclr_a0", "w_iclr_up",
             "w_gate_lr_up", "k_k", "k_a", "r_k", "gn_w", "gn_b", "w_out", "norm_ffn_g",
             "w_router", "b_router", "w_gate_up", "b_gate_up", "w_down", "b_down")
    vals = (norm_mix_g, w_in, shift_mu, decay_w0, w_decay_up, iclr_a0, w_iclr_up, w_gate_lr_up,
            k_k, k_a, r_k, gn_w, gn_b, w_out, norm_ffn_g, w_router, b_router, w_gate_up,
            b_gate_up, w_down, b_down)
    p = {nm: v[0] for nm, v in zip(names, vals)}
    h, gate_pad, ys, dest = _layer(x.reshape(b * s, d), b, s, p)
    out = _combine(h, gate_pad, ys, dest, norm_final_g)
    return out.reshape(b, s, d)
```

```python
import functools
import math

import jax
import jax.numpy as jnp
from jax import lax
from jax.experimental import pallas as pl
from jax.experimental.pallas import tpu as pltpu

F32 = jnp.float32
BF16 = jnp.bfloat16
I32 = jnp.int32

D_MODEL = 1024
HEAD_DIM = 64
ATTN_WIDTH = 512
ATTN_HEADS = ATTN_WIDTH // HEAD_DIM
RWKV_WIDTH = 512
RWKV_HEADS = RWKV_WIDTH // HEAD_DIM
DILATED_PATTERNS = ((128, 1), (512, 4), (2048, 16))
ATTN_BLOCK = 128
DECAY_LORA = 64
ICLR_LORA = 64
GATE_LORA = 128
SHIFT_WIDTH = 3 * RWKV_WIDTH + DECAY_LORA + ICLR_LORA + GATE_LORA
N_EXPERTS = 32
TOP_K = 4
D_FF = D_MODEL
SWIGLU_LIMIT = 7.0
SWIGLU_ALPHA = 1.702
RMS_EPS = 1e-5
GN_EPS = 64e-5

LANES = 128
SUBLANES = 8
MXU_DEPTH = 256
ROW_TILE = D_MODEL // LANES
assert ROW_TILE == SUBLANES, "a D_MODEL-wide f32 row must fill exactly one (8,128) tile"
CHUNK = 64
MASK_VALUE = -1e30
LOG2_E = 1.4426950408889634
ATTN_GROUP = 4
VMEM_LIMIT = 56 * 1024 * 1024

TOKEN_TILE = 512
RWKV_TILE = 512
EXPERT_BLOCK = 512


def _params(semantics, vmem=VMEM_LIMIT):
    return pltpu.CompilerParams(dimension_semantics=semantics, vmem_limit_bytes=vmem)


def _dot(a, b):
    return jnp.dot(a.astype(BF16), b.astype(BF16), preferred_element_type=F32)


def _dot_nt(a, b):
    return lax.dot_general(a.astype(BF16), b.astype(BF16), (((1,), (1,)), ((), ())),
                           preferred_element_type=F32)


def _dot_tn(a, b):
    return lax.dot_general(a.astype(BF16), b.astype(BF16), (((0,), (0,)), ((), ())),
                           preferred_element_type=F32)


def _split3(x):
    h = x.astype(BF16)
    r1 = x - h.astype(F32)
    m = r1.astype(BF16)
    l = (r1 - m.astype(F32)).astype(BF16)
    return h, m, l


def _dot_exact_lhs(a_bf16, x):
    h, m, l = _split3(x)
    d = lambda y: jnp.dot(a_bf16, y, preferred_element_type=F32)
    return d(h) + d(m) + d(l)


def _dot_exact_rhs(x, b_bf16):
    h, m, l = _split3(x)
    d = lambda y: jnp.dot(y, b_bf16, preferred_element_type=F32)
    return d(h) + d(m) + d(l)


def _dot_hi(a, b):
    ah = a.astype(BF16)
    al = (a - ah.astype(F32)).astype(BF16)
    bh = b.astype(BF16)
    bl = (b - bh.astype(F32)).astype(BF16)
    d = lambda x, y: jnp.dot(x, y, preferred_element_type=F32)
    return d(ah, bh) + d(ah, bl) + d(al, bh)


def _store_row_tiles(ref, x):
    rows = x.shape[0]
    for s in range(ROW_TILE):
        ref[pl.ds(s, rows, stride=ROW_TILE), :] = x[:, s * LANES:(s + 1) * LANES]


def _load_row_tiles(ref, rows):
    return jnp.concatenate(
        [ref[pl.ds(s, rows, stride=ROW_TILE), :] for s in range(ROW_TILE)], axis=1)


def _sigmoid(x):
    return 1.0 / (1.0 + jnp.exp(-x))


def _head_ones(width):
    r = lax.broadcasted_iota(I32, (width, width), 0) // HEAD_DIM
    c = lax.broadcasted_iota(I32, (width, width), 1) // HEAD_DIM
    return jnp.where(r == c, 1.0, 0.0).astype(BF16)


def _inproj_kernel(x_ref, g_ref, wq_ref, wr_ref, qkv_ref, rest_ref):
    x = x_ref[...]
    xn = x * lax.rsqrt(jnp.mean(x * x, axis=-1, keepdims=True) + RMS_EPS) * g_ref[...]
    xb = xn.astype(BF16)
    qkv_ref[...] = jnp.dot(xb, wq_ref[...], preferred_element_type=F32)
    rest_ref[...] = jnp.dot(xb, wr_ref[...], preferred_element_type=F32)


def _in_proj(xf, g, w_in):
    n = xf.shape[0]
    tm = TOKEN_TILE
    wq = w_in[:, :3 * ATTN_WIDTH].astype(BF16)
    wr = w_in[:, 3 * ATTN_WIDTH:].astype(BF16)
    return pl.pallas_call(
        _inproj_kernel,
        grid=(n // tm,),
        in_specs=[
            pl.BlockSpec((tm, D_MODEL), lambda i: (i, 0)),
            pl.BlockSpec((1, D_MODEL), lambda i: (0, 0)),
            pl.BlockSpec((D_MODEL, 3 * ATTN_WIDTH), lambda i: (0, 0)),
            pl.BlockSpec((D_MODEL, SHIFT_WIDTH), lambda i: (0, 0)),
        ],
        out_specs=[
            pl.BlockSpec((tm, 3 * ATTN_WIDTH), lambda i: (i, 0)),
            pl.BlockSpec((tm, SHIFT_WIDTH), lambda i: (i, 0)),
        ],
        out_shape=[
            jax.ShapeDtypeStruct((n, 3 * ATTN_WIDTH), F32),
            jax.ShapeDtypeStruct((n, SHIFT_WIDTH), F32),
        ],
        compiler_params=_params(("parallel",)),
        name="in_proj",
    )(xf, g.reshape(1, D_MODEL), wq, wr)


def _attn_kernel(slopes_ref, q_ref, k_ref, v_ref, o_ref, bias_ref, m_acc, d_acc, n_acc, *, seq):
    c = ATTN_BLOCK
    hp = pl.program_id(1)
    lane = lax.broadcasted_iota(I32, (c, LANES), 1)
    lo_half = lane < HEAD_DIM
    qi = lax.broadcasted_iota(I32, (c, 2 * c), 0)
    kj = lax.broadcasted_iota(I32, (c, 2 * c), 1)
    diff = qi + c - kj
    for p, (window, dil) in enumerate(DILATED_PATTERNS):
        steps = window // dil
        valid = (diff >= 0) & (diff <= steps)
        dist = (dil * diff).astype(F32)
        for hh in range(2):
            slope = slopes_ref[hp * 2 + hh] * LOG2_E
            bias = jnp.where(valid, -slope * dist, MASK_VALUE)
            bias_ref[2 * (p * 2 + hh)] = bias
            bias_ref[2 * (p * 2 + hh) + 1] = jnp.where(kj >= c, bias, MASK_VALUE)

    scale = LOG2_E / math.sqrt(HEAD_DIM)
    order = sorted(range(len(DILATED_PATTERNS)), key=lambda p: -DILATED_PATTERNS[p][1])

    def rows(start, dil):
        if dil == 1:
            return pl.ds(pl.multiple_of(start, c), c)
        return pl.ds(start, c, stride=dil)

    def group(i, carry, p, dil, nb):
        blocks = []
        for g in range(ATTN_GROUP):
            j = i * ATTN_GROUP + g
            if nb >= ATTN_GROUP:
                r, n = j // nb, j % nb
                first = (n == 0) if g == 0 else False
            else:
                r, n = j // nb, g % nb
                first = n == 0
            blocks.append((r + dil * c * n, r + dil * c * jnp.maximum(n - 1, 0), first))
        qs, ks, vs = [], [], []
        kc = vc = None
        for g, (cur, prev, first) in enumerate(blocks):
            qs.append(q_ref[0, rows(cur, dil), :] * scale)
            kp, vp = kc, vc
            kc = k_ref[0, rows(cur, dil), :].astype(BF16)
            vc = v_ref[0, rows(cur, dil), :].astype(BF16)
            if first is True:
                ks.append(kc)
                vs.append(vc)
                continue
            if g == 0:
                kp = k_ref[0, rows(prev, dil), :].astype(BF16)
                vp = v_ref[0, rows(prev, dil), :].astype(BF16)
            ks.append(jnp.concatenate([kp, kc], axis=0))
            vs.append(jnp.concatenate([vp, vc], axis=0))
        scores = []
        for (cur, prev, first), qf, kcat in zip(blocks, qs, ks):
            for hh in range(2):
                head_mask = lo_half if hh == 0 else jnp.logical_not(lo_half)
                qh = jnp.where(head_mask, qf, 0.0).astype(BF16)
                s = lax.dot_general(qh, kcat, (((1,), (1,)), ((), ())), preferred_element_type=F32)
                slot = 2 * (p * 2 + hh)
                if first is True:
                    s = s + bias_ref[slot][:, c:]
                elif first is False:
                    s = s + bias_ref[slot]
                else:
                    s = s + bias_ref[slot + first.astype(I32)]
                scores.append(s)
        probs = []
        for s in scores:
            m = jnp.max(s, axis=-1, keepdims=True)
            e = jnp.exp2(s - m)
            probs.append((m, jnp.sum(e, axis=-1, keepdims=True), e.astype(BF16)))
        for bi, (cur, prev, first) in enumerate(blocks):
            (m0, d0, e0), (m1, d1, e1) = probs[2 * bi], probs[2 * bi + 1]
            o0 = jnp.dot(e0, vs[bi], preferred_element_type=F32)
            o1 = jnp.dot(e1, vs[bi], preferred_element_type=F32)
            m_b = jnp.where(lo_half, m0, m1)
            d_b = jnp.where(lo_half, d0, d1)
            n_b = jnp.where(lo_half, o0, o1)
            sl = rows(cur, dil)
            if p == order[0]:
                m_acc[sl, :] = m_b
                d_acc[sl, :] = d_b
                n_acc[sl, :] = n_b
            else:
                m_o = m_acc[sl, :]
                m_n = jnp.maximum(m_o, m_b)
                a_o = jnp.exp2(m_o - m_n)
                a_b = jnp.exp2(m_b - m_n)
                d_n = d_acc[sl, :] * a_o + d_b * a_b
                n_n = n_acc[sl, :] * a_o + n_b * a_b
                if p == order[-1]:
                    o_ref[0, sl, :] = n_n / d_n
                else:
                    d_acc[sl, :] = d_n
                    n_acc[sl, :] = n_n
                    m_acc[sl, :] = m_n
        return carry

    for p in order:
        dil = DILATED_PATTERNS[p][1]
        nb = seq // (c * dil)
        n_groups = (nb * dil) // ATTN_GROUP
        lax.fori_loop(0, n_groups, functools.partial(group, p=p, dil=dil, nb=nb), 0)


def _attention(qkv, slopes):
    b, s, _ = qkv.shape
    for window, dil in DILATED_PATTERNS:
        nb = s // (ATTN_BLOCK * dil)
        assert s % (ATTN_BLOCK * dil) == 0 and window // dil <= ATTN_BLOCK
        assert nb % ATTN_GROUP == 0 or (ATTN_GROUP % nb == 0 and (nb * dil) % ATTN_GROUP == 0)
    n_pairs = ATTN_WIDTH // LANES
    blk = lambda off: pl.BlockSpec((1, s, LANES), lambda bi, hp, sl: (bi, 0, off + hp))
    return pl.pallas_call(
        functools.partial(_attn_kernel, seq=s),
        grid_spec=pltpu.PrefetchScalarGridSpec(
            num_scalar_prefetch=1,
            grid=(b, n_pairs),
            in_specs=[blk(0), blk(n_pairs), blk(2 * n_pairs)],
            out_specs=pl.BlockSpec((1, s, LANES), lambda bi, hp, sl: (bi, 0, hp)),
            scratch_shapes=[
                pltpu.VMEM((4 * len(DILATED_PATTERNS), ATTN_BLOCK, 2 * ATTN_BLOCK), F32),
                pltpu.VMEM((s, LANES), F32),
                pltpu.VMEM((s, LANES), F32),
                pltpu.VMEM((s, LANES), F32),
            ],
        ),
        out_shape=jax.ShapeDtypeStruct((b, s, ATTN_WIDTH), F32),
        compiler_params=_params(("parallel", "parallel")),
        name="dilated_attention",
    )(slopes, qkv, qkv, qkv)


def _seg_sum(x, ones_ref, passes):
    hw = ones_ref.shape[0]
    terms = [x.astype(BF16)]
    if passes == 2:
        terms.append((x - terms[0].astype(F32)).astype(BF16))
    parts = []
    for j in range(x.shape[1] // hw):
        sl = slice(j * hw, (j + 1) * hw)
        parts.append(sum(jnp.dot(t[:, sl], ones_ref[...], preferred_element_type=F32) for t in terms))
    return jnp.concatenate(parts, axis=1)


def _rwkv_kernel(rest_ref, mu_ref, w0_ref, wdu_ref, a0_ref, wau_ref, wg_ref, kk_ref, ka_ref,
                 rk_ref, gnw_ref, gnb_ref, o_ref, st_ref, prev_ref, tri_ref, ones_ref, *, n_chunks):
    l = CHUNK
    pw = LANES
    n_pairs = RWKV_WIDTH // pw
    tc = n_chunks * l
    w = RWKV_WIDTH

    @pl.when(pl.program_id(1) == 0)
    def _():
        st_ref[...] = jnp.zeros_like(st_ref)
        prev_ref[...] = jnp.zeros_like(prev_ref)
        ri = lax.broadcasted_iota(I32, tri_ref.shape, 0)
        ci = lax.broadcasted_iota(I32, tri_ref.shape, 1)
        tri_ref[...] = jnp.where((ri >= ci) & (ri // l == ci // l), 1.0, 0.0).astype(BF16)
        hr = lax.broadcasted_iota(I32, ones_ref.shape, 0) // HEAD_DIM
        hc = lax.broadcasted_iota(I32, ones_ref.shape, 1) // HEAD_DIM
        ones_ref[...] = jnp.where(hr == hc, 1.0, 0.0).astype(BF16)

    z = rest_ref[0]
    zrow = lax.broadcasted_iota(I32, z.shape, 0)
    zprev = jnp.where(zrow == 0, prev_ref[...], pltpu.roll(z, 1, 0))
    prev_ref[...] = z[tc - 1:tc, :]
    xs = z + (zprev - z) * mu_ref[...]
    r = xs[:, :w]
    k_in = xs[:, w:2 * w]
    v = xs[:, 2 * w:3 * w]
    wd = xs[:, 3 * w:3 * w + DECAY_LORA]
    ad = xs[:, 3 * w + DECAY_LORA:3 * w + DECAY_LORA + ICLR_LORA]
    gd = xs[:, 3 * w + DECAY_LORA + ICLR_LORA:]
    zz = w0_ref[...] + _dot_hi(jnp.tanh(wd), wdu_ref[...])
    lw = -math.exp(-0.5) * _sigmoid(zz)
    a = _sigmoid(a0_ref[...] + _dot(ad, wau_ref[...]))
    gate = _dot(_sigmoid(gd), wg_ref[...])
    kk = k_in * kk_ref[...]
    kk = kk * lax.rsqrt(jnp.maximum(_seg_sum(kk * kk, ones_ref, 2), 1e-24))
    k = k_in * (1.0 + (a - 1.0) * ka_ref[...])
    tr = tri_ref.shape[0]
    cum = jnp.concatenate(
        [_dot_exact_lhs(tri_ref[...], lw[j * tr:(j + 1) * tr]) for j in range(tc // tr)], axis=0)
    tot = jnp.concatenate(
        [jnp.broadcast_to(cum[c * l + l - 1:c * l + l, :], (l, w)) for c in range(n_chunks)], axis=0)
    p_in = jnp.exp(cum)
    p_ex = jnp.exp(cum - lw)
    p_inv = jnp.exp(-cum)
    p_end = jnp.exp(tot - cum)
    p_tot = jnp.exp(tot)
    kka = kk * a
    al = -kk * p_ex
    be = kka * p_inv
    kt = k * p_inv
    rt = r * p_in
    bh = kka * p_end
    kh = k * p_end

    row = lax.broadcasted_iota(I32, (l, pw), 0)
    col = lax.broadcasted_iota(I32, (l, pw), 1) % l
    strict = row > col
    incl = row >= col
    eye_ss = jnp.where(row == col, 1.0, 0.0).astype(F32)
    row_bd = lax.broadcasted_iota(I32, (2 * l, pw), 0)
    lane_bd = lax.broadcasted_iota(I32, (2 * l, pw), 1)
    diag_mask = (row_bd // l) == (lane_bd // l)
    eye_bd = row_bd == lane_bd

    def bd(x):
        return jnp.where(diag_mask, jnp.concatenate([x, x], axis=0), 0.0)

    items = [(c, p) for c in range(n_chunks) for p in range(n_pairs)]
    cut = lambda t, c, p: t[c * l:(c + 1) * l, p * pw:(p + 1) * pw]

    zeros_bd = jnp.zeros((2 * l, pw), F32)
    zeros_ss = jnp.zeros((l, pw), F32)
    a_ab, a_rb, a_ak, a_rk, akv = {}, {}, {}, {}, {}
    for it in items:
        lhs = jnp.concatenate([cut(al, *it), cut(rt, *it)], axis=0)
        g = _dot_nt(lhs, jnp.concatenate([bd(cut(be, *it)), bd(cut(kt, *it))], axis=0))
        a_ab[it] = jnp.where(strict, g[:l, :pw], 0.0)
        a_ak[it] = jnp.where(strict, g[:l, pw:], 0.0)
        a_rb[it] = jnp.where(incl, g[l:, :pw], 0.0)
        a_rk[it] = jnp.where(incl, g[l:, pw:], 0.0)
    for it in items:
        akv[it] = _dot(a_ak[it], bd(cut(v, *it)))
    xs = {it: _dot(a_ab[it], bd(a_ab[it])) for it in items}
    ts = {it: eye_ss + a_ab[it] for it in items}
    for _ in range(int(math.log2(l)) - 2):
        for it in items:
            both = _dot(jnp.concatenate([xs[it], ts[it]], axis=0), bd(xs[it]))
            xs[it] = both[:l]
            ts[it] = ts[it] + both[l:]
    for it in items:
        ts[it] = ts[it] + _dot(ts[it], bd(xs[it]))
    w_t, u0 = {}, {}
    for it in items:
        wu = _dot(ts[it], jnp.concatenate([bd(cut(al, *it)), bd(akv[it])], axis=1))
        w_t[it] = wu[:, :pw]
        u0[it] = wu[:, pw:]
    r_hat, y0, m_bd, c_bd = {}, {}, {}, {}
    for it in items:
        v_i = cut(v, *it)
        ry = _dot(jnp.concatenate([a_rb[it], a_rk[it]], axis=1),
                  jnp.concatenate([jnp.concatenate([bd(w_t[it]), bd(u0[it])], axis=1),
                                   jnp.concatenate([zeros_bd, bd(v_i)], axis=1)], axis=0))
        r_hat[it] = cut(rt, *it) + ry[:, :pw]
        y0[it] = ry[:, pw:]
        mc = _dot_tn(jnp.concatenate([cut(bh, *it), cut(kh, *it)], axis=0),
                     jnp.concatenate([jnp.concatenate([w_t[it], u0[it]], axis=1),
                                      jnp.concatenate([zeros_ss, v_i], axis=1)], axis=0))
        decay = jnp.broadcast_to(cut(p_tot, *it)[:1], (2 * l, pw))
        m_bd[it] = jnp.where(diag_mask, mc[:, :pw], 0.0) + jnp.where(eye_bd, decay, 0.0)
        c_bd[it] = jnp.where(diag_mask, mc[:, pw:], 0.0)
    st = [st_ref[p] for p in range(n_pairs)]
    for c in range(n_chunks):
        for p in range(n_pairs):
            it = (c, p)
            both = _dot(jnp.concatenate([r_hat[it], m_bd[it]], axis=0), st[p])
            o_ref[0, c * l:(c + 1) * l, p * pw:(p + 1) * pw] = both[:l] + y0[it]
            st[p] = both[l:] + c_bd[it]
    for p in range(n_pairs):
        st_ref[p] = st[p]

    y = o_ref[0]
    inv_n = 1.0 / HEAD_DIM
    mean = _seg_sum(y, ones_ref, 1) * inv_n
    yc = y - mean
    var = _seg_sum(yc * yc, ones_ref, 1) * inv_n
    yn = yc * lax.rsqrt(var + GN_EPS) * gnw_ref[...] + gnb_ref[...]
    bonus = _seg_sum(r * k * rk_ref[...], ones_ref, 1) * v
    o_ref[0] = (yn + bonus) * gate


def _rwkv_mix(rest, shift_mu, decay_w0, w_decay_up, iclr_a0, w_iclr_up, w_gate_lr_up, k_k, k_a,
              r_k, gn_w, gn_b):
    b, s, _ = rest.shape
    w = RWKV_WIDTH
    tc = RWKV_TILE
    assert s % tc == 0 and tc % MXU_DEPTH == 0 and MXU_DEPTH % CHUNK == 0
    row = lambda x: x.reshape(1, -1)
    full = lambda shape: pl.BlockSpec(shape, lambda bi, i: (0, 0))
    return pl.pallas_call(
        functools.partial(_rwkv_kernel, n_chunks=tc // CHUNK),
        grid=(b, s // tc),
        in_specs=[
            pl.BlockSpec((1, tc, SHIFT_WIDTH), lambda bi, i: (bi, i, 0)),
            full((1, SHIFT_WIDTH)), full((1, w)), full((DECAY_LORA, w)), full((1, w)),
            full((ICLR_LORA, w)), full((GATE_LORA, w)), full((1, w)), full((1, w)),
            full((1, w)), full((1, w)), full((1, w)),
        ],
        out_specs=pl.BlockSpec((1, tc, w), lambda bi, i: (bi, i, 0)),
        out_shape=jax.ShapeDtypeStruct((b, s, w), F32),
        scratch_shapes=[
            pltpu.VMEM((w // LANES, LANES, LANES), F32),
            pltpu.VMEM((1, SHIFT_WIDTH), F32),
            pltpu.VMEM((MXU_DEPTH, MXU_DEPTH), BF16),
            pltpu.VMEM((2 * LANES, 2 * LANES), BF16),
        ],
        compiler_params=_params(("parallel", "arbitrary")),
        name="rwkv7_mix",
    )(rest, row(shift_mu), row(decay_w0), w_decay_up, row(iclr_a0), w_iclr_up, w_gate_lr_up,
      row(k_k), row(k_a), row(r_k), row(gn_w), row(gn_b))


def _mix_kernel(attn_ref, rw_ref, x_ref, wo_ref, nf_ref, wr_ref, br_ref,
                h_out, xn_out, idx_out, gate_out, rank_out, cnt_out, cnt_ref):
    @pl.when(pl.program_id(0) == 0)
    def _():
        cnt_ref[...] = jnp.zeros_like(cnt_ref)

    mixed = (jnp.dot(attn_ref[...].astype(BF16), wo_ref[:ATTN_WIDTH, :], preferred_element_type=F32)
             + jnp.dot(rw_ref[...].astype(BF16), wo_ref[ATTN_WIDTH:, :], preferred_element_type=F32))
    h = x_ref[...] + mixed
    h_out[...] = h
    xn = h * lax.rsqrt(jnp.mean(h * h, axis=-1, keepdims=True) + RMS_EPS) * nf_ref[...]
    _store_row_tiles(xn_out, xn)
    logits = _dot_hi(xn, wr_ref[...]) + br_ref[...]
    eidx = lax.broadcasted_iota(I32, logits.shape, 1).astype(F32)
    lane = lax.broadcasted_iota(I32, (logits.shape[0], LANES), 1)
    idx_pad = jnp.zeros((logits.shape[0], LANES), F32)
    val_pad = jnp.full((logits.shape[0], LANES), MASK_VALUE, F32)
    cur = logits
    sels = []
    for kth in range(TOP_K):
        m = jnp.max(cur, axis=-1, keepdims=True)
        sel = jnp.min(jnp.where(cur == m, eidx, float(N_EXPERTS)), axis=-1, keepdims=True)
        sels.append(sel)
        idx_pad = jnp.where(lane == kth, sel, idx_pad)
        val_pad = jnp.where(lane == kth, m, val_pad)
        cur = jnp.where(eidx == sel, -jnp.inf, cur)
    top = jnp.max(val_pad, axis=-1, keepdims=True)
    e = jnp.exp(val_pad - top)
    gate_out[...] = e / jnp.sum(e, axis=-1, keepdims=True)
    idx_out[...] = idx_pad.astype(I32)

    tm = logits.shape[0]
    lane_f = lane.astype(F32)
    picked = [lane_f == sel for sel in sels]
    onehot = sum(jnp.where(pk, 1.0, 0.0) for pk in picked)
    tr = lax.broadcasted_iota(I32, (tm, tm), 0)
    tc = lax.broadcasted_iota(I32, (tm, tm), 1)
    earlier = jnp.where(tr > tc, 1.0, 0.0).astype(BF16)
    before = jnp.dot(earlier, onehot.astype(BF16), preferred_element_type=F32) + cnt_ref[...]
    rank_pad = jnp.zeros((tm, LANES), F32)
    for kth, pk in enumerate(picked):
        rank_k = jnp.sum(jnp.where(pk, before, 0.0), axis=-1, keepdims=True)
        rank_pad = jnp.where(lane == kth, rank_k, rank_pad)
    rank_out[...] = rank_pad.astype(I32)
    counts = cnt_ref[...] + jnp.sum(onehot, axis=0, keepdims=True)
    cnt_ref[...] = counts
    cnt_out[...] = counts


def _mix_out(attn, rw, xf, w_out, norm_ffn_g, w_router, b_router):
    n = xf.shape[0]
    tm = TOKEN_TILE
    row = lambda x: x.reshape(1, -1)
    half = pl.BlockSpec((tm, RWKV_WIDTH), lambda i: (i, 0))
    wide = pl.BlockSpec((tm, D_MODEL), lambda i: (i, 0))
    pad = pl.BlockSpec((tm, LANES), lambda i: (i, 0))
    full = lambda shape: pl.BlockSpec(shape, lambda i: (0, 0))
    return pl.pallas_call(
        _mix_kernel,
        grid=(n // tm,),
        in_specs=[half, half, wide, full((D_MODEL, D_MODEL)), full((1, D_MODEL)),
                  full((D_MODEL, N_EXPERTS)), full((1, N_EXPERTS))],
        out_specs=[wide, pl.BlockSpec((tm * ROW_TILE, LANES), lambda i: (i, 0)), pad, pad, pad,
                   full((1, LANES))],
        out_shape=[jax.ShapeDtypeStruct((n, D_MODEL), F32),
                   jax.ShapeDtypeStruct((n * ROW_TILE, LANES), F32),
                   jax.ShapeDtypeStruct((n, LANES), I32), jax.ShapeDtypeStruct((n, LANES), F32),
                   jax.ShapeDtypeStruct((n, LANES), I32), jax.ShapeDtypeStruct((1, LANES), F32)],
        scratch_shapes=[pltpu.VMEM((1, LANES), F32)],
        compiler_params=_params(("arbitrary",)),
        name="mix_out_router",
    )(attn, rw, xf, w_out.astype(BF16), row(norm_ffn_g), w_router, row(b_router))


def _dispatch_kernel(pad_ref, nvalid_ref, dest_ref, x_ref, out_ref, zbuf, sem, zsem,
                     *, tm, bm, n_blocks):
    tile = lambda row, n_rows: pl.ds(pl.multiple_of(row * ROW_TILE, ROW_TILE), n_rows * ROW_TILE)

    @pl.when(pl.program_id(0) == 0)
    def _():
        zbuf[...] = jnp.zeros_like(zbuf)

        def fill(start):
            cp = pltpu.make_async_copy(zbuf, out_ref.at[tile(start, bm), :], zsem)
            cp.start()
            cp.wait()

        def pad_fill(e, carry):
            fill(pad_ref[e])
            return carry

        def tail_fill(blk, carry):
            fill(blk * bm)
            return carry

        lax.fori_loop(0, N_EXPERTS, pad_fill, 0)
        lax.fori_loop(nvalid_ref[0], n_blocks + 1, tail_fill, 0)

    def issue(j, carry):
        for kth in range(TOP_K):
            pltpu.make_async_copy(
                x_ref.at[tile(j, 1), :],
                out_ref.at[tile(dest_ref[j * TOP_K + kth], 1), :],
                sem).start(priority=kth % 2)
        return carry

    lax.fori_loop(0, tm, issue, 0, unroll=8)
    for kth in range(TOP_K):
        pltpu.make_async_copy(x_ref, out_ref.at[tile(0, tm), :], sem).wait()


def _dispatch(xn_tiles, dest_flat, pad_start, n_valid, n_blocks, bm):
    n = xn_tiles.shape[0] // ROW_TILE
    tm = TOKEN_TILE
    return pl.pallas_call(
        functools.partial(_dispatch_kernel, tm=tm, bm=bm, n_blocks=n_blocks),
        grid_spec=pltpu.PrefetchScalarGridSpec(
            num_scalar_prefetch=2,
            grid=(n // tm,),
            in_specs=[
                pl.BlockSpec((tm * TOP_K,), lambda i, ps, nv: (i,), memory_space=pltpu.SMEM),
                pl.BlockSpec((tm * ROW_TILE, LANES), lambda i, ps, nv: (i, 0)),
            ],
            out_specs=pl.BlockSpec(memory_space=pl.ANY),
            scratch_shapes=[pltpu.VMEM((bm * ROW_TILE, LANES), F32), pltpu.SemaphoreType.DMA,
                            pltpu.SemaphoreType.DMA],
        ),
        out_shape=jax.ShapeDtypeStruct(((n_blocks + 1) * bm * ROW_TILE, LANES), F32),
        compiler_params=_params(("arbitrary",)),
        name="moe_dispatch",
    )(pad_start, n_valid, dest_flat, xn_tiles)


def _expert_kernel(be_ref, nvalid_ref, x_ref, wgu_ref, bgu_ref, wd_ref, bd_ref, o_ref,
                   wgu_bf, wd_bf):
    i = pl.program_id(0)
    cw = 256

    @pl.when(jnp.logical_or(i == 0, be_ref[i] != be_ref[jnp.maximum(i - 1, 0)]))
    def _():
        for j in range(2 * D_FF // cw):
            wgu_bf[:, j * cw:(j + 1) * cw] = wgu_ref[0, :, j * cw:(j + 1) * cw].astype(BF16)
        for j in range(D_MODEL // cw):
            wd_bf[:, j * cw:(j + 1) * cw] = wd_ref[0, :, j * cw:(j + 1) * cw].astype(BF16)

    @pl.when(i < nvalid_ref[0])
    def _():
        bm = x_ref.shape[0] // ROW_TILE
        x = _load_row_tiles(x_ref, bm).astype(BF16)
        acts = []
        gus = []
        for j in range(D_FF // cw):
            g = jnp.dot(x, wgu_bf[:, j * cw:(j + 1) * cw], preferred_element_type=F32)
            u = jnp.dot(x, wgu_bf[:, D_FF + j * cw:D_FF + (j + 1) * cw],
                        preferred_element_type=F32)
            gus.append((g + bgu_ref[0, :, j * cw:(j + 1) * cw],
                        u + bgu_ref[0, :, D_FF + j * cw:D_FF + (j + 1) * cw]))
        for g, u in gus:
            g = jnp.minimum(g, SWIGLU_LIMIT)
            u = jnp.clip(u, -SWIGLU_LIMIT, SWIGLU_LIMIT)
            acts.append(((u + 1.0) * (g * _sigmoid(SWIGLU_ALPHA * g))).astype(BF16))
        out = jnp.dot(jnp.concatenate(acts, axis=1), wd_bf[...], preferred_element_type=F32)
        _store_row_tiles(o_ref, out + bd_ref[0])

    @pl.when(i >= nvalid_ref[0])
    def _():
        o_ref[...] = jnp.zeros_like(o_ref)


def _experts(xs, blk_expert, n_valid, w_gate_up, b_gate_up, w_down, b_down, n_blocks, bm):
    n_rows = n_blocks * bm
    return pl.pallas_call(
        _expert_kernel,
        grid_spec=pltpu.PrefetchScalarGridSpec(
            num_scalar_prefetch=2,
            grid=(n_blocks,),
            in_specs=[
                pl.BlockSpec((bm * ROW_TILE, LANES), lambda i, be, nv: (i, 0)),
                pl.BlockSpec((1, D_MODEL, 2 * D_FF), lambda i, be, nv: (be[i], 0, 0)),
                pl.BlockSpec((1, 1, 2 * D_FF), lambda i, be, nv: (be[i], 0, 0)),
                pl.BlockSpec((1, D_FF, D_MODEL), lambda i, be, nv: (be[i], 0, 0)),
                pl.BlockSpec((1, 1, D_MODEL), lambda i, be, nv: (be[i], 0, 0)),
            ],
            out_specs=pl.BlockSpec((bm * ROW_TILE, LANES), lambda i, be, nv: (i, 0)),
            scratch_shapes=[pltpu.VMEM((D_MODEL, 2 * D_FF), BF16), pltpu.VMEM((D_FF, D_MODEL), BF16)],
        ),
        out_shape=jax.ShapeDtypeStruct((n_rows * ROW_TILE, LANES), F32),
        compiler_params=_params(("arbitrary",)),
        name="moe_experts",
    )(blk_expert, n_valid, xs, w_gate_up, b_gate_up.reshape(N_EXPERTS, 1, 2 * D_FF),
      w_down, b_down.reshape(N_EXPERTS, 1, D_MODEL))


def _combine_kernel(dest_ref, dest_next_ref, h_ref, gate_ref, g_ref, ys_ref, o_ref, buf, sem,
                    *, tm, n_tiles):
    i = pl.program_id(0)
    slot = i % 2
    tile = lambda row, n_rows: pl.ds(pl.multiple_of(row * ROW_TILE, ROW_TILE), n_rows * ROW_TILE)

    def issue(dref, s):
        def body(j, carry):
            for kth in range(TOP_K):
                pltpu.make_async_copy(
                    ys_ref.at[tile(dref[j * TOP_K + kth], 1), :],
                    buf.at[s, kth, tile(j, 1), :],
                    sem.at[s]).start(priority=kth % 2)
            return carry

        lax.fori_loop(0, tm, body, 0, unroll=8)

    @pl.when(i == 0)
    def _():
        issue(dest_ref, 0)

    def step(s):
        @pl.when(i + 1 < n_tiles)
        def _():
            issue(dest_next_ref, 1 - s)

        for kth in range(TOP_K):
            pltpu.make_async_copy(ys_ref.at[tile(0, tm), :], buf.at[s, kth], sem.at[s]).wait()
        gates = gate_ref[...]
        hf = h_ref[...]
        for kth in range(TOP_K):
            hf = hf + gates[:, kth:kth + 1] * _load_row_tiles(buf.at[s, kth], tm)
        o_ref[...] = hf * lax.rsqrt(jnp.mean(hf * hf, axis=-1, keepdims=True) + RMS_EPS) * g_ref[...]

    for s in range(2):
        pl.when(slot == s)(functools.partial(step, s))


def _combine(h, gate_pad, ys, dest_flat, norm_final_g):
    n = h.shape[0]
    tm = TOKEN_TILE
    n_tiles = n // tm
    return pl.pallas_call(
        functools.partial(_combine_kernel, tm=tm, n_tiles=n_tiles),
        grid=(n_tiles,),
        in_specs=[
            pl.BlockSpec((tm * TOP_K,), lambda i: (i,), memory_space=pltpu.SMEM),
            pl.BlockSpec((tm * TOP_K,), lambda i: (jnp.minimum(i + 1, n_tiles - 1),),
                         memory_space=pltpu.SMEM),
            pl.BlockSpec((tm, D_MODEL), lambda i: (i, 0)),
            pl.BlockSpec((tm, LANES), lambda i: (i, 0)),
            pl.BlockSpec((1, D_MODEL), lambda i: (0, 0)),
            pl.BlockSpec(memory_space=pl.ANY),
        ],
        out_specs=pl.BlockSpec((tm, D_MODEL), lambda i: (i, 0)),
        out_shape=jax.ShapeDtypeStruct((n, D_MODEL), F32),
        scratch_shapes=[pltpu.VMEM((2, TOP_K, tm * ROW_TILE, LANES), F32),
                        pltpu.SemaphoreType.DMA((2,))],
        compiler_params=_params(("arbitrary",)),
        name="moe_combine_norm",
    )(dest_flat, dest_flat, h, gate_pad, norm_final_g.reshape(1, D_MODEL), ys)


def _routing(top_idx, rank, counts, bm):
    n = top_idx.shape[0]
    padded = (counts + bm - 1) // bm * bm
    pend = jnp.cumsum(padded)
    pstart = pend - padded
    dest = pstart[top_idx] + rank
    n_blocks = (n * TOP_K + N_EXPERTS * (bm - 1) + bm - 1) // bm
    blk_first_row = jnp.arange(n_blocks, dtype=pend.dtype) * bm
    blk_expert = jnp.minimum(
        jnp.sum((pend[None, :] <= blk_first_row[:, None]).astype(I32), axis=1),
        N_EXPERTS - 1).astype(I32)
    n_valid = (pend[-1:] // bm).astype(I32)
    pad_start = (pstart + counts).astype(I32)
    return dest.reshape(n * TOP_K).astype(I32), blk_expert, n_valid, pad_start, n_blocks


def _layer(xf, b, s, p):
    qkv, rest = _in_proj(xf, p["norm_mix_g"], p["w_in"])
    slopes = jnp.exp2(-8.0 / ATTN_HEADS * jnp.arange(1, ATTN_HEADS + 1, dtype=F32))
    attn = _attention(qkv.reshape(b, s, 3 * ATTN_WIDTH), slopes)
    rw = _rwkv_mix(
        rest.reshape(b, s, SHIFT_WIDTH), p["shift_mu"], p["decay_w0"], p["w_decay_up"],
        p["iclr_a0"], p["w_iclr_up"], p["w_gate_lr_up"], p["k_k"], p["k_a"],
        p["r_k"].reshape(-1), p["gn_w"], p["gn_b"])
    flat = lambda t: t.reshape(b * s, -1)
    h, xn, idx_pad, gate_pad, rank_pad, counts = _mix_out(
        flat(attn), flat(rw), xf, p["w_out"], p["norm_ffn_g"], p["w_router"], p["b_router"])
    bm = EXPERT_BLOCK
    dest, blk_expert, n_valid, pad_start, n_blocks = _routing(
        idx_pad[:, :TOP_K], rank_pad[:, :TOP_K], counts[0, :N_EXPERTS].astype(I32), bm)
    xs = _dispatch(xn, dest, pad_start, n_valid, n_blocks, bm)
    ys = _experts(xs, blk_expert, n_valid, p["w_gate_up"], p["b_gate_up"], p["w_down"],
                  p["b_down"], n_blocks, bm)
    return h, gate_pad, ys, dest


def kernel(x, norm_mix_g, w_in, shift_mu, decay_w0, w_decay_up, iclr_a0, w_iclr_up, w_gate_lr_up,
           k_k, k_a, r_k, gn_w, gn_b, w_out, norm_ffn_g, w_router, b_router, w_gate_up,
           b_gate_up, w_down, b_down, norm_final_g):
    b, s, d = x.shape
    assert d == D_MODEL and w_in.shape[0] == 1, "single-layer block"
    names = ("norm_mix_g", "w_in", "shift_mu", "decay_w0", "w_decay_up", "iclr_a0", "w_iclr_up",
             "w_gate_lr_up", "k_k", "k_a", "r_k", "gn_w", "gn_b", "w_out", "norm_ffn_g",
             "w_router", "b_router", "w_gate_up", "b_gate_up", "w_down", "b_down")
    vals = (norm_mix_g, w_in, shift_mu, decay_w0, w_decay_up, iclr_a0, w_iclr_up, w_gate_lr_up,
            k_k, k_a, r_k, gn_w, gn_b, w_out, norm_ffn_g, w_router, b_router, w_gate_up,
            b_gate_up, w_down, b_down)
    p = {nm: v[0] for nm, v in zip(names, vals)}
    h, gate_pad, ys, dest = _layer(x.reshape(b * s, d), b, s, p)
    out = _combine(h, gate_pad, ys, dest, norm_final_g)
    return out.reshape(b, s, d)
```

```python
import functools
import math

import jax
import jax.numpy as jnp
from jax import lax
from jax.experimental import pallas as pl
from jax.experimental.pallas import tpu as pltpu

F32 = jnp.float32
BF16 = jnp.bfloat16
I32 = jnp.int32

D_MODEL = 1024
HEAD_DIM = 64
ATTN_WIDTH = 512
ATTN_HEADS = ATTN_WIDTH // HEAD_DIM
RWKV_WIDTH = 512
DILATED_PATTERNS = ((128, 1), (512, 4), (2048, 16))
ATTN_BLOCK = 128
DECAY_LORA = 64
ICLR_LORA = 64
GATE_LORA = 128
SHIFT_WIDTH = 3 * RWKV_WIDTH + DECAY_LORA + ICLR_LORA + GATE_LORA
N_EXPERTS = 32
TOP_K = 4
D_FF = D_MODEL
SWIGLU_LIMIT = 7.0
SWIGLU_ALPHA = 1.702
RMS_EPS = 1e-5
GN_EPS = 64e-5

LANES = 128
SUBLANES = 8
MXU_DEPTH = 256
ROW_TILE = D_MODEL // LANES
assert ROW_TILE == SUBLANES, "a D_MODEL-wide f32 row must fill exactly one (8,128) tile"
CHUNK = 64
MASK_VALUE = -1e30
LOG2_E = 1.4426950408889634
ATTN_GROUP = 4
VMEM_LIMIT = 56 * 1024 * 1024

TOKEN_TILE = 512
RWKV_TILE = 512
EXPERT_BLOCK = 512


def _params(semantics, vmem=VMEM_LIMIT):
    return pltpu.CompilerParams(dimension_semantics=semantics, vmem_limit_bytes=vmem)


def _dot(a, b):
    return jnp.dot(a.astype(BF16), b.astype(BF16), preferred_element_type=F32)


def _dot_nt(a, b):
    return lax.dot_general(a.astype(BF16), b.astype(BF16), (((1,), (1,)), ((), ())),
                           preferred_element_type=F32)


def _dot_tn(a, b):
    return lax.dot_general(a.astype(BF16), b.astype(BF16), (((0,), (0,)), ((), ())),
                           preferred_element_type=F32)


def _split3(x):
    h = x.astype(BF16)
    r1 = x - h.astype(F32)
    m = r1.astype(BF16)
    l = (r1 - m.astype(F32)).astype(BF16)
    return h, m, l


def _dot_exact_lhs(a_bf16, x):
    h, m, l = _split3(x)
    d = lambda y: jnp.dot(a_bf16, y, preferred_element_type=F32)
    return d(h) + d(m) + d(l)


def _dot_hi(a, b):
    ah = a.astype(BF16)
    al = (a - ah.astype(F32)).astype(BF16)
    bh = b.astype(BF16)
    bl = (b - bh.astype(F32)).astype(BF16)
    d = lambda x, y: jnp.dot(x, y, preferred_element_type=F32)
    return d(ah, bh) + d(ah, bl) + d(al, bh)


def _store_row_tiles(ref, x):
    rows = x.shape[0]
    for s in range(ROW_TILE):
        ref[pl.ds(s, rows, stride=ROW_TILE), :] = x[:, s * LANES:(s + 1) * LANES]


def _load_row_tiles(ref, rows):
    return jnp.concatenate(
        [ref[pl.ds(s, rows, stride=ROW_TILE), :] for s in range(ROW_TILE)], axis=1)


def _sigmoid(x):
    return 1.0 / (1.0 + jnp.exp(-x))


def _inproj_kernel(x_ref, g_ref, wq_ref, wr_ref, qkv_ref, rest_ref):
    x = x_ref[...]
    xn = x * lax.rsqrt(jnp.mean(x * x, axis=-1, keepdims=True) + RMS_EPS) * g_ref[...]
    xb = xn.astype(BF16)
    qkv_ref[...] = jnp.dot(xb, wq_ref[...], preferred_element_type=F32)
    rest_ref[...] = jnp.dot(xb, wr_ref[...], preferred_element_type=F32)


def _in_proj(xf, g, w_in):
    n = xf.shape[0]
    tm = TOKEN_TILE
    wq = w_in[:, :3 * ATTN_WIDTH].astype(BF16)
    wr = w_in[:, 3 * ATTN_WIDTH:].astype(BF16)
    return pl.pallas_call(
        _inproj_kernel,
        grid=(n // tm,),
        in_specs=[
            pl.BlockSpec((tm, D_MODEL), lambda i: (i, 0)),
            pl.BlockSpec((1, D_MODEL), lambda i: (0, 0)),
            pl.BlockSpec((D_MODEL, 3 * ATTN_WIDTH), lambda i: (0, 0)),
            pl.BlockSpec((D_MODEL, SHIFT_WIDTH), lambda i: (0, 0)),
        ],
        out_specs=[
            pl.BlockSpec((tm, 3 * ATTN_WIDTH), lambda i: (i, 0)),
            pl.BlockSpec((tm, SHIFT_WIDTH), lambda i: (i, 0)),
        ],
        out_shape=[
            jax.ShapeDtypeStruct((n, 3 * ATTN_WIDTH), F32),
            jax.ShapeDtypeStruct((n, SHIFT_WIDTH), F32),
        ],
        compiler_params=_params(("parallel",)),
        name="in_proj",
    )(xf, g.reshape(1, D_MODEL), wq, wr)


def _attn_kernel(slopes_ref, q_ref, k_ref, v_ref, o_ref, bias_ref, m_acc, d_acc, n_acc, *, seq):
    c = ATTN_BLOCK
    hp = pl.program_id(1)
    lane = lax.broadcasted_iota(I32, (c, LANES), 1)
    lo_half = lane < HEAD_DIM
    qi = lax.broadcasted_iota(I32, (c, 2 * c), 0)
    kj = lax.broadcasted_iota(I32, (c, 2 * c), 1)
    diff = qi + c - kj
    for p, (window, dil) in enumerate(DILATED_PATTERNS):
        steps = window // dil
        valid = (diff >= 0) & (diff <= steps)
        dist = (dil * diff).astype(F32)
        for hh in range(2):
            slope = slopes_ref[hp * 2 + hh] * LOG2_E
            bias = jnp.where(valid, -slope * dist, MASK_VALUE)
            bias_ref[2 * (p * 2 + hh)] = bias
            bias_ref[2 * (p * 2 + hh) + 1] = jnp.where(kj >= c, bias, MASK_VALUE)

    scale = LOG2_E / math.sqrt(HEAD_DIM)
    order = sorted(range(len(DILATED_PATTERNS)), key=lambda p: -DILATED_PATTERNS[p][1])

    def rows(start, dil):
        if dil == 1:
            return pl.ds(pl.multiple_of(start, c), c)
        return pl.ds(start, c, stride=dil)

    def group(i, carry, p, dil, nb):
        blocks = []
        for g in range(ATTN_GROUP):
            j = i * ATTN_GROUP + g
            if nb >= ATTN_GROUP:
                r, n = j // nb, j % nb
                first = (n == 0) if g == 0 else False
            else:
                r, n = j // nb, g % nb
                first = n == 0
            blocks.append((r + dil * c * n, r + dil * c * jnp.maximum(n - 1, 0), first))
        qs, ks, vs = [], [], []
        kc = vc = None
        for g, (cur, prev, first) in enumerate(blocks):
            qs.append(q_ref[0, rows(cur, dil), :] * scale)
            kp, vp = kc, vc
            kc = k_ref[0, rows(cur, dil), :].astype(BF16)
            vc = v_ref[0, rows(cur, dil), :].astype(BF16)
            if first is True:
                ks.append(kc)
                vs.append(vc)
                continue
            if g == 0:
                kp = k_ref[0, rows(prev, dil), :].astype(BF16)
                vp = v_ref[0, rows(prev, dil), :].astype(BF16)
            ks.append(jnp.concatenate([kp, kc], axis=0))
            vs.append(jnp.concatenate([vp, vc], axis=0))
        scores = []
        for (cur, prev, first), qf, kcat in zip(blocks, qs, ks):
            for hh in range(2):
                head_mask = lo_half if hh == 0 else jnp.logical_not(lo_half)
                qh = jnp.where(head_mask, qf, 0.0).astype(BF16)
                s = lax.dot_general(qh, kcat, (((1,), (1,)), ((), ())), preferred_element_type=F32)
                slot = 2 * (p * 2 + hh)
                if first is True:
                    s = s + bias_ref[slot][:, c:]
                elif first is False:
                    s = s + bias_ref[slot]
                else:
                    s = s + bias_ref[slot + first.astype(I32)]
                scores.append(s)
        probs = []
        for s in scores:
            m = jnp.max(s, axis=-1, keepdims=True)
            e = jnp.exp2(s - m)
            probs.append((m, jnp.sum(e, axis=-1, keepdims=True), e.astype(BF16)))
        for bi, (cur, prev, first) in enumerate(blocks):
            (m0, d0, e0), (m1, d1, e1) = probs[2 * bi], probs[2 * bi + 1]
            o0 = jnp.dot(e0, vs[bi], preferred_element_type=F32)
            o1 = jnp.dot(e1, vs[bi], preferred_element_type=F32)
            m_b = jnp.where(lo_half, m0, m1)
            d_b = jnp.where(lo_half, d0, d1)
            n_b = jnp.where(lo_half, o0, o1)
            sl = rows(cur, dil)
            if p == order[0]:
                m_acc[sl, :] = m_b
                d_acc[sl, :] = d_b
                n_acc[sl, :] = n_b
            else:
                m_o = m_acc[sl, :]
                m_n = jnp.maximum(m_o, m_b)
                a_o = jnp.exp2(m_o - m_n)
                a_b = jnp.exp2(m_b - m_n)
                d_n = d_acc[sl, :] * a_o + d_b * a_b
                n_n = n_acc[sl, :] * a_o + n_b * a_b
                if p == order[-1]:
                    o_ref[0, sl, :] = n_n / d_n
                else:
                    d_acc[sl, :] = d_n
                    n_acc[sl, :] = n_n
                    m_acc[sl, :] = m_n
        return carry

    for p in order:
        dil = DILATED_PATTERNS[p][1]
        nb = seq // (c * dil)
        n_groups = (nb * dil) // ATTN_GROUP
        lax.fori_loop(0, n_groups, functools.partial(group, p=p, dil=dil, nb=nb), 0)


def _attention(qkv, slopes):
    b, s, _ = qkv.shape
    for window, dil in DILATED_PATTERNS:
        nb = s // (ATTN_BLOCK * dil)
        assert s % (ATTN_BLOCK * dil) == 0 and window // dil <= ATTN_BLOCK
        assert nb % ATTN_GROUP == 0 or (ATTN_GROUP % nb == 0 and (nb * dil) % ATTN_GROUP == 0)
    n_pairs = ATTN_WIDTH // LANES
    blk = lambda off: pl.BlockSpec((1, s, LANES), lambda bi, hp, sl: (bi, 0, off + hp))
    return pl.pallas_call(
        functools.partial(_attn_kernel, seq=s),
        grid_spec=pltpu.PrefetchScalarGridSpec(
            num_scalar_prefetch=1,
            grid=(b, n_pairs),
            in_specs=[blk(0), blk(n_pairs), blk(2 * n_pairs)],
            out_specs=pl.BlockSpec((1, s, LANES), lambda bi, hp, sl: (bi, 0, hp)),
            scratch_shapes=[
                pltpu.VMEM((4 * len(DILATED_PATTERNS), ATTN_BLOCK, 2 * ATTN_BLOCK), F32),
                pltpu.VMEM((s, LANES), F32),
                pltpu.VMEM((s, LANES), F32),
                pltpu.VMEM((s, LANES), F32),
            ],
        ),
        out_shape=jax.ShapeDtypeStruct((b, s, ATTN_WIDTH), F32),
        compiler_params=_params(("parallel", "parallel")),
        name="dilated_attention",
    )(slopes, qkv, qkv, qkv)


def _seg_sum(x, ones_ref, passes):
    hw = ones_ref.shape[0]
    terms = [x.astype(BF16)]
    if passes == 2:
        terms.append((x - terms[0].astype(F32)).astype(BF16))
    parts = []
    for j in range(x.shape[1] // hw):
        sl = slice(j * hw, (j + 1) * hw)
        parts.append(sum(jnp.dot(t[:, sl], ones_ref[...], preferred_element_type=F32) for t in terms))
    return jnp.concatenate(parts, axis=1)


def _rwkv_kernel(rest_ref, mu_ref, w0_ref, wdu_ref, a0_ref, wau_ref, wg_ref, kk_ref, ka_ref,
                 rk_ref, gnw_ref, gnb_ref, o_ref, st_ref, prev_ref, tri_ref, ones_ref, *, n_chunks):
    l = CHUNK
    pw = LANES
    n_pairs = RWKV_WIDTH // pw
    tc = n_chunks * l
    w = RWKV_WIDTH

    @pl.when(pl.program_id(1) == 0)
    def _():
        st_ref[...] = jnp.zeros_like(st_ref)
        prev_ref[...] = jnp.zeros_like(prev_ref)
        ri = lax.broadcasted_iota(I32, tri_ref.shape, 0)
        ci = lax.broadcasted_iota(I32, tri_ref.shape, 1)
        tri_ref[...] = jnp.where((ri >= ci) & (ri // l == ci // l), 1.0, 0.0).astype(BF16)
        hr = lax.broadcasted_iota(I32, ones_ref.shape, 0) // HEAD_DIM
        hc = lax.broadcasted_iota(I32, ones_ref.shape, 1) // HEAD_DIM
        ones_ref[...] = jnp.where(hr == hc, 1.0, 0.0).astype(BF16)

    z = rest_ref[0]
    zrow = lax.broadcasted_iota(I32, z.shape, 0)
    zprev = jnp.where(zrow == 0, prev_ref[...], pltpu.roll(z, 1, 0))
    prev_ref[...] = z[tc - 1:tc, :]
    xs = z + (zprev - z) * mu_ref[...]
    r = xs[:, :w]
    k_in = xs[:, w:2 * w]
    v = xs[:, 2 * w:3 * w]
    wd = xs[:, 3 * w:3 * w + DECAY_LORA]
    ad = xs[:, 3 * w + DECAY_LORA:3 * w + DECAY_LORA + ICLR_LORA]
    gd = xs[:, 3 * w + DECAY_LORA + ICLR_LORA:]
    zz = w0_ref[...] + _dot_hi(jnp.tanh(wd), wdu_ref[...])
    lw = -math.exp(-0.5) * _sigmoid(zz)
    a = _sigmoid(a0_ref[...] + _dot(ad, wau_ref[...]))
    gate = _dot(_sigmoid(gd), wg_ref[...])
    kk = k_in * kk_ref[...]
    kk = kk * lax.rsqrt(jnp.maximum(_seg_sum(kk * kk, ones_ref, 2), 1e-24))
    k = k_in * (1.0 + (a - 1.0) * ka_ref[...])
    tr = tri_ref.shape[0]
    cum = jnp.concatenate(
        [_dot_exact_lhs(tri_ref[...], lw[j * tr:(j + 1) * tr]) for j in range(tc // tr)], axis=0)
    tot = jnp.concatenate(
        [jnp.broadcast_to(cum[c * l + l - 1:c * l + l, :], (l, w)) for c in range(n_chunks)], axis=0)
    p_in = jnp.exp(cum)
    p_ex = jnp.exp(cum - lw)
    p_inv = jnp.exp(-cum)
    p_end = jnp.exp(tot - cum)
    p_tot = jnp.exp(tot)
    kka = kk * a
    al = -kk * p_ex
    be = kka * p_inv
    kt = k * p_inv
    rt = r * p_in
    bh = kka * p_end
    kh = k * p_end

    row = lax.broadcasted_iota(I32, (l, pw), 0)
    col = lax.broadcasted_iota(I32, (l, pw), 1) % l
    strict = row > col
    incl = row >= col
    eye_ss = jnp.where(row == col, 1.0, 0.0).astype(F32)
    row_bd = lax.broadcasted_iota(I32, (2 * l, pw), 0)
    lane_bd = lax.broadcasted_iota(I32, (2 * l, pw), 1)
    diag_mask = (row_bd // l) == (lane_bd // l)
    eye_bd = row_bd == lane_bd

    def bd(x):
        return jnp.where(diag_mask, jnp.concatenate([x, x], axis=0), 0.0)

    items = [(c, p) for c in range(n_chunks) for p in range(n_pairs)]
    cut = lambda t, c, p: t[c * l:(c + 1) * l, p * pw:(p + 1) * pw]

    zeros_bd = jnp.zeros((2 * l, pw), F32)
    zeros_ss = jnp.zeros((l, pw), F32)
    a_ab, a_rb, a_ak, a_rk, akv = {}, {}, {}, {}, {}
    for it in items:
        lhs = jnp.concatenate([cut(al, *it), cut(rt, *it)], axis=0)
        g = _dot_nt(lhs, jnp.concatenate([bd(cut(be, *it)), bd(cut(kt, *it))], axis=0))
        a_ab[it] = jnp.where(strict, g[:l, :pw], 0.0)
        a_ak[it] = jnp.where(strict, g[:l, pw:], 0.0)
        a_rb[it] = jnp.where(incl, g[l:, :pw], 0.0)
        a_rk[it] = jnp.where(incl, g[l:, pw:], 0.0)
    for it in items:
        akv[it] = _dot(a_ak[it], bd(cut(v, *it)))
    xs = {it: _dot(a_ab[it], bd(a_ab[it])) for it in items}
    ts = {it: eye_ss + a_ab[it] for it in items}
    for _ in range(int(math.log2(l)) - 2):
        for it in items:
            both = _dot(jnp.concatenate([xs[it], ts[it]], axis=0), bd(xs[it]))
            xs[it] = both[:l]
            ts[it] = ts[it] + both[l:]
    for it in items:
        ts[it] = ts[it] + _dot(ts[it], bd(xs[it]))
    w_t, u0 = {}, {}
    for it in items:
        wu = _dot(ts[it], jnp.concatenate([bd(cut(al, *it)), bd(akv[it])], axis=1))
        w_t[it] = wu[:, :pw]
        u0[it] = wu[:, pw:]
    r_hat, y0, m_bd, c_bd = {}, {}, {}, {}
    for it in items:
        v_i = cut(v, *it)
        ry = _dot(jnp.concatenate([a_rb[it], a_rk[it]], axis=1),
                  jnp.concatenate([jnp.concatenate([bd(w_t[it]), bd(u0[it])], axis=1),
                                   jnp.concatenate([zeros_bd, bd(v_i)], axis=1)], axis=0))
        r_hat[it] = cut(rt, *it) + ry[:, :pw]
        y0[it] = ry[:, pw:]
        mc = _dot_tn(jnp.concatenate([cut(bh, *it), cut(kh, *it)], axis=0),
                     jnp.concatenate([jnp.concatenate([w_t[it], u0[it]], axis=1),
                                      jnp.concatenate([zeros_ss, v_i], axis=1)], axis=0))
        decay = jnp.broadcast_to(cut(p_tot, *it)[:1], (2 * l, pw))
        m_bd[it] = jnp.where(diag_mask, mc[:, :pw], 0.0) + jnp.where(eye_bd, decay, 0.0)
        c_bd[it] = jnp.where(diag_mask, mc[:, pw:], 0.0)
    st = [st_ref[p] for p in range(n_pairs)]
    for c in range(n_chunks):
        for p in range(n_pairs):
            it = (c, p)
            both = _dot(jnp.concatenate([r_hat[it], m_bd[it]], axis=0), st[p])
            o_ref[0, c * l:(c + 1) * l, p * pw:(p + 1) * pw] = both[:l] + y0[it]
            st[p] = both[l:] + c_bd[it]
    for p in range(n_pairs):
        st_ref[p] = st[p]

    y = o_ref[0]
    inv_n = 1.0 / HEAD_DIM
    mean = _seg_sum(y, ones_ref, 1) * inv_n
    yc = y - mean
    var = _seg_sum(yc * yc, ones_ref, 1) * inv_n
    yn = yc * lax.rsqrt(var + GN_EPS) * gnw_ref[...] + gnb_ref[...]
    bonus = _seg_sum(r * k * rk_ref[...], ones_ref, 1) * v
    o_ref[0] = (yn + bonus) * gate


def _rwkv_mix(rest, shift_mu, decay_w0, w_decay_up, iclr_a0, w_iclr_up, w_gate_lr_up, k_k, k_a,
              r_k, gn_w, gn_b):
    b, s, _ = rest.shape
    w = RWKV_WIDTH
    tc = RWKV_TILE
    assert s % tc == 0 and tc % MXU_DEPTH == 0 and MXU_DEPTH % CHUNK == 0
    row = lambda x: x.reshape(1, -1)
    full = lambda shape: pl.BlockSpec(shape, lambda bi, i: (0, 0))
    return pl.pallas_call(
        functools.partial(_rwkv_kernel, n_chunks=tc // CHUNK),
        grid=(b, s // tc),
        in_specs=[
            pl.BlockSpec((1, tc, SHIFT_WIDTH), lambda bi, i: (bi, i, 0)),
            full((1, SHIFT_WIDTH)), full((1, w)), full((DECAY_LORA, w)), full((1, w)),
            full((ICLR_LORA, w)), full((GATE_LORA, w)), full((1, w)), full((1, w)),
            full((1, w)), full((1, w)), full((1, w)),
        ],
        out_specs=pl.BlockSpec((1, tc, w), lambda bi, i: (bi, i, 0)),
        out_shape=jax.ShapeDtypeStruct((b, s, w), F32),
        scratch_shapes=[
            pltpu.VMEM((w // LANES, LANES, LANES), F32),
            pltpu.VMEM((1, SHIFT_WIDTH), F32),
            pltpu.VMEM((MXU_DEPTH, MXU_DEPTH), BF16),
            pltpu.VMEM((MXU_DEPTH, MXU_DEPTH), BF16),
        ],
        compiler_params=_params(("parallel", "arbitrary")),
        name="rwkv7_mix",
    )(rest, row(shift_mu), row(decay_w0), w_decay_up, row(iclr_a0), w_iclr_up, w_gate_lr_up,
      row(k_k), row(k_a), row(r_k), row(gn_w), row(gn_b))


def _mix_kernel(attn_ref, rw_ref, x_ref, wo_ref, nf_ref, wr_ref, br_ref,
                h_out, xn_out, idx_out, gate_out, rank_out, cnt_out, cnt_ref):
    @pl.when(pl.program_id(0) == 0)
    def _():
        cnt_ref[...] = jnp.zeros_like(cnt_ref)

    mixed = (jnp.dot(attn_ref[...].astype(BF16), wo_ref[:ATTN_WIDTH, :], preferred_element_type=F32)
             + jnp.dot(rw_ref[...].astype(BF16), wo_ref[ATTN_WIDTH:, :], preferred_element_type=F32))
    h = x_ref[...] + mixed
    h_out[...] = h
    xn = h * lax.rsqrt(jnp.mean(h * h, axis=-1, keepdims=True) + RMS_EPS) * nf_ref[...]
    _store_row_tiles(xn_out, xn)
    logits = _dot_hi(xn, wr_ref[...]) + br_ref[...]
    eidx = lax.broadcasted_iota(I32, logits.shape, 1).astype(F32)
    lane = lax.broadcasted_iota(I32, (logits.shape[0], LANES), 1)
    idx_pad = jnp.zeros((logits.shape[0], LANES), F32)
    val_pad = jnp.full((logits.shape[0], LANES), MASK_VALUE, F32)
    cur = logits
    sels = []
    for kth in range(TOP_K):
        m = jnp.max(cur, axis=-1, keepdims=True)
        sel = jnp.min(jnp.where(cur == m, eidx, float(N_EXPERTS)), axis=-1, keepdims=True)
        sels.append(sel)
        idx_pad = jnp.where(lane == kth, sel, idx_pad)
        val_pad = jnp.where(lane == kth, m, val_pad)
        cur = jnp.where(eidx == sel, -jnp.inf, cur)
    top = jnp.max(val_pad, axis=-1, keepdims=True)
    e = jnp.exp(val_pad - top)
    gate_out[...] = e / jnp.sum(e, axis=-1, keepdims=True)
    idx_out[...] = idx_pad.astype(I32)

    tm = logits.shape[0]
    lane_f = lane.astype(F32)
    picked = [lane_f == sel for sel in sels]
    onehot = sum(jnp.where(pk, 1.0, 0.0) for pk in picked)
    tr = lax.broadcasted_iota(I32, (tm, tm), 0)
    tc = lax.broadcasted_iota(I32, (tm, tm), 1)
    earlier = jnp.where(tr > tc, 1.0, 0.0).astype(BF16)
    before = jnp.dot(earlier, onehot.astype(BF16), preferred_element_type=F32) + cnt_ref[...]
    rank_pad = jnp.zeros((tm, LANES), F32)
    for kth, pk in enumerate(picked):
        rank_k = jnp.sum(jnp.where(pk, before, 0.0), axis=-1, keepdims=True)
        rank_pad = jnp.where(lane == kth, rank_k, rank_pad)
    rank_out[...] = rank_pad.astype(I32)
    counts = cnt_ref[...] + jnp.sum(onehot, axis=0, keepdims=True)
    cnt_ref[...] = counts
    cnt_out[...] = counts


def _mix_out(attn, rw, xf, w_out, norm_ffn_g, w_router, b_router):
    n = xf.shape[0]
    tm = TOKEN_TILE
    row = lambda x: x.reshape(1, -1)
    half = pl.BlockSpec((tm, RWKV_WIDTH), lambda i: (i, 0))
    wide = pl.BlockSpec((tm, D_MODEL), lambda i: (i, 0))
    pad = pl.BlockSpec((tm, LANES), lambda i: (i, 0))
    full = lambda shape: pl.BlockSpec(shape, lambda i: (0, 0))
    return pl.pallas_call(
        _mix_kernel,
        grid=(n // tm,),
        in_specs=[half, half, wide, full((D_MODEL, D_MODEL)), full((1, D_MODEL)),
                  full((D_MODEL, N_EXPERTS)), full((1, N_EXPERTS))],
        out_specs=[wide, pl.BlockSpec((tm * ROW_TILE, LANES), lambda i: (i, 0)), pad, pad, pad,
                   full((1, LANES))],
        out_shape=[jax.ShapeDtypeStruct((n, D_MODEL), F32),
                   jax.ShapeDtypeStruct((n * ROW_TILE, LANES), F32),
                   jax.ShapeDtypeStruct((n, LANES), I32), jax.ShapeDtypeStruct((n, LANES), F32),
                   jax.ShapeDtypeStruct((n, LANES), I32), jax.ShapeDtypeStruct((1, LANES), F32)],
        scratch_shapes=[pltpu.VMEM((1, LANES), F32)],
        compiler_params=_params(("arbitrary",)),
        name="mix_out_router",
    )(attn, rw, xf, w_out.astype(BF16), row(norm_ffn_g), w_router, row(b_router))


def _dispatch_kernel(pad_ref, nvalid_ref, dest_ref, x_ref, out_ref, zbuf, sem, zsem,
                     *, tm, bm, n_blocks):
    tile = lambda row, n_rows: pl.ds(pl.multiple_of(row * ROW_TILE, ROW_TILE), n_rows * ROW_TILE)

    @pl.when(pl.program_id(0) == 0)
    def _():
        zbuf[...] = jnp.zeros_like(zbuf)

        def fill(start):
            cp = pltpu.make_async_copy(zbuf, out_ref.at[tile(start, bm), :], zsem)
            cp.start()
            cp.wait()

        def pad_fill(e, carry):
            fill(pad_ref[e])
            return carry

        def tail_fill(blk, carry):
            fill(blk * bm)
            return carry

        lax.fori_loop(0, N_EXPERTS, pad_fill, 0)
        lax.fori_loop(nvalid_ref[0], n_blocks + 1, tail_fill, 0)

    def issue(j, carry):
        for kth in range(TOP_K):
            pltpu.make_async_copy(
                x_ref.at[tile(j, 1), :],
                out_ref.at[tile(dest_ref[j * TOP_K + kth], 1), :],
                sem).start(priority=kth % 2)
        return carry

    lax.fori_loop(0, tm, issue, 0, unroll=8)
    for kth in range(TOP_K):
        pltpu.make_async_copy(x_ref, out_ref.at[tile(0, tm), :], sem).wait()


def _dispatch(xn_tiles, dest_flat, pad_start, n_valid, n_blocks, bm):
    n = xn_tiles.shape[0] // ROW_TILE
    tm = TOKEN_TILE
    return pl.pallas_call(
        functools.partial(_dispatch_kernel, tm=tm, bm=bm, n_blocks=n_blocks),
        grid_spec=pltpu.PrefetchScalarGridSpec(
            num_scalar_prefetch=2,
            grid=(n // tm,),
            in_specs=[
                pl.BlockSpec((tm * TOP_K,), lambda i, ps, nv: (i,), memory_space=pltpu.SMEM),
                pl.BlockSpec((tm * ROW_TILE, LANES), lambda i, ps, nv: (i, 0)),
            ],
            out_specs=pl.BlockSpec(memory_space=pl.ANY),
            scratch_shapes=[pltpu.VMEM((bm * ROW_TILE, LANES), F32), pltpu.SemaphoreType.DMA,
                            pltpu.SemaphoreType.DMA],
        ),
        out_shape=jax.ShapeDtypeStruct(((n_blocks + 1) * bm * ROW_TILE, LANES), F32),
        compiler_params=_params(("arbitrary",)),
        name="moe_dispatch",
    )(pad_start, n_valid, dest_flat, xn_tiles)


def _expert_kernel(be_ref, nvalid_ref, x_ref, wgu_ref, bgu_ref, wd_ref, bd_ref, o_ref,
                   wgu_bf, wd_bf):
    i = pl.program_id(0)
    cw = MXU_DEPTH

    @pl.when(jnp.logical_or(i == 0, be_ref[i] != be_ref[jnp.maximum(i - 1, 0)]))
    def _():
        for j in range(2 * D_FF // cw):
            wgu_bf[:, j * cw:(j + 1) * cw] = wgu_ref[0, :, j * cw:(j + 1) * cw].astype(BF16)
        for j in range(D_MODEL // cw):
            wd_bf[:, j * cw:(j + 1) * cw] = wd_ref[0, :, j * cw:(j + 1) * cw].astype(BF16)

    @pl.when(i < nvalid_ref[0])
    def _():
        bm = x_ref.shape[0] // ROW_TILE
        x = _load_row_tiles(x_ref, bm).astype(BF16)
        acts = []
        gus = []
        for j in range(D_FF // cw):
            g = jnp.dot(x, wgu_bf[:, j * cw:(j + 1) * cw], preferred_element_type=F32)
            u = jnp.dot(x, wgu_bf[:, D_FF + j * cw:D_FF + (j + 1) * cw],
                        preferred_element_type=F32)
            gus.append((g + bgu_ref[0, :, j * cw:(j + 1) * cw],
                        u + bgu_ref[0, :, D_FF + j * cw:D_FF + (j + 1) * cw]))
        for g, u in gus:
            g = jnp.minimum(g, SWIGLU_LIMIT)
            u = jnp.clip(u, -SWIGLU_LIMIT, SWIGLU_LIMIT)
            acts.append(((u + 1.0) * (g * _sigmoid(SWIGLU_ALPHA * g))).astype(BF16))
        out = jnp.dot(jnp.concatenate(acts, axis=1), wd_bf[...], preferred_element_type=F32)
        _store_row_tiles(o_ref, out + bd_ref[0])

    @pl.when(i >= nvalid_ref[0])
    def _():
        o_ref[...] = jnp.zeros_like(o_ref)


def _experts(xs, blk_expert, n_valid, w_gate_up, b_gate_up, w_down, b_down, n_blocks, bm):
    n_rows = n_blocks * bm
    return pl.pallas_call(
        _expert_kernel,
        grid_spec=pltpu.PrefetchScalarGridSpec(
            num_scalar_prefetch=2,
            grid=(n_blocks,),
            in_specs=[
                pl.BlockSpec((bm * ROW_TILE, LANES), lambda i, be, nv: (i, 0)),
                pl.BlockSpec((1, D_MODEL, 2 * D_FF), lambda i, be, nv: (be[i], 0, 0)),
                pl.BlockSpec((1, 1, 2 * D_FF), lambda i, be, nv: (be[i], 0, 0)),
                pl.BlockSpec((1, D_FF, D_MODEL), lambda i, be, nv: (be[i], 0, 0)),
                pl.BlockSpec((1, 1, D_MODEL), lambda i, be, nv: (be[i], 0, 0)),
            ],
            out_specs=pl.BlockSpec((bm * ROW_TILE, LANES), lambda i, be, nv: (i, 0)),
            scratch_shapes=[pltpu.VMEM((D_MODEL, 2 * D_FF), BF16), pltpu.VMEM((D_FF, D_MODEL), BF16)],
        ),
        out_shape=jax.ShapeDtypeStruct((n_rows * ROW_TILE, LANES), F32),
        compiler_params=_params(("arbitrary",)),
        name="moe_experts",
    )(blk_expert, n_valid, xs, w_gate_up, b_gate_up.reshape(N_EXPERTS, 1, 2 * D_FF),
      w_down, b_down.reshape(N_EXPERTS, 1, D_MODEL))


def _combine_kernel(dest_ref, dest_next_ref, h_ref, gate_ref, g_ref, ys_ref, o_ref, buf, sem,
                    *, tm, n_tiles):
    i = pl.program_id(0)
    slot = i % 2
    tile = lambda row, n_rows: pl.ds(pl.multiple_of(row * ROW_TILE, ROW_TILE), n_rows * ROW_TILE)

    def issue(dref, s):
        def body(j, carry):
            for kth in range(TOP_K):
                pltpu.make_async_copy(
                    ys_ref.at[tile(dref[j * TOP_K + kth], 1), :],
                    buf.at[s, kth, tile(j, 1), :],
                    sem.at[s]).start(priority=kth % 2)
            return carry

        lax.fori_loop(0, tm, body, 0, unroll=8)

    @pl.when(i == 0)
    def _():
        issue(dest_ref, 0)

    def step(s):
        @pl.when(i + 1 < n_tiles)
        def _():
            issue(dest_next_ref, 1 - s)

        for kth in range(TOP_K):
            pltpu.make_async_copy(ys_ref.at[tile(0, tm), :], buf.at[s, kth], sem.at[s]).wait()
        gates = gate_ref[...]
        hf = h_ref[...]
        for kth in range(TOP_K):
            hf = hf + gates[:, kth:kth + 1] * _load_row_tiles(buf.at[s, kth], tm)
        o_ref[...] = hf * lax.rsqrt(jnp.mean(hf * hf, axis=-1, keepdims=True) + RMS_EPS) * g_ref[...]

    for s in range(2):
        pl.when(slot == s)(functools.partial(step, s))


def _combine(h, gate_pad, ys, dest_flat, norm_final_g):
    n = h.shape[0]
    tm = TOKEN_TILE
    n_tiles = n // tm
    return pl.pallas_call(
        functools.partial(_combine_kernel, tm=tm, n_tiles=n_tiles),
        grid=(n_tiles,),
        in_specs=[
            pl.BlockSpec((tm * TOP_K,), lambda i: (i,), memory_space=pltpu.SMEM),
            pl.BlockSpec((tm * TOP_K,), lambda i: (jnp.minimum(i + 1, n_tiles - 1),),
                         memory_space=pltpu.SMEM),
            pl.BlockSpec((tm, D_MODEL), lambda i: (i, 0)),
            pl.BlockSpec((tm, LANES), lambda i: (i, 0)),
            pl.BlockSpec((1, D_MODEL), lambda i: (0, 0)),
            pl.BlockSpec(memory_space=pl.ANY),
        ],
        out_specs=pl.BlockSpec((tm, D_MODEL), lambda i: (i, 0)),
        out_shape=jax.ShapeDtypeStruct((n, D_MODEL), F32),
        scratch_shapes=[pltpu.VMEM((2, TOP_K, tm * ROW_TILE, LANES), F32),
                        pltpu.SemaphoreType.DMA((2,))],
        compiler_params=_params(("arbitrary",)),
        name="moe_combine_norm",
    )(dest_flat, dest_flat, h, gate_pad, norm_final_g.reshape(1, D_MODEL), ys)


def _routing(top_idx, rank, counts, bm):
    n = top_idx.shape[0]
    padded = (counts + bm - 1) // bm * bm
    pend = jnp.cumsum(padded)
    pstart = pend - padded
    dest = pstart[top_idx] + rank
    n_blocks = (n * TOP_K + N_EXPERTS * (bm - 1) + bm - 1) // bm
    blk_first_row = jnp.arange(n_blocks, dtype=pend.dtype) * bm
    blk_expert = jnp.minimum(
        jnp.sum((pend[None, :] <= blk_first_row[:, None]).astype(I32), axis=1),
        N_EXPERTS - 1).astype(I32)
    n_valid = (pend[-1:] // bm).astype(I32)
    pad_start = (pstart + counts).astype(I32)
    return dest.reshape(n * TOP_K).astype(I32), blk_expert, n_valid, pad_start, n_blocks


def _layer(xf, b, s, p):
    qkv, rest = _in_proj(xf, p["norm_mix_g"], p["w_in"])
    slopes = jnp.exp2(-8.0 / ATTN_HEADS * jnp.arange(1, ATTN_HEADS + 1, dtype=F32))
    attn = _attention(qkv.reshape(b, s, 3 * ATTN_WIDTH), slopes)
    rw = _rwkv_mix(
        rest.reshape(b, s, SHIFT_WIDTH), p["shift_mu"], p["decay_w0"], p["w_decay_up"],
        p["iclr_a0"], p["w_iclr_up"], p["w_gate_lr_up"], p["k_k"], p["k_a"],
        p["r_k"].reshape(-1), p["gn_w"], p["gn_b"])
    flat = lambda t: t.reshape(b * s, -1)
    h, xn, idx_pad, gate_pad, rank_pad, counts = _mix_out(
        flat(attn), flat(rw), xf, p["w_out"], p["norm_ffn_g"], p["w_router"], p["b_router"])
    bm = EXPERT_BLOCK
    dest, blk_expert, n_valid, pad_start, n_blocks = _routing(
        idx_pad[:, :TOP_K], rank_pad[:, :TOP_K], counts[0, :N_EXPERTS].astype(I32), bm)
    xs = _dispatch(xn, dest, pad_start, n_valid, n_blocks, bm)
    ys = _experts(xs, blk_expert, n_valid, p["w_gate_up"], p["b_gate_up"], p["w_down"],
                  p["b_down"], n_blocks, bm)
    return h, gate_pad, ys, dest


def kernel(x, norm_mix_g, w_in, shift_mu, decay_w0, w_decay_up, iclr_a0, w_iclr_up, w_gate_lr_up,
           k_k, k_a, r_k, gn_w, gn_b, w_out, norm_ffn_g, w_router, b_router, w_gate_up,
           b_gate_up, w_down, b_down, norm_final_g):
    b, s, d = x.shape
    assert d == D_MODEL and w_in.shape[0] == 1, "single-layer block"
    names = ("norm_mix_g", "w_in", "shift_mu", "decay_w0", "w_decay_up", "iclr_a0", "w_iclr_up",
             "w_gate_lr_up", "k_k", "k_a", "r_k", "gn_w", "gn_b", "w_out", "norm_ffn_g",
             "w_router", "b_router", "w_gate_up", "b_gate_up", "w_down", "b_down")
    vals = (norm_mix_g, w_in, shift_mu, decay_w0, w_decay_up, iclr_a0, w_iclr_up, w_gate_lr_up,
            k_k, k_a, r_k, gn_w, gn_b, w_out, norm_ffn_g, w_router, b_router, w_gate_up,
            b_gate_up, w_down, b_down)
    p = {nm: v[0] for nm, v in zip(names, vals)}
    h, gate_pad, ys, dest = _layer(x.reshape(b * s, d), b, s, p)
    out = _combine(h, gate_pad, ys, dest, norm_final_g)
    return out.reshape(b, s, d)
```

```python
import functools
import math

import jax
import jax.numpy as jnp
from jax import lax
from jax.experimental import pallas as pl
from jax.experimental.pallas import tpu as pltpu

F32 = jnp.float32
BF16 = jnp.bfloat16
I32 = jnp.int32

D_MODEL = 1024
HEAD_DIM = 64
ATTN_WIDTH = 512
ATTN_HEADS = ATTN_WIDTH // HEAD_DIM
RWKV_WIDTH = 512
DILATED_PATTERNS = ((128, 1), (512, 4), (2048, 16))
ATTN_BLOCK = 128
DECAY_LORA = 64
ICLR_LORA = 64
GATE_LORA = 128
SHIFT_WIDTH = 3 * RWKV_WIDTH + DECAY_LORA + ICLR_LORA + GATE_LORA
N_EXPERTS = 32
TOP_K = 4
D_FF = D_MODEL
SWIGLU_LIMIT = 7.0
SWIGLU_ALPHA = 1.702
RMS_EPS = 1e-5
GN_EPS = 64e-5

LANES = 128
SUBLANES = 8
MXU_DEPTH = 256
ROW_TILE = D_MODEL // LANES
assert ROW_TILE == SUBLANES, "a D_MODEL-wide f32 row must fill exactly one (8,128) tile"
CHUNK = 64
MASK_VALUE = -1e30
LOG2_E = 1.4426950408889634
ATTN_GROUP = 4
VMEM_LIMIT = 56 * 1024 * 1024

TOKEN_TILE = 512
RWKV_TILE = 512
EXPERT_BLOCK = 512


def _params(semantics, vmem=VMEM_LIMIT):
    return pltpu.CompilerParams(dimension_semantics=semantics, vmem_limit_bytes=vmem)


def _dot(a, b):
    return jnp.dot(a.astype(BF16), b.astype(BF16), preferred_element_type=F32)


def _dot_nt(a, b):
    return lax.dot_general(a.astype(BF16), b.astype(BF16), (((1,), (1,)), ((), ())),
                           preferred_element_type=F32)


def _dot_tn(a, b):
    return lax.dot_general(a.astype(BF16), b.astype(BF16), (((0,), (0,)), ((), ())),
                           preferred_element_type=F32)


def _split3(x):
    h = x.astype(BF16)
    r1 = x - h.astype(F32)
    m = r1.astype(BF16)
    l = (r1 - m.astype(F32)).astype(BF16)
    return h, m, l


def _dot_exact_lhs(a_bf16, x):
    h, m, l = _split3(x)
    d = lambda y: jnp.dot(a_bf16, y, preferred_element_type=F32)
    return d(h) + d(m) + d(l)


def _dot_hi(a, b):
    ah = a.astype(BF16)
    al = (a - ah.astype(F32)).astype(BF16)
    bh = b.astype(BF16)
    bl = (b - bh.astype(F32)).astype(BF16)
    d = lambda x, y: jnp.dot(x, y, preferred_element_type=F32)
    return d(ah, bh) + d(ah, bl) + d(al, bh)


def _store_row_tiles(ref, x):
    rows = x.shape[0]
    for s in range(ROW_TILE):
        ref[pl.ds(s, rows, stride=ROW_TILE), :] = x[:, s * LANES:(s + 1) * LANES]


def _load_row_tiles(ref, rows):
    return jnp.concatenate(
        [ref[pl.ds(s, rows, stride=ROW_TILE), :] for s in range(ROW_TILE)], axis=1)


def _sigmoid(x):
    return 1.0 / (1.0 + jnp.exp(-x))


def _inproj_kernel(x_ref, g_ref, wq_ref, wr_ref, qkv_ref, rest_ref):
    x = x_ref[...]
    xn = x * lax.rsqrt(jnp.mean(x * x, axis=-1, keepdims=True) + RMS_EPS) * g_ref[...]
    xb = xn.astype(BF16)
    qkv_ref[...] = jnp.dot(xb, wq_ref[...], preferred_element_type=F32)
    rest_ref[...] = jnp.dot(xb, wr_ref[...], preferred_element_type=F32)


def _in_proj(xf, g, w_in):
    n = xf.shape[0]
    tm = TOKEN_TILE
    wq = w_in[:, :3 * ATTN_WIDTH].astype(BF16)
    wr = w_in[:, 3 * ATTN_WIDTH:].astype(BF16)
    return pl.pallas_call(
        _inproj_kernel,
        grid=(n // tm,),
        in_specs=[
            pl.BlockSpec((tm, D_MODEL), lambda i: (i, 0)),
            pl.BlockSpec((1, D_MODEL), lambda i: (0, 0)),
            pl.BlockSpec((D_MODEL, 3 * ATTN_WIDTH), lambda i: (0, 0)),
            pl.BlockSpec((D_MODEL, SHIFT_WIDTH), lambda i: (0, 0)),
        ],
        out_specs=[
            pl.BlockSpec((tm, 3 * ATTN_WIDTH), lambda i: (i, 0)),
            pl.BlockSpec((tm, SHIFT_WIDTH), lambda i: (i, 0)),
        ],
        out_shape=[
            jax.ShapeDtypeStruct((n, 3 * ATTN_WIDTH), F32),
            jax.ShapeDtypeStruct((n, SHIFT_WIDTH), F32),
        ],
        compiler_params=_params(("parallel",)),
        name="in_proj",
    )(xf, g.reshape(1, D_MODEL), wq, wr)


def _attn_kernel(slopes_ref, q_ref, k_ref, v_ref, o_ref, bias_ref, m_acc, d_acc, n_acc, *, seq):
    c = ATTN_BLOCK
    hp = pl.program_id(1)
    lane = lax.broadcasted_iota(I32, (c, LANES), 1)
    lo_half = lane < HEAD_DIM
    qi = lax.broadcasted_iota(I32, (c, 2 * c), 0)
    kj = lax.broadcasted_iota(I32, (c, 2 * c), 1)
    diff = qi + c - kj
    for p, (window, dil) in enumerate(DILATED_PATTERNS):
        steps = window // dil
        valid = (diff >= 0) & (diff <= steps)
        dist = (dil * diff).astype(F32)
        for hh in range(2):
            slope = slopes_ref[hp * 2 + hh] * LOG2_E
            bias = jnp.where(valid, -slope * dist, MASK_VALUE)
            bias_ref[2 * (p * 2 + hh)] = bias
            bias_ref[2 * (p * 2 + hh) + 1] = jnp.where(kj >= c, bias, MASK_VALUE)

    scale = LOG2_E / math.sqrt(HEAD_DIM)
    order = sorted(range(len(DILATED_PATTERNS)), key=lambda p: -DILATED_PATTERNS[p][1])

    def rows(start, dil):
        if dil == 1:
            return pl.ds(pl.multiple_of(start, c), c)
        return pl.ds(start, c, stride=dil)

    def group(i, carry, p, dil, nb):
        blocks = []
        for g in range(ATTN_GROUP):
            j = i * ATTN_GROUP + g
            if nb >= ATTN_GROUP:
                r, n = j // nb, j % nb
                first = (n == 0) if g == 0 else False
            else:
                r, n = j // nb, g % nb
                first = n == 0
            blocks.append((r + dil * c * n, r + dil * c * jnp.maximum(n - 1, 0), first))
        qs, ks, vs = [], [], []
        kc = vc = None
        for g, (cur, prev, first) in enumerate(blocks):
            qs.append(q_ref[0, rows(cur, dil), :] * scale)
            kp, vp = kc, vc
            kc = k_ref[0, rows(cur, dil), :].astype(BF16)
            vc = v_ref[0, rows(cur, dil), :].astype(BF16)
            if first is True:
                ks.append(kc)
                vs.append(vc)
                continue
            if g == 0:
                kp = k_ref[0, rows(prev, dil), :].astype(BF16)
                vp = v_ref[0, rows(prev, dil), :].astype(BF16)
            ks.append(jnp.concatenate([kp, kc], axis=0))
            vs.append(jnp.concatenate([vp, vc], axis=0))
        scores = []
        for (cur, prev, first), qf, kcat in zip(blocks, qs, ks):
            for hh in range(2):
                head_mask = lo_half if hh == 0 else jnp.logical_not(lo_half)
                qh = jnp.where(head_mask, qf, 0.0).astype(BF16)
                s = lax.dot_general(qh, kcat, (((1,), (1,)), ((), ())), preferred_element_type=F32)
                slot = 2 * (p * 2 + hh)
                if first is True:
                    s = s + bias_ref[slot][:, c:]
                elif first is False:
                    s = s + bias_ref[slot]
                else:
                    s = s + bias_ref[slot + first.astype(I32)]
                scores.append(s)
        probs = []
        for s in scores:
            m = jnp.max(s, axis=-1, keepdims=True)
            e = jnp.exp2(s - m)
            probs.append((m, jnp.sum(e, axis=-1, keepdims=True), e.astype(BF16)))
        for bi, (cur, prev, first) in enumerate(blocks):
            (m0, d0, e0), (m1, d1, e1) = probs[2 * bi], probs[2 * bi + 1]
            o0 = jnp.dot(e0, vs[bi], preferred_element_type=F32)
            o1 = jnp.dot(e1, vs[bi], preferred_element_type=F32)
            m_b = jnp.where(lo_half, m0, m1)
            d_b = jnp.where(lo_half, d0, d1)
            n_b = jnp.where(lo_half, o0, o1)
            sl = rows(cur, dil)
            if p == order[0]:
                m_acc[sl, :] = m_b
                d_acc[sl, :] = d_b
                n_acc[sl, :] = n_b
            else:
                m_o = m_acc[sl, :]
                m_n = jnp.maximum(m_o, m_b)
                a_o = jnp.exp2(m_o - m_n)
                a_b = jnp.exp2(m_b - m_n)
                d_n = d_acc[sl, :] * a_o + d_b * a_b
                n_n = n_acc[sl, :] * a_o + n_b * a_b
                if p == order[-1]:
                    o_ref[0, sl, :] = n_n / d_n
                else:
                    d_acc[sl, :] = d_n
                    n_acc[sl, :] = n_n
                    m_acc[sl, :] = m_n
        return carry

    for p in order:
        dil = DILATED_PATTERNS[p][1]
        nb = seq // (c * dil)
        n_groups = (nb * dil) // ATTN_GROUP
        lax.fori_loop(0, n_groups, functools.partial(group, p=p, dil=dil, nb=nb), 0)


def _attention(qkv, slopes):
    b, s, _ = qkv.shape
    for window, dil in DILATED_PATTERNS:
        nb = s // (ATTN_BLOCK * dil)
        assert s % (ATTN_BLOCK * dil) == 0 and window // dil <= ATTN_BLOCK
        assert nb % ATTN_GROUP == 0 or (ATTN_GROUP % nb == 0 and (nb * dil) % ATTN_GROUP == 0)
    n_pairs = ATTN_WIDTH // LANES
    blk = lambda off: pl.BlockSpec((1, s, LANES), lambda bi, hp, sl: (bi, 0, off + hp))
    return pl.pallas_call(
        functools.partial(_attn_kernel, seq=s),
        grid_spec=pltpu.PrefetchScalarGridSpec(
            num_scalar_prefetch=1,
            grid=(b, n_pairs),
            in_specs=[blk(0), blk(n_pairs), blk(2 * n_pairs)],
            out_specs=pl.BlockSpec((1, s, LANES), lambda bi, hp, sl: (bi, 0, hp)),
            scratch_shapes=[
                pltpu.VMEM((4 * len(DILATED_PATTERNS), ATTN_BLOCK, 2 * ATTN_BLOCK), F32),
                pltpu.VMEM((s, LANES), F32),
                pltpu.VMEM((s, LANES), F32),
                pltpu.VMEM((s, LANES), F32),
            ],
        ),
        out_shape=jax.ShapeDtypeStruct((b, s, ATTN_WIDTH), F32),
        compiler_params=_params(("parallel", "parallel")),
        name="dilated_attention",
    )(slopes, qkv, qkv, qkv)


def _seg_sum(x, ones_ref, passes):
    hw = ones_ref.shape[0]
    terms = [x.astype(BF16)]
    if passes == 2:
        terms.append((x - terms[0].astype(F32)).astype(BF16))
    parts = []
    for j in range(x.shape[1] // hw):
        sl = slice(j * hw, (j + 1) * hw)
        parts.append(sum(jnp.dot(t[:, sl], ones_ref[...], preferred_element_type=F32) for t in terms))
    return jnp.concatenate(parts, axis=1)


def _rwkv_kernel(rest_ref, mu_ref, w0_ref, wdu_ref, a0_ref, wau_ref, wg_ref, kk_ref, ka_ref,
                 rk_ref, gnw_ref, gnb_ref, o_ref, st_ref, prev_ref, tri_ref, ones_ref, *, n_chunks):
    l = CHUNK
    pw = LANES
    n_pairs = RWKV_WIDTH // pw
    tc = n_chunks * l
    w = RWKV_WIDTH

    @pl.when(pl.program_id(1) == 0)
    def _():
        st_ref[...] = jnp.zeros_like(st_ref)
        prev_ref[...] = jnp.zeros_like(prev_ref)
        ri = lax.broadcasted_iota(I32, tri_ref.shape, 0)
        ci = lax.broadcasted_iota(I32, tri_ref.shape, 1)
        tri_ref[...] = jnp.where((ri >= ci) & (ri // l == ci // l), 1.0, 0.0).astype(BF16)
        hr = lax.broadcasted_iota(I32, ones_ref.shape, 0) // HEAD_DIM
        hc = lax.broadcasted_iota(I32, ones_ref.shape, 1) // HEAD_DIM
        ones_ref[...] = jnp.where(hr == hc, 1.0, 0.0).astype(BF16)

    z = rest_ref[0]
    zrow = lax.broadcasted_iota(I32, z.shape, 0)
    zprev = jnp.where(zrow == 0, prev_ref[...], pltpu.roll(z, 1, 0))
    prev_ref[...] = z[tc - 1:tc, :]
    xs = z + (zprev - z) * mu_ref[...]
    r = xs[:, :w]
    k_in = xs[:, w:2 * w]
    v = xs[:, 2 * w:3 * w]
    wd = xs[:, 3 * w:3 * w + DECAY_LORA]
    ad = xs[:, 3 * w + DECAY_LORA:3 * w + DECAY_LORA + ICLR_LORA]
    gd = xs[:, 3 * w + DECAY_LORA + ICLR_LORA:]
    zz = w0_ref[...] + _dot_hi(jnp.tanh(wd), wdu_ref[...])
    lw = -math.exp(-0.5) * _sigmoid(zz)
    a = _sigmoid(a0_ref[...] + _dot(ad, wau_ref[...]))
    gate = _dot(_sigmoid(gd), wg_ref[...])
    kk = k_in * kk_ref[...]
    kk = kk * lax.rsqrt(jnp.maximum(_seg_sum(kk * kk, ones_ref, 2), 1e-24))
    k = k_in * (1.0 + (a - 1.0) * ka_ref[...])
    tr = tri_ref.shape[0]
    cum = jnp.concatenate(
        [_dot_exact_lhs(tri_ref[...], lw[j * tr:(j + 1) * tr]) for j in range(tc // tr)], axis=0)
    tot = jnp.concatenate(
        [jnp.broadcast_to(cum[c * l + l - 1:c * l + l, :], (l, w)) for c in range(n_chunks)], axis=0)
    p_in = jnp.exp(cum)
    p_ex = jnp.exp(cum - lw)
    p_inv = jnp.exp(-cum)
    p_end = jnp.exp(tot - cum)
    p_tot = jnp.exp(tot)
    kka = kk * a
    al = -kk * p_ex
    be = kka * p_inv
    kt = k * p_inv
    rt = r * p_in
    bh = kka * p_end
    kh = k * p_end

    row = lax.broadcasted_iota(I32, (l, pw), 0)
    col = lax.broadcasted_iota(I32, (l, pw), 1) % l
    strict = row > col
    incl = row >= col
    eye_ss = jnp.where(row == col, 1.0, 0.0).astype(F32)
    row_bd = lax.broadcasted_iota(I32, (2 * l, pw), 0)
    lane_bd = lax.broadcasted_iota(I32, (2 * l, pw), 1)
    diag_mask = (row_bd // l) == (lane_bd // l)
    eye_bd = row_bd == lane_bd

    def bd(x):
        return jnp.where(diag_mask, jnp.concatenate([x, x], axis=0), 0.0)

    items = [(c, p) for c in range(n_chunks) for p in range(n_pairs)]
    cut = lambda t, c, p: t[c * l:(c + 1) * l, p * pw:(p + 1) * pw]

    zeros_bd = jnp.zeros((2 * l, pw), F32)
    zeros_ss = jnp.zeros((l, pw), F32)
    a_ab, a_rb, a_ak, a_rk, akv = {}, {}, {}, {}, {}
    for it in items:
        lhs = jnp.concatenate([cut(al, *it), cut(rt, *it)], axis=0)
        g = _dot_nt(lhs, jnp.concatenate([bd(cut(be, *it)), bd(cut(kt, *it))], axis=0))
        a_ab[it] = jnp.where(strict, g[:l, :pw], 0.0)
        a_ak[it] = jnp.where(strict, g[:l, pw:], 0.0)
        a_rb[it] = jnp.where(incl, g[l:, :pw], 0.0)
        a_rk[it] = jnp.where(incl, g[l:, pw:], 0.0)
    for it in items:
        akv[it] = _dot(a_ak[it], bd(cut(v, *it)))
    xs = {it: _dot(a_ab[it], bd(a_ab[it])) for it in items}
    ts = {it: eye_ss + a_ab[it] for it in items}
    for _ in range(int(math.log2(l)) - 2):
        for it in items:
            both = _dot(jnp.concatenate([xs[it], ts[it]], axis=0), bd(xs[it]))
            xs[it] = both[:l]
            ts[it] = ts[it] + both[l:]
    for it in items:
        ts[it] = ts[it] + _dot(ts[it], bd(xs[it]))
    w_t, u0 = {}, {}
    for it in items:
        wu = _dot(ts[it], jnp.concatenate([bd(cut(al, *it)), bd(akv[it])], axis=1))
        w_t[it] = wu[:, :pw]
        u0[it] = wu[:, pw:]
    r_hat, y0, m_bd, c_bd = {}, {}, {}, {}
    for it in items:
        v_i = cut(v, *it)
        ry = _dot(jnp.concatenate([a_rb[it], a_rk[it]], axis=1),
                  jnp.concatenate([jnp.concatenate([bd(w_t[it]), bd(u0[it])], axis=1),
                                   jnp.concatenate([zeros_bd, bd(v_i)], axis=1)], axis=0))
        r_hat[it] = cut(rt, *it) + ry[:, :pw]
        y0[it] = ry[:, pw:]
        mc = _dot_tn(jnp.concatenate([cut(bh, *it), cut(kh, *it)], axis=0),
                     jnp.concatenate([jnp.concatenate([w_t[it], u0[it]], axis=1),
                                      jnp.concatenate([zeros_ss, v_i], axis=1)], axis=0))
        decay = jnp.broadcast_to(cut(p_tot, *it)[:1], (2 * l, pw))
        m_bd[it] = jnp.where(diag_mask, mc[:, :pw], 0.0) + jnp.where(eye_bd, decay, 0.0)
        c_bd[it] = jnp.where(diag_mask, mc[:, pw:], 0.0)
    st = [st_ref[p] for p in range(n_pairs)]
    for c in range(n_chunks):
        for p in range(n_pairs):
            it = (c, p)
            both = _dot(jnp.concatenate([r_hat[it], m_bd[it]], axis=0), st[p])
            o_ref[0, c * l:(c + 1) * l, p * pw:(p + 1) * pw] = both[:l] + y0[it]
            st[p] = both[l:] + c_bd[it]
    for p in range(n_pairs):
        st_ref[p] = st[p]

    y = o_ref[0]
    inv_n = 1.0 / HEAD_DIM
    mean = _seg_sum(y, ones_ref, 1) * inv_n
    yc = y - mean
    var = _seg_sum(yc * yc, ones_ref, 1) * inv_n
    yn = yc * lax.rsqrt(var + GN_EPS) * gnw_ref[...] + gnb_ref[...]
    bonus = _seg_sum(r * k * rk_ref[...], ones_ref, 1) * v
    o_ref[0] = (yn + bonus) * gate


def _rwkv_mix(rest, shift_mu, decay_w0, w_decay_up, iclr_a0, w_iclr_up, w_gate_lr_up, k_k, k_a,
              r_k, gn_w, gn_b):
    b, s, _ = rest.shape
    w = RWKV_WIDTH
    tc = RWKV_TILE
    assert s % tc == 0 and tc % MXU_DEPTH == 0 and MXU_DEPTH % CHUNK == 0
    row = lambda x: x.reshape(1, -1)
    full = lambda shape: pl.BlockSpec(shape, lambda bi, i: (0, 0))
    return pl.pallas_call(
        functools.partial(_rwkv_kernel, n_chunks=tc // CHUNK),
        grid=(b, s // tc),
        in_specs=[
            pl.BlockSpec((1, tc, SHIFT_WIDTH), lambda bi, i: (bi, i, 0)),
            full((1, SHIFT_WIDTH)), full((1, w)), full((DECAY_LORA, w)), full((1, w)),
            full((ICLR_LORA, w)), full((GATE_LORA, w)), full((1, w)), full((1, w)),
            full((1, w)), full((1, w)), full((1, w)),
        ],
        out_specs=pl.BlockSpec((1, tc, w), lambda bi, i: (bi, i, 0)),
        out_shape=jax.ShapeDtypeStruct((b, s, w), F32),
        scratch_shapes=[
            pltpu.VMEM((w // LANES, LANES, LANES), F32),
            pltpu.VMEM((1, SHIFT_WIDTH), F32),
            pltpu.VMEM((MXU_DEPTH, MXU_DEPTH), BF16),
            pltpu.VMEM((MXU_DEPTH, MXU_DEPTH), BF16),
        ],
        compiler_params=_params(("parallel", "arbitrary")),
        name="rwkv7_mix",
    )(rest, row(shift_mu), row(decay_w0), w_decay_up, row(iclr_a0), w_iclr_up, w_gate_lr_up,
      row(k_k), row(k_a), row(r_k), row(gn_w), row(gn_b))


def _mix_kernel(attn_ref, rw_ref, x_ref, wo_ref, nf_ref, wr_ref, br_ref,
                h_out, xn_out, idx_out, gate_out, rank_out, cnt_out, cnt_ref):
    @pl.when(pl.program_id(0) == 0)
    def _():
        cnt_ref[...] = jnp.zeros_like(cnt_ref)

    mixed = (jnp.dot(attn_ref[...].astype(BF16), wo_ref[:ATTN_WIDTH, :], preferred_element_type=F32)
             + jnp.dot(rw_ref[...].astype(BF16), wo_ref[ATTN_WIDTH:, :], preferred_element_type=F32))
    h = x_ref[...] + mixed
    h_out[...] = h
    xn = h * lax.rsqrt(jnp.mean(h * h, axis=-1, keepdims=True) + RMS_EPS) * nf_ref[...]
    _store_row_tiles(xn_out, xn)
    logits = _dot_hi(xn, wr_ref[...]) + br_ref[...]
    eidx = lax.broadcasted_iota(I32, logits.shape, 1).astype(F32)
    lane = lax.broadcasted_iota(I32, (logits.shape[0], LANES), 1)
    idx_pad = jnp.zeros((logits.shape[0], LANES), F32)
    val_pad = jnp.full((logits.shape[0], LANES), MASK_VALUE, F32)
    cur = logits
    sels = []
    for kth in range(TOP_K):
        m = jnp.max(cur, axis=-1, keepdims=True)
        sel = jnp.min(jnp.where(cur == m, eidx, float(N_EXPERTS)), axis=-1, keepdims=True)
        sels.append(sel)
        idx_pad = jnp.where(lane == kth, sel, idx_pad)
        val_pad = jnp.where(lane == kth, m, val_pad)
        cur = jnp.where(eidx == sel, -jnp.inf, cur)
    top = jnp.max(val_pad, axis=-1, keepdims=True)
    e = jnp.exp(val_pad - top)
    gate_out[...] = e / jnp.sum(e, axis=-1, keepdims=True)
    idx_out[...] = idx_pad.astype(I32)

    tm = logits.shape[0]
    lane_f = lane.astype(F32)
    picked = [lane_f == sel for sel in sels]
    onehot = sum(jnp.where(pk, 1.0, 0.0) for pk in picked)
    tr = lax.broadcasted_iota(I32, (tm, tm), 0)
    tc = lax.broadcasted_iota(I32, (tm, tm), 1)
    earlier = jnp.where(tr > tc, 1.0, 0.0).astype(BF16)
    before = jnp.dot(earlier, onehot.astype(BF16), preferred_element_type=F32) + cnt_ref[...]
    rank_pad = jnp.zeros((tm, LANES), F32)
    for kth, pk in enumerate(picked):
        rank_k = jnp.sum(jnp.where(pk, before, 0.0), axis=-1, keepdims=True)
        rank_pad = jnp.where(lane == kth, rank_k, rank_pad)
    rank_out[...] = rank_pad.astype(I32)
    counts = cnt_ref[...] + jnp.sum(onehot, axis=0, keepdims=True)
    cnt_ref[...] = counts
    cnt_out[...] = counts


def _mix_out(attn, rw, xf, w_out, norm_ffn_g, w_router, b_router):
    n = xf.shape[0]
    tm = TOKEN_TILE
    row = lambda x: x.reshape(1, -1)
    half = pl.BlockSpec((tm, RWKV_WIDTH), lambda i: (i, 0))
    wide = pl.BlockSpec((tm, D_MODEL), lambda i: (i, 0))
    pad = pl.BlockSpec((tm, LANES), lambda i: (i, 0))
    full = lambda shape: pl.BlockSpec(shape, lambda i: (0, 0))
    return pl.pallas_call(
        _mix_kernel,
        grid=(n // tm,),
        in_specs=[half, half, wide, full((D_MODEL, D_MODEL)), full((1, D_MODEL)),
                  full((D_MODEL, N_EXPERTS)), full((1, N_EXPERTS))],
        out_specs=[wide, pl.BlockSpec((tm * ROW_TILE, LANES), lambda i: (i, 0)), pad, pad, pad,
                   full((1, LANES))],
        out_shape=[jax.ShapeDtypeStruct((n, D_MODEL), F32),
                   jax.ShapeDtypeStruct((n * ROW_TILE, LANES), F32),
                   jax.ShapeDtypeStruct((n, LANES), I32), jax.ShapeDtypeStruct((n, LANES), F32),
                   jax.ShapeDtypeStruct((n, LANES), I32), jax.ShapeDtypeStruct((1, LANES), F32)],
        scratch_shapes=[pltpu.VMEM((1, LANES), F32)],
        compiler_params=_params(("arbitrary",)),
        name="mix_out_router",
    )(attn, rw, xf, w_out.astype(BF16), row(norm_ffn_g), w_router, row(b_router))


def _dispatch_kernel(pad_ref, nvalid_ref, dest_ref, x_ref, wgu_ref, wd_ref, out_ref, wgu_bf_ref,
                     wd_bf_ref, zbuf, sem, zsem, *, tm, bm, n_blocks):
    tile = lambda row, n_rows: pl.ds(pl.multiple_of(row * ROW_TILE, ROW_TILE), n_rows * ROW_TILE)

    @pl.when(pl.program_id(0) == 0)
    def _():
        zbuf[...] = jnp.zeros_like(zbuf)

        def fill(start):
            cp = pltpu.make_async_copy(zbuf, out_ref.at[tile(start, bm), :], zsem)
            cp.start()
            cp.wait()

        def pad_fill(e, carry):
            fill(pad_ref[e])
            return carry

        def tail_fill(blk, carry):
            fill(blk * bm)
            return carry

        lax.fori_loop(0, N_EXPERTS, pad_fill, 0)
        lax.fori_loop(nvalid_ref[0], n_blocks + 1, tail_fill, 0)

    def issue(j, carry):
        for kth in range(TOP_K):
            pltpu.make_async_copy(
                x_ref.at[tile(j, 1), :],
                out_ref.at[tile(dest_ref[j * TOP_K + kth], 1), :],
                sem).start(priority=kth % 2)
        return carry

    lax.fori_loop(0, tm, issue, 0, unroll=8)
    wgu_bf_ref[...] = wgu_ref[...].astype(BF16)
    wd_bf_ref[...] = wd_ref[...].astype(BF16)
    for kth in range(TOP_K):
        pltpu.make_async_copy(x_ref, out_ref.at[tile(0, tm), :], sem).wait()


def _dispatch(xn_tiles, dest_flat, pad_start, n_valid, w_gate_up, w_down, n_blocks, bm):
    n = xn_tiles.shape[0] // ROW_TILE
    tm = TOKEN_TILE
    n_steps = n // tm
    w_rows = N_EXPERTS * D_MODEL
    assert w_rows % n_steps == 0 and (w_rows // n_steps) % (2 * SUBLANES) == 0
    wr = w_rows // n_steps
    xs, wgu_bf, wd_bf = pl.pallas_call(
        functools.partial(_dispatch_kernel, tm=tm, bm=bm, n_blocks=n_blocks),
        grid_spec=pltpu.PrefetchScalarGridSpec(
            num_scalar_prefetch=2,
            grid=(n_steps,),
            in_specs=[
                pl.BlockSpec((tm * TOP_K,), lambda i, ps, nv: (i,), memory_space=pltpu.SMEM),
                pl.BlockSpec((tm * ROW_TILE, LANES), lambda i, ps, nv: (i, 0)),
                pl.BlockSpec((wr, 2 * D_FF), lambda i, ps, nv: (i, 0)),
                pl.BlockSpec((wr, D_MODEL), lambda i, ps, nv: (i, 0)),
            ],
            out_specs=[pl.BlockSpec(memory_space=pl.ANY),
                       pl.BlockSpec((wr, 2 * D_FF), lambda i, ps, nv: (i, 0)),
                       pl.BlockSpec((wr, D_MODEL), lambda i, ps, nv: (i, 0))],
            scratch_shapes=[pltpu.VMEM((bm * ROW_TILE, LANES), F32), pltpu.SemaphoreType.DMA,
                            pltpu.SemaphoreType.DMA],
        ),
        out_shape=[jax.ShapeDtypeStruct(((n_blocks + 1) * bm * ROW_TILE, LANES), F32),
                   jax.ShapeDtypeStruct((w_rows, 2 * D_FF), BF16),
                   jax.ShapeDtypeStruct((w_rows, D_MODEL), BF16)],
        compiler_params=_params(("arbitrary",)),
        name="moe_dispatch",
    )(pad_start, n_valid, dest_flat, xn_tiles, w_gate_up.reshape(w_rows, 2 * D_FF),
      w_down.reshape(w_rows, D_MODEL))
    return (xs, wgu_bf.reshape(N_EXPERTS, D_MODEL, 2 * D_FF),
            wd_bf.reshape(N_EXPERTS, D_FF, D_MODEL))


def _expert_kernel(be_ref, nvalid_ref, x_ref, wgu_ref, bgu_ref, wd_ref, bd_ref, o_ref):
    i = pl.program_id(0)
    cw = MXU_DEPTH
    wgu_bf = wgu_ref.at[0]
    wd_bf = wd_ref.at[0]

    @pl.when(i < nvalid_ref[0])
    def _():
        bm = x_ref.shape[0] // ROW_TILE
        x = _load_row_tiles(x_ref, bm).astype(BF16)
        acts = []
        gus = []
        for j in range(D_FF // cw):
            g = jnp.dot(x, wgu_bf[:, j * cw:(j + 1) * cw], preferred_element_type=F32)
            u = jnp.dot(x, wgu_bf[:, D_FF + j * cw:D_FF + (j + 1) * cw],
                        preferred_element_type=F32)
            gus.append((g + bgu_ref[0, :, j * cw:(j + 1) * cw],
                        u + bgu_ref[0, :, D_FF + j * cw:D_FF + (j + 1) * cw]))
        for g, u in gus:
            g = jnp.minimum(g, SWIGLU_LIMIT)
            u = jnp.clip(u, -SWIGLU_LIMIT, SWIGLU_LIMIT)
            acts.append(((u + 1.0) * (g * _sigmoid(SWIGLU_ALPHA * g))).astype(BF16))
        out = jnp.dot(jnp.concatenate(acts, axis=1), wd_bf[...], preferred_element_type=F32)
        _store_row_tiles(o_ref, out + bd_ref[0])

    @pl.when(i >= nvalid_ref[0])
    def _():
        o_ref[...] = jnp.zeros_like(o_ref)


def _experts(xs, blk_expert, n_valid, w_gate_up, b_gate_up, w_down, b_down, n_blocks, bm):
    n_rows = n_blocks * bm
    return pl.pallas_call(
        _expert_kernel,
        grid_spec=pltpu.PrefetchScalarGridSpec(
            num_scalar_prefetch=2,
            grid=(n_blocks,),
            in_specs=[
                pl.BlockSpec((bm * ROW_TILE, LANES), lambda i, be, nv: (i, 0)),
                pl.BlockSpec((1, D_MODEL, 2 * D_FF), lambda i, be, nv: (be[i], 0, 0)),
                pl.BlockSpec((1, 1, 2 * D_FF), lambda i, be, nv: (be[i], 0, 0)),
                pl.BlockSpec((1, D_FF, D_MODEL), lambda i, be, nv: (be[i], 0, 0)),
                pl.BlockSpec((1, 1, D_MODEL), lambda i, be, nv: (be[i], 0, 0)),
            ],
            out_specs=pl.BlockSpec((bm * ROW_TILE, LANES), lambda i, be, nv: (i, 0)),
        ),
        out_shape=jax.ShapeDtypeStruct((n_rows * ROW_TILE, LANES), F32),
        compiler_params=_params(("arbitrary",)),
        name="moe_experts",
    )(blk_expert, n_valid, xs, w_gate_up, b_gate_up.reshape(N_EXPERTS, 1, 2 * D_FF),
      w_down, b_down.reshape(N_EXPERTS, 1, D_MODEL))


def _combine_kernel(dest_ref, dest_next_ref, h_ref, gate_ref, g_ref, ys_ref, o_ref, buf, sem,
                    *, tm, n_tiles):
    i = pl.program_id(0)
    slot = i % 2
    tile = lambda row, n_rows: pl.ds(pl.multiple_of(row * ROW_TILE, ROW_TILE), n_rows * ROW_TILE)

    def issue(dref, s):
        def body(j, carry):
            for kth in range(TOP_K):
                pltpu.make_async_copy(
                    ys_ref.at[tile(dref[j * TOP_K + kth], 1), :],
                    buf.at[s, kth, tile(j, 1), :],
                    sem.at[s]).start(priority=kth % 2)
            return carry

        lax.fori_loop(0, tm, body, 0, unroll=8)

    @pl.when(i == 0)
    def _():
        issue(dest_ref, 0)

    def step(s):
        @pl.when(i + 1 < n_tiles)
        def _():
            issue(dest_next_ref, 1 - s)

        for kth in range(TOP_K):
            pltpu.make_async_copy(ys_ref.at[tile(0, tm), :], buf.at[s, kth], sem.at[s]).wait()
        gates = gate_ref[...]
        hf = h_ref[...]
        for kth in range(TOP_K):
            hf = hf + gates[:, kth:kth + 1] * _load_row_tiles(buf.at[s, kth], tm)
        o_ref[...] = hf * lax.rsqrt(jnp.mean(hf * hf, axis=-1, keepdims=True) + RMS_EPS) * g_ref[...]

    for s in range(2):
        pl.when(slot == s)(functools.partial(step, s))


def _combine(h, gate_pad, ys, dest_flat, norm_final_g):
    n = h.shape[0]
    tm = TOKEN_TILE
    n_tiles = n // tm
    return pl.pallas_call(
        functools.partial(_combine_kernel, tm=tm, n_tiles=n_tiles),
        grid=(n_tiles,),
        in_specs=[
            pl.BlockSpec((tm * TOP_K,), lambda i: (i,), memory_space=pltpu.SMEM),
            pl.BlockSpec((tm * TOP_K,), lambda i: (jnp.minimum(i + 1, n_tiles - 1),),
                         memory_space=pltpu.SMEM),
            pl.BlockSpec((tm, D_MODEL), lambda i: (i, 0)),
            pl.BlockSpec((tm, LANES), lambda i: (i, 0)),
            pl.BlockSpec((1, D_MODEL), lambda i: (0, 0)),
            pl.BlockSpec(memory_space=pl.ANY),
        ],
        out_specs=pl.BlockSpec((tm, D_MODEL), lambda i: (i, 0)),
        out_shape=jax.ShapeDtypeStruct((n, D_MODEL), F32),
        scratch_shapes=[pltpu.VMEM((2, TOP_K, tm * ROW_TILE, LANES), F32),
                        pltpu.SemaphoreType.DMA((2,))],
        compiler_params=_params(("arbitrary",)),
        name="moe_combine_norm",
    )(dest_flat, dest_flat, h, gate_pad, norm_final_g.reshape(1, D_MODEL), ys)


def _routing(top_idx, rank, counts, bm):
    n = top_idx.shape[0]
    padded = (counts + bm - 1) // bm * bm
    pend = jnp.cumsum(padded)
    pstart = pend - padded
    dest = pstart[top_idx] + rank
    n_blocks = (n * TOP_K + N_EXPERTS * (bm - 1) + bm - 1) // bm
    blk_first_row = jnp.arange(n_blocks, dtype=pend.dtype) * bm
    blk_expert = jnp.minimum(
        jnp.sum((pend[None, :] <= blk_first_row[:, None]).astype(I32), axis=1),
        N_EXPERTS - 1).astype(I32)
    n_valid = (pend[-1:] // bm).astype(I32)
    pad_start = (pstart + counts).astype(I32)
    return dest.reshape(n * TOP_K).astype(I32), blk_expert, n_valid, pad_start, n_blocks


def _layer(xf, b, s, p):
    qkv, rest = _in_proj(xf, p["norm_mix_g"], p["w_in"])
    slopes = jnp.exp2(-8.0 / ATTN_HEADS * jnp.arange(1, ATTN_HEADS + 1, dtype=F32))
    attn = _attention(qkv.reshape(b, s, 3 * ATTN_WIDTH), slopes)
    rw = _rwkv_mix(
        rest.reshape(b, s, SHIFT_WIDTH), p["shift_mu"], p["decay_w0"], p["w_decay_up"],
        p["iclr_a0"], p["w_iclr_up"], p["w_gate_lr_up"], p["k_k"], p["k_a"],
        p["r_k"].reshape(-1), p["gn_w"], p["gn_b"])
    flat = lambda t: t.reshape(b * s, -1)
    h, xn, idx_pad, gate_pad, rank_pad, counts = _mix_out(
        flat(attn), flat(rw), xf, p["w_out"], p["norm_ffn_g"], p["w_router"], p["b_router"])
    bm = EXPERT_BLOCK
    dest, blk_expert, n_valid, pad_start, n_blocks = _routing(
        idx_pad[:, :TOP_K], rank_pad[:, :TOP_K], counts[0, :N_EXPERTS].astype(I32), bm)
    xs, wgu_bf, wd_bf = _dispatch(xn, dest, pad_start, n_valid, p["w_gate_up"], p["w_down"],
                                  n_blocks, bm)
    ys = _experts(xs, blk_expert, n_valid, wgu_bf, p["b_gate_up"], wd_bf, p["b_down"], n_blocks, bm)
    return h, gate_pad, ys, dest


def kernel(x, norm_mix_g, w_in, shift_mu, decay_w0, w_decay_up, iclr_a0, w_iclr_up, w_gate_lr_up,
           k_k, k_a, r_k, gn_w, gn_b, w_out, norm_ffn_g, w_router, b_router, w_gate_up,
           b_gate_up, w_down, b_down, norm_final_g):
    b, s, d = x.shape
    assert d == D_MODEL and w_in.shape[0] == 1, "single-layer block"
    names = ("norm_mix_g", "w_in", "shift_mu", "decay_w0", "w_decay_up", "iclr_a0", "w_iclr_up",
             "w_gate_lr_up", "k_k", "k_a", "r_k", "gn_w", "gn_b", "w_out", "norm_ffn_g",
             "w_router", "b_router", "w_gate_up", "b_gate_up", "w_down", "b_down")
    vals = (norm_mix_g, w_in, shift_mu, decay_w0, w_decay_up, iclr_a0, w_iclr_up, w_gate_lr_up,
            k_k, k_a, r_k, gn_w, gn_b, w_out, norm_ffn_g, w_router, b_router, w_gate_up,
            b_gate_up, w_down, b_down)
    p = {nm: v[0] for nm, v in zip(names, vals)}
    h, gate_pad, ys, dest = _layer(x.reshape(b * s, d), b, s, p)
    out = _combine(h, gate_pad, ys, dest, norm_final_g)
    return out.reshape(b, s, d)
```

```python
import functools
import math

import jax
import jax.numpy as jnp
from jax import lax
from jax.experimental import pallas as pl
from jax.experimental.pallas import tpu as pltpu

F32 = jnp.float32
BF16 = jnp.bfloat16
I32 = jnp.int32

D_MODEL = 1024
HEAD_DIM = 64
ATTN_WIDTH = 512
ATTN_HEADS = ATTN_WIDTH // HEAD_DIM
RWKV_WIDTH = 512
DILATED_PATTERNS = ((128, 1), (512, 4), (2048, 16))
ATTN_BLOCK = 128
DECAY_LORA = 64
ICLR_LORA = 64
GATE_LORA = 128
SHIFT_WIDTH = 3 * RWKV_WIDTH + DECAY_LORA + ICLR_LORA + GATE_LORA
N_EXPERTS = 32
TOP_K = 4
D_FF = D_MODEL
SWIGLU_LIMIT = 7.0
SWIGLU_ALPHA = 1.702
RMS_EPS = 1e-5
GN_EPS = 64e-5

LANES = 128
SUBLANES = 8
MXU_DEPTH = 256
ROW_TILE = D_MODEL // LANES
assert ROW_TILE == SUBLANES, "a D_MODEL-wide f32 row must fill exactly one (8,128) tile"
CHUNK = 64
MASK_VALUE = -1e30
LOG2_E = 1.4426950408889634
ATTN_GROUP = 4
VMEM_LIMIT = 56 * 1024 * 1024

TOKEN_TILE = 512
RWKV_TILE = 512
EXPERT_BLOCK = 512


def _params(semantics, vmem=VMEM_LIMIT):
    return pltpu.CompilerParams(dimension_semantics=semantics, vmem_limit_bytes=vmem)


def _dot(a, b):
    return jnp.dot(a.astype(BF16), b.astype(BF16), preferred_element_type=F32)


def _dot_nt(a, b):
    return lax.dot_general(a.astype(BF16), b.astype(BF16), (((1,), (1,)), ((), ())),
                           preferred_element_type=F32)


def _dot_tn(a, b):
    return lax.dot_general(a.astype(BF16), b.astype(BF16), (((0,), (0,)), ((), ())),
                           preferred_element_type=F32)


def _split3(x):
    h = x.astype(BF16)
    r1 = x - h.astype(F32)
    m = r1.astype(BF16)
    l = (r1 - m.astype(F32)).astype(BF16)
    return h, m, l


def _dot_exact_lhs(a_bf16, x):
    h, m, l = _split3(x)
    d = lambda y: jnp.dot(a_bf16, y, preferred_element_type=F32)
    return d(h) + d(m) + d(l)


def _dot_hi(a, b):
    ah = a.astype(BF16)
    al = (a - ah.astype(F32)).astype(BF16)
    bh = b.astype(BF16)
    bl = (b - bh.astype(F32)).astype(BF16)
    d = lambda x, y: jnp.dot(x, y, preferred_element_type=F32)
    return d(ah, bh) + d(ah, bl) + d(al, bh)


def _store_row_tiles(ref, x):
    rows = x.shape[0]
    for s in range(ROW_TILE):
        ref[pl.ds(s, rows, stride=ROW_TILE), :] = x[:, s * LANES:(s + 1) * LANES]


def _load_row_tiles(ref, rows):
    return jnp.concatenate(
        [ref[pl.ds(s, rows, stride=ROW_TILE), :] for s in range(ROW_TILE)], axis=1)


def _sigmoid(x):
    return 1.0 / (1.0 + jnp.exp(-x))


def _inproj_kernel(x_ref, g_ref, wq_ref, wr_ref, qkv_ref, rest_ref):
    x = x_ref[...]
    xn = x * lax.rsqrt(jnp.mean(x * x, axis=-1, keepdims=True) + RMS_EPS) * g_ref[...]
    xb = xn.astype(BF16)
    qkv_ref[...] = jnp.dot(xb, wq_ref[...], preferred_element_type=F32)
    rest_ref[...] = jnp.dot(xb, wr_ref[...], preferred_element_type=F32)


def _in_proj(xf, g, w_in):
    n = xf.shape[0]
    tm = TOKEN_TILE
    wq = w_in[:, :3 * ATTN_WIDTH].astype(BF16)
    wr = w_in[:, 3 * ATTN_WIDTH:].astype(BF16)
    return pl.pallas_call(
        _inproj_kernel,
        grid=(n // tm,),
        in_specs=[
            pl.BlockSpec((tm, D_MODEL), lambda i: (i, 0)),
            pl.BlockSpec((1, D_MODEL), lambda i: (0, 0)),
            pl.BlockSpec((D_MODEL, 3 * ATTN_WIDTH), lambda i: (0, 0)),
            pl.BlockSpec((D_MODEL, SHIFT_WIDTH), lambda i: (0, 0)),
        ],
        out_specs=[
            pl.BlockSpec((tm, 3 * ATTN_WIDTH), lambda i: (i, 0)),
            pl.BlockSpec((tm, SHIFT_WIDTH), lambda i: (i, 0)),
        ],
        out_shape=[
            jax.ShapeDtypeStruct((n, 3 * ATTN_WIDTH), F32),
            jax.ShapeDtypeStruct((n, SHIFT_WIDTH), F32),
        ],
        compiler_params=_params(("parallel",)),
        name="in_proj",
    )(xf, g.reshape(1, D_MODEL), wq, wr)


def _attn_kernel(slopes_ref, q_ref, k_ref, v_ref, o_ref, bias_ref, m_acc, d_acc, n_acc, *, seq):
    c = ATTN_BLOCK
    hp = pl.program_id(1)
    lane = lax.broadcasted_iota(I32, (c, LANES), 1)
    lo_half = lane < HEAD_DIM
    qi = lax.broadcasted_iota(I32, (c, 2 * c), 0)
    kj = lax.broadcasted_iota(I32, (c, 2 * c), 1)
    diff = qi + c - kj
    for p, (window, dil) in enumerate(DILATED_PATTERNS):
        steps = window // dil
        valid = (diff >= 0) & (diff <= steps)
        dist = (dil * diff).astype(F32)
        for hh in range(2):
            slope = slopes_ref[hp * 2 + hh] * LOG2_E
            bias = jnp.where(valid, -slope * dist, MASK_VALUE)
            bias_ref[2 * (p * 2 + hh)] = bias
            bias_ref[2 * (p * 2 + hh) + 1] = jnp.where(kj >= c, bias, MASK_VALUE)

    scale = LOG2_E / math.sqrt(HEAD_DIM)
    order = sorted(range(len(DILATED_PATTERNS)), key=lambda p: -DILATED_PATTERNS[p][1])

    def rows(start, dil):
        if dil == 1:
            return pl.ds(pl.multiple_of(start, c), c)
        return pl.ds(start, c, stride=dil)

    def group(i, carry, p, dil, nb):
        blocks = []
        for g in range(ATTN_GROUP):
            j = i * ATTN_GROUP + g
            if nb >= ATTN_GROUP:
                r, n = j // nb, j % nb
                first = (n == 0) if g == 0 else False
            else:
                r, n = j // nb, g % nb
                first = n == 0
            blocks.append((r + dil * c * n, r + dil * c * jnp.maximum(n - 1, 0), first))
        qs, ks, vs = [], [], []
        kc = vc = None
        for g, (cur, prev, first) in enumerate(blocks):
            qs.append(q_ref[0, rows(cur, dil), :] * scale)
            kp, vp = kc, vc
            kc = k_ref[0, rows(cur, dil), :].astype(BF16)
            vc = v_ref[0, rows(cur, dil), :].astype(BF16)
            if first is True:
                ks.append(kc)
                vs.append(vc)
                continue
            if g == 0:
                kp = k_ref[0, rows(prev, dil), :].astype(BF16)
                vp = v_ref[0, rows(prev, dil), :].astype(BF16)
            ks.append(jnp.concatenate([kp, kc], axis=0))
            vs.append(jnp.concatenate([vp, vc], axis=0))
        scores = []
        for (cur, prev, first), qf, kcat in zip(blocks, qs, ks):
            for hh in range(2):
                head_mask = lo_half if hh == 0 else jnp.logical_not(lo_half)
                qh = jnp.where(head_mask, qf, 0.0).astype(BF16)
                s = lax.dot_general(qh, kcat, (((1,), (1,)), ((), ())), preferred_element_type=F32)
                slot = 2 * (p * 2 + hh)
                if first is True:
                    s = s + bias_ref[slot][:, c:]
                elif first is False:
                    s = s + bias_ref[slot]
                else:
                    s = s + bias_ref[slot + first.astype(I32)]
                scores.append(s)
        probs = []
        for s in scores:
            m = jnp.max(s, axis=-1, keepdims=True)
            e = jnp.exp2(s - m)
            probs.append((m, jnp.sum(e, axis=-1, keepdims=True), e.astype(BF16)))
        for bi, (cur, prev, first) in enumerate(blocks):
            (m0, d0, e0), (m1, d1, e1) = probs[2 * bi], probs[2 * bi + 1]
            o0 = jnp.dot(e0, vs[bi], preferred_element_type=F32)
            o1 = jnp.dot(e1, vs[bi], preferred_element_type=F32)
            m_b = jnp.where(lo_half, m0, m1)
            d_b = jnp.where(lo_half, d0, d1)
            n_b = jnp.where(lo_half, o0, o1)
            sl = rows(cur, dil)
            if p == order[0]:
                m_acc[sl, :] = m_b
                d_acc[sl, :] = d_b
                n_acc[sl, :] = n_b
            else:
                m_o = m_acc[sl, :]
                m_n = jnp.maximum(m_o, m_b)
                a_o = jnp.exp2(m_o - m_n)
                a_b = jnp.exp2(m_b - m_n)
                d_n = d_acc[sl, :] * a_o + d_b * a_b
                n_n = n_acc[sl, :] * a_o + n_b * a_b
                if p == order[-1]:
                    o_ref[0, sl, :] = n_n / d_n
                else:
                    d_acc[sl, :] = d_n
                    n_acc[sl, :] = n_n
                    m_acc[sl, :] = m_n
        return carry

    for p in order:
        dil = DILATED_PATTERNS[p][1]
        nb = seq // (c * dil)
        n_groups = (nb * dil) // ATTN_GROUP
        lax.fori_loop(0, n_groups, functools.partial(group, p=p, dil=dil, nb=nb), 0)


def _attention(qkv, slopes):
    b, s, _ = qkv.shape
    for window, dil in DILATED_PATTERNS:
        nb = s // (ATTN_BLOCK * dil)
        assert s % (ATTN_BLOCK * dil) == 0 and window // dil <= ATTN_BLOCK
        assert nb % ATTN_GROUP == 0 or (ATTN_GROUP % nb == 0 and (nb * dil) % ATTN_GROUP == 0)
    n_pairs = ATTN_WIDTH // LANES
    blk = lambda off: pl.BlockSpec((1, s, LANES), lambda bi, hp, sl: (bi, 0, off + hp))
    return pl.pallas_call(
        functools.partial(_attn_kernel, seq=s),
        grid_spec=pltpu.PrefetchScalarGridSpec(
            num_scalar_prefetch=1,
            grid=(b, n_pairs),
            in_specs=[blk(0), blk(n_pairs), blk(2 * n_pairs)],
            out_specs=pl.BlockSpec((1, s, LANES), lambda bi, hp, sl: (bi, 0, hp)),
            scratch_shapes=[
                pltpu.VMEM((4 * len(DILATED_PATTERNS), ATTN_BLOCK, 2 * ATTN_BLOCK), F32),
                pltpu.VMEM((s, LANES), F32),
                pltpu.VMEM((s, LANES), F32),
                pltpu.VMEM((s, LANES), F32),
            ],
        ),
        out_shape=jax.ShapeDtypeStruct((b, s, ATTN_WIDTH), F32),
        compiler_params=_params(("parallel", "parallel")),
        name="dilated_attention",
    )(slopes, qkv, qkv, qkv)


def _seg_sum(x, ones_ref, passes):
    hw = ones_ref.shape[0]
    terms = [x.astype(BF16)]
    if passes == 2:
        terms.append((x - terms[0].astype(F32)).astype(BF16))
    parts = []
    for j in range(x.shape[1] // hw):
        sl = slice(j * hw, (j + 1) * hw)
        parts.append(sum(jnp.dot(t[:, sl], ones_ref[...], preferred_element_type=F32) for t in terms))
    return jnp.concatenate(parts, axis=1)


def _rwkv_kernel(rest_ref, mu_ref, w0_ref, wdu_ref, a0_ref, wau_ref, wg_ref, kk_ref, ka_ref,
                 rk_ref, gnw_ref, gnb_ref, o_ref, st_ref, prev_ref, tri_ref, ones_ref, *, n_chunks):
    l = CHUNK
    pw = LANES
    n_pairs = RWKV_WIDTH // pw
    tc = n_chunks * l
    w = RWKV_WIDTH

    @pl.when(pl.program_id(1) == 0)
    def _():
        st_ref[...] = jnp.zeros_like(st_ref)
        prev_ref[...] = jnp.zeros_like(prev_ref)
        ri = lax.broadcasted_iota(I32, tri_ref.shape, 0)
        ci = lax.broadcasted_iota(I32, tri_ref.shape, 1)
        tri_ref[...] = jnp.where((ri >= ci) & (ri // l == ci // l), 1.0, 0.0).astype(BF16)
        hr = lax.broadcasted_iota(I32, ones_ref.shape, 0) // HEAD_DIM
        hc = lax.broadcasted_iota(I32, ones_ref.shape, 1) // HEAD_DIM
        ones_ref[...] = jnp.where(hr == hc, 1.0, 0.0).astype(BF16)

    z = rest_ref[0]
    zrow = lax.broadcasted_iota(I32, z.shape, 0)
    zprev = jnp.where(zrow == 0, prev_ref[...], pltpu.roll(z, 1, 0))
    prev_ref[...] = z[tc - 1:tc, :]
    xs = z + (zprev - z) * mu_ref[...]
    r = xs[:, :w]
    k_in = xs[:, w:2 * w]
    v = xs[:, 2 * w:3 * w]
    wd = xs[:, 3 * w:3 * w + DECAY_LORA]
    ad = xs[:, 3 * w + DECAY_LORA:3 * w + DECAY_LORA + ICLR_LORA]
    gd = xs[:, 3 * w + DECAY_LORA + ICLR_LORA:]
    zz = w0_ref[...] + _dot_hi(jnp.tanh(wd), wdu_ref[...])
    lw = -math.exp(-0.5) * _sigmoid(zz)
    a = _sigmoid(a0_ref[...] + _dot(ad, wau_ref[...]))
    gate = _dot(_sigmoid(gd), wg_ref[...])
    kk = k_in * kk_ref[...]
    kk = kk * lax.rsqrt(jnp.maximum(_seg_sum(kk * kk, ones_ref, 2), 1e-24))
    k = k_in * (1.0 + (a - 1.0) * ka_ref[...])
    tr = tri_ref.shape[0]
    cum = jnp.concatenate(
        [_dot_exact_lhs(tri_ref[...], lw[j * tr:(j + 1) * tr]) for j in range(tc // tr)], axis=0)
    tot = jnp.concatenate(
        [jnp.broadcast_to(cum[c * l + l - 1:c * l + l, :], (l, w)) for c in range(n_chunks)], axis=0)
    p_in = jnp.exp(cum)
    p_ex = jnp.exp(cum - lw)
    p_inv = jnp.exp(-cum)
    p_end = jnp.exp(tot - cum)
    p_tot = jnp.exp(tot)
    kka = kk * a
    al = -kk * p_ex
    be = kka * p_inv
    kt = k * p_inv
    rt = r * p_in
    bh = kka * p_end
    kh = k * p_end

    row = lax.broadcasted_iota(I32, (l, pw), 0)
    col = lax.broadcasted_iota(I32, (l, pw), 1) % l
    strict = row > col
    incl = row >= col
    eye_ss = jnp.where(row == col, 1.0, 0.0).astype(F32)
    row_bd = lax.broadcasted_iota(I32, (2 * l, pw), 0)
    lane_bd = lax.broadcasted_iota(I32, (2 * l, pw), 1)
    diag_mask = (row_bd // l) == (lane_bd // l)
    eye_bd = row_bd == lane_bd

    def bd(x):
        return jnp.where(diag_mask, jnp.concatenate([x, x], axis=0), 0.0)

    items = [(c, p) for c in range(n_chunks) for p in range(n_pairs)]
    cut = lambda t, c, p: t[c * l:(c + 1) * l, p * pw:(p + 1) * pw]

    zeros_bd = jnp.zeros((2 * l, pw), F32)
    zeros_ss = jnp.zeros((l, pw), F32)
    a_ab, a_rb, a_ak, a_rk, akv = {}, {}, {}, {}, {}
    for it in items:
        lhs = jnp.concatenate([cut(al, *it), cut(rt, *it)], axis=0)
        g = _dot_nt(lhs, jnp.concatenate([bd(cut(be, *it)), bd(cut(kt, *it))], axis=0))
        a_ab[it] = jnp.where(strict, g[:l, :pw], 0.0)
        a_ak[it] = jnp.where(strict, g[:l, pw:], 0.0)
        a_rb[it] = jnp.where(incl, g[l:, :pw], 0.0)
        a_rk[it] = jnp.where(incl, g[l:, pw:], 0.0)
    for it in items:
        akv[it] = _dot(a_ak[it], bd(cut(v, *it)))
    xs = {it: _dot(a_ab[it], bd(a_ab[it])) for it in items}
    ts = {it: eye_ss + a_ab[it] for it in items}
    for _ in range(int(math.log2(l)) - 2):
        for it in items:
            both = _dot(jnp.concatenate([xs[it], ts[it]], axis=0), bd(xs[it]))
            xs[it] = both[:l]
            ts[it] = ts[it] + both[l:]
    for it in items:
        ts[it] = ts[it] + _dot(ts[it], bd(xs[it]))
    w_t, u0 = {}, {}
    for it in items:
        wu = _dot(ts[it], jnp.concatenate([bd(cut(al, *it)), bd(akv[it])], axis=1))
        w_t[it] = wu[:, :pw]
        u0[it] = wu[:, pw:]
    r_hat, y0, m_bd, c_bd = {}, {}, {}, {}
    for it in items:
        v_i = cut(v, *it)
        ry = _dot(jnp.concatenate([a_rb[it], a_rk[it]], axis=1),
                  jnp.concatenate([jnp.concatenate([bd(w_t[it]), bd(u0[it])], axis=1),
                                   jnp.concatenate([zeros_bd, bd(v_i)], axis=1)], axis=0))
        r_hat[it] = cut(rt, *it) + ry[:, :pw]
        y0[it] = ry[:, pw:]
        mc = _dot_tn(jnp.concatenate([cut(bh, *it), cut(kh, *it)], axis=0),
                     jnp.concatenate([jnp.concatenate([w_t[it], u0[it]], axis=1),
                                      jnp.concatenate([zeros_ss, v_i], axis=1)], axis=0))
        decay = jnp.broadcast_to(cut(p_tot, *it)[:1], (2 * l, pw))
        m_bd[it] = jnp.where(diag_mask, mc[:, :pw], 0.0) + jnp.where(eye_bd, decay, 0.0)
        c_bd[it] = jnp.where(diag_mask, mc[:, pw:], 0.0)
    st = [st_ref[p] for p in range(n_pairs)]
    for c in range(n_chunks):
        for p in range(n_pairs):
            it = (c, p)
            both = _dot(jnp.concatenate([r_hat[it], m_bd[it]], axis=0), st[p])
            o_ref[0, c * l:(c + 1) * l, p * pw:(p + 1) * pw] = both[:l] + y0[it]
            st[p] = both[l:] + c_bd[it]
    for p in range(n_pairs):
        st_ref[p] = st[p]

    y = o_ref[0]
    inv_n = 1.0 / HEAD_DIM
    mean = _seg_sum(y, ones_ref, 1) * inv_n
    yc = y - mean
    var = _seg_sum(yc * yc, ones_ref, 1) * inv_n
    yn = yc * lax.rsqrt(var + GN_EPS) * gnw_ref[...] + gnb_ref[...]
    bonus = _seg_sum(r * k * rk_ref[...], ones_ref, 1) * v
    o_ref[0] = (yn + bonus) * gate


def _rwkv_mix(rest, shift_mu, decay_w0, w_decay_up, iclr_a0, w_iclr_up, w_gate_lr_up, k_k, k_a,
              r_k, gn_w, gn_b):
    b, s, _ = rest.shape
    w = RWKV_WIDTH
    tc = RWKV_TILE
    assert s % tc == 0 and tc % MXU_DEPTH == 0 and MXU_DEPTH % CHUNK == 0
    row = lambda x: x.reshape(1, -1)
    full = lambda shape: pl.BlockSpec(shape, lambda bi, i: (0, 0))
    return pl.pallas_call(
        functools.partial(_rwkv_kernel, n_chunks=tc // CHUNK),
        grid=(b, s // tc),
        in_specs=[
            pl.BlockSpec((1, tc, SHIFT_WIDTH), lambda bi, i: (bi, i, 0)),
            full((1, SHIFT_WIDTH)), full((1, w)), full((DECAY_LORA, w)), full((1, w)),
            full((ICLR_LORA, w)), full((GATE_LORA, w)), full((1, w)), full((1, w)),
            full((1, w)), full((1, w)), full((1, w)),
        ],
        out_specs=pl.BlockSpec((1, tc, w), lambda bi, i: (bi, i, 0)),
        out_shape=jax.ShapeDtypeStruct((b, s, w), F32),
        scratch_shapes=[
            pltpu.VMEM((w // LANES, LANES, LANES), F32),
            pltpu.VMEM((1, SHIFT_WIDTH), F32),
            pltpu.VMEM((MXU_DEPTH, MXU_DEPTH), BF16),
            pltpu.VMEM((MXU_DEPTH, MXU_DEPTH), BF16),
        ],
        compiler_params=_params(("parallel", "arbitrary")),
        name="rwkv7_mix",
    )(rest, row(shift_mu), row(decay_w0), w_decay_up, row(iclr_a0), w_iclr_up, w_gate_lr_up,
      row(k_k), row(k_a), row(r_k), row(gn_w), row(gn_b))


def _mix_kernel(attn_ref, rw_ref, x_ref, wo_ref, nf_ref, wr_ref, br_ref, wgu_ref, wd_ref,
                h_out, xn_out, idx_out, gate_out, rank_out, cnt_out, wgu_bf_ref, wd_bf_ref, cnt_ref):
    wgu_bf_ref[...] = wgu_ref[...].astype(BF16)
    wd_bf_ref[...] = wd_ref[...].astype(BF16)

    @pl.when(pl.program_id(0) == 0)
    def _():
        cnt_ref[...] = jnp.zeros_like(cnt_ref)

    mixed = (jnp.dot(attn_ref[...].astype(BF16), wo_ref[:ATTN_WIDTH, :], preferred_element_type=F32)
             + jnp.dot(rw_ref[...].astype(BF16), wo_ref[ATTN_WIDTH:, :], preferred_element_type=F32))
    h = x_ref[...] + mixed
    h_out[...] = h
    xn = h * lax.rsqrt(jnp.mean(h * h, axis=-1, keepdims=True) + RMS_EPS) * nf_ref[...]
    _store_row_tiles(xn_out, xn)
    logits = _dot_hi(xn, wr_ref[...]) + br_ref[...]
    eidx = lax.broadcasted_iota(I32, logits.shape, 1).astype(F32)
    lane = lax.broadcasted_iota(I32, (logits.shape[0], LANES), 1)
    idx_pad = jnp.zeros((logits.shape[0], LANES), F32)
    val_pad = jnp.full((logits.shape[0], LANES), MASK_VALUE, F32)
    cur = logits
    sels = []
    for kth in range(TOP_K):
        m = jnp.max(cur, axis=-1, keepdims=True)
        sel = jnp.min(jnp.where(cur == m, eidx, float(N_EXPERTS)), axis=-1, keepdims=True)
        sels.append(sel)
        idx_pad = jnp.where(lane == kth, sel, idx_pad)
        val_pad = jnp.where(lane == kth, m, val_pad)
        cur = jnp.where(eidx == sel, -jnp.inf, cur)
    top = jnp.max(val_pad, axis=-1, keepdims=True)
    e = jnp.exp(val_pad - top)
    gate_out[...] = e / jnp.sum(e, axis=-1, keepdims=True)
    idx_out[...] = idx_pad.astype(I32)

    tm = logits.shape[0]
    lane_f = lane.astype(F32)
    picked = [lane_f == sel for sel in sels]
    onehot = sum(jnp.where(pk, 1.0, 0.0) for pk in picked)
    tr = lax.broadcasted_iota(I32, (tm, tm), 0)
    tc = lax.broadcasted_iota(I32, (tm, tm), 1)
    earlier = jnp.where(tr > tc, 1.0, 0.0).astype(BF16)
    before = jnp.dot(earlier, onehot.astype(BF16), preferred_element_type=F32) + cnt_ref[...]
    rank_pad = jnp.zeros((tm, LANES), F32)
    for kth, pk in enumerate(picked):
        rank_k = jnp.sum(jnp.where(pk, before, 0.0), axis=-1, keepdims=True)
        rank_pad = jnp.where(lane == kth, rank_k, rank_pad)
    rank_out[...] = rank_pad.astype(I32)
    counts = cnt_ref[...] + jnp.sum(onehot, axis=0, keepdims=True)
    cnt_ref[...] = counts
    cnt_out[...] = counts


def _mix_out(attn, rw, xf, w_out, norm_ffn_g, w_router, b_router, w_gate_up, w_down):
    n = xf.shape[0]
    tm = TOKEN_TILE
    n_steps = n // tm
    w_rows = N_EXPERTS * D_MODEL
    assert w_rows % n_steps == 0 and (w_rows // n_steps) % (2 * SUBLANES) == 0
    wr = w_rows // n_steps
    wgu_spec = pl.BlockSpec((wr, 2 * D_FF), lambda i: (i, 0))
    wd_spec = pl.BlockSpec((wr, D_MODEL), lambda i: (i, 0))
    row = lambda x: x.reshape(1, -1)
    half = pl.BlockSpec((tm, RWKV_WIDTH), lambda i: (i, 0))
    wide = pl.BlockSpec((tm, D_MODEL), lambda i: (i, 0))
    pad = pl.BlockSpec((tm, LANES), lambda i: (i, 0))
    full = lambda shape: pl.BlockSpec(shape, lambda i: (0, 0))
    return pl.pallas_call(
        _mix_kernel,
        grid=(n // tm,),
        in_specs=[half, half, wide, full((D_MODEL, D_MODEL)), full((1, D_MODEL)),
                  full((D_MODEL, N_EXPERTS)), full((1, N_EXPERTS)), wgu_spec, wd_spec],
        out_specs=[wide, pl.BlockSpec((tm * ROW_TILE, LANES), lambda i: (i, 0)), pad, pad, pad,
                   full((1, LANES)), wgu_spec, wd_spec],
        out_shape=[jax.ShapeDtypeStruct((n, D_MODEL), F32),
                   jax.ShapeDtypeStruct((n * ROW_TILE, LANES), F32),
                   jax.ShapeDtypeStruct((n, LANES), I32), jax.ShapeDtypeStruct((n, LANES), F32),
                   jax.ShapeDtypeStruct((n, LANES), I32), jax.ShapeDtypeStruct((1, LANES), F32),
                   jax.ShapeDtypeStruct((w_rows, 2 * D_FF), BF16),
                   jax.ShapeDtypeStruct((w_rows, D_MODEL), BF16)],
        scratch_shapes=[pltpu.VMEM((1, LANES), F32)],
        compiler_params=_params(("arbitrary",)),
        name="mix_out_router",
    )(attn, rw, xf, w_out.astype(BF16), row(norm_ffn_g), w_router, row(b_router),
      w_gate_up.reshape(w_rows, 2 * D_FF), w_down.reshape(w_rows, D_MODEL))


def _dispatch_kernel(pad_ref, nvalid_ref, dest_ref, x_ref, out_ref, zbuf, sem, zsem,
                     *, tm, bm, n_blocks):
    tile = lambda row, n_rows: pl.ds(pl.multiple_of(row * ROW_TILE, ROW_TILE), n_rows * ROW_TILE)

    @pl.when(pl.program_id(0) == 0)
    def _():
        zbuf[...] = jnp.zeros_like(zbuf)

        def fill(start):
            cp = pltpu.make_async_copy(zbuf, out_ref.at[tile(start, bm), :], zsem)
            cp.start()
            cp.wait()

        def pad_fill(e, carry):
            fill(pad_ref[e])
            return carry

        def tail_fill(blk, carry):
            fill(blk * bm)
            return carry

        lax.fori_loop(0, N_EXPERTS, pad_fill, 0)
        lax.fori_loop(nvalid_ref[0], n_blocks + 1, tail_fill, 0)

    def issue(j, carry):
        for kth in range(TOP_K):
            pltpu.make_async_copy(
                x_ref.at[tile(j, 1), :],
                out_ref.at[tile(dest_ref[j * TOP_K + kth], 1), :],
                sem).start(priority=kth % 2)
        return carry

    lax.fori_loop(0, tm, issue, 0, unroll=8)
    for kth in range(TOP_K):
        pltpu.make_async_copy(x_ref, out_ref.at[tile(0, tm), :], sem).wait()


def _dispatch(xn_tiles, dest_flat, pad_start, n_valid, n_blocks, bm):
    n = xn_tiles.shape[0] // ROW_TILE
    tm = TOKEN_TILE
    return pl.pallas_call(
        functools.partial(_dispatch_kernel, tm=tm, bm=bm, n_blocks=n_blocks),
        grid_spec=pltpu.PrefetchScalarGridSpec(
            num_scalar_prefetch=2,
            grid=(n // tm,),
            in_specs=[
                pl.BlockSpec((tm * TOP_K,), lambda i, ps, nv: (i,), memory_space=pltpu.SMEM),
                pl.BlockSpec((tm * ROW_TILE, LANES), lambda i, ps, nv: (i, 0)),
            ],
            out_specs=pl.BlockSpec(memory_space=pl.ANY),
            scratch_shapes=[pltpu.VMEM((bm * ROW_TILE, LANES), F32), pltpu.SemaphoreType.DMA,
                            pltpu.SemaphoreType.DMA],
        ),
        out_shape=jax.ShapeDtypeStruct(((n_blocks + 1) * bm * ROW_TILE, LANES), F32),
        compiler_params=_params(("arbitrary",)),
        name="moe_dispatch",
    )(pad_start, n_valid, dest_flat, xn_tiles)


def _expert_kernel(be_ref, nvalid_ref, x_ref, wgu_ref, bgu_ref, wd_ref, bd_ref, o_ref):
    i = pl.program_id(0)
    cw = MXU_DEPTH
    wgu_bf = wgu_ref.at[0]
    wd_bf = wd_ref.at[0]

    @pl.when(i < nvalid_ref[0])
    def _():
        bm = x_ref.shape[0] // ROW_TILE
        x = _load_row_tiles(x_ref, bm).astype(BF16)
        acts = []
        gus = []
        for j in range(D_FF // cw):
            g = jnp.dot(x, wgu_bf[:, j * cw:(j + 1) * cw], preferred_element_type=F32)
            u = jnp.dot(x, wgu_bf[:, D_FF + j * cw:D_FF + (j + 1) * cw],
                        preferred_element_type=F32)
            gus.append((g + bgu_ref[0, :, j * cw:(j + 1) * cw],
                        u + bgu_ref[0, :, D_FF + j * cw:D_FF + (j + 1) * cw]))
        for g, u in gus:
            g = jnp.minimum(g, SWIGLU_LIMIT)
            u = jnp.clip(u, -SWIGLU_LIMIT, SWIGLU_LIMIT)
            acts.append(((u + 1.0) * (g * _sigmoid(SWIGLU_ALPHA * g))).astype(BF16))
        out = jnp.dot(jnp.concatenate(acts, axis=1), wd_bf[...], preferred_element_type=F32)
        _store_row_tiles(o_ref, out + bd_ref[0])

    @pl.when(i >= nvalid_ref[0])
    def _():
        o_ref[...] = jnp.zeros_like(o_ref)


def _experts(xs, blk_expert, n_valid, w_gate_up, b_gate_up, w_down, b_down, n_blocks, bm):
    n_rows = n_blocks * bm
    return pl.pallas_call(
        _expert_kernel,
        grid_spec=pltpu.PrefetchScalarGridSpec(
            num_scalar_prefetch=2,
            grid=(n_blocks,),
            in_specs=[
                pl.BlockSpec((bm * ROW_TILE, LANES), lambda i, be, nv: (i, 0)),
                pl.BlockSpec((1, D_MODEL, 2 * D_FF), lambda i, be, nv: (be[i], 0, 0)),
                pl.BlockSpec((1, 1, 2 * D_FF), lambda i, be, nv: (be[i], 0, 0)),
                pl.BlockSpec((1, D_FF, D_MODEL), lambda i, be, nv: (be[i], 0, 0)),
                pl.BlockSpec((1, 1, D_MODEL), lambda i, be, nv: (be[i], 0, 0)),
            ],
            out_specs=pl.BlockSpec((bm * ROW_TILE, LANES), lambda i, be, nv: (i, 0)),
        ),
        out_shape=jax.ShapeDtypeStruct((n_rows * ROW_TILE, LANES), F32),
        compiler_params=_params(("arbitrary",)),
        name="moe_experts",
    )(blk_expert, n_valid, xs, w_gate_up, b_gate_up.reshape(N_EXPERTS, 1, 2 * D_FF),
      w_down, b_down.reshape(N_EXPERTS, 1, D_MODEL))


def _combine_kernel(dest_ref, dest_next_ref, h_ref, gate_ref, g_ref, ys_ref, o_ref, buf, sem,
                    *, tm, n_tiles):
    i = pl.program_id(0)
    slot = i % 2
    tile = lambda row, n_rows: pl.ds(pl.multiple_of(row * ROW_TILE, ROW_TILE), n_rows * ROW_TILE)

    def issue(dref, s):
        def body(j, carry):
            for kth in range(TOP_K):
                pltpu.make_async_copy(
                    ys_ref.at[tile(dref[j * TOP_K + kth], 1), :],
                    buf.at[s, kth, tile(j, 1), :],
                    sem.at[s]).start(priority=kth % 2)
            return carry

        lax.fori_loop(0, tm, body, 0, unroll=8)

    @pl.when(i == 0)
    def _():
        issue(dest_ref, 0)

    def step(s):
        @pl.when(i + 1 < n_tiles)
        def _():
            issue(dest_next_ref, 1 - s)

        for kth in range(TOP_K):
            pltpu.make_async_copy(ys_ref.at[tile(0, tm), :], buf.at[s, kth], sem.at[s]).wait()
        gates = gate_ref[...]
        hf = h_ref[...]
        for kth in range(TOP_K):
            hf = hf + gates[:, kth:kth + 1] * _load_row_tiles(buf.at[s, kth], tm)
        o_ref[...] = hf * lax.rsqrt(jnp.mean(hf * hf, axis=-1, keepdims=True) + RMS_EPS) * g_ref[...]

    for s in range(2):
        pl.when(slot == s)(functools.partial(step, s))


def _combine(h, gate_pad, ys, dest_flat, norm_final_g):
    n = h.shape[0]
    tm = TOKEN_TILE
    n_tiles = n // tm
    return pl.pallas_call(
        functools.partial(_combine_kernel, tm=tm, n_tiles=n_tiles),
        grid=(n_tiles,),
        in_specs=[
            pl.BlockSpec((tm * TOP_K,), lambda i: (i,), memory_space=pltpu.SMEM),
            pl.BlockSpec((tm * TOP_K,), lambda i: (jnp.minimum(i + 1, n_tiles - 1),),
                         memory_space=pltpu.SMEM),
            pl.BlockSpec((tm, D_MODEL), lambda i: (i, 0)),
            pl.BlockSpec((tm, LANES), lambda i: (i, 0)),
            pl.BlockSpec((1, D_MODEL), lambda i: (0, 0)),
            pl.BlockSpec(memory_space=pl.ANY),
        ],
        out_specs=pl.BlockSpec((tm, D_MODEL), lambda i: (i, 0)),
        out_shape=jax.ShapeDtypeStruct((n, D_MODEL), F32),
        scratch_shapes=[pltpu.VMEM((2, TOP_K, tm * ROW_TILE, LANES), F32),
                        pltpu.SemaphoreType.DMA((2,))],
        compiler_params=_params(("arbitrary",)),
        name="moe_combine_norm",
    )(dest_flat, dest_flat, h, gate_pad, norm_final_g.reshape(1, D_MODEL), ys)


def _routing(top_idx, rank, counts, bm):
    n = top_idx.shape[0]
    padded = (counts + bm - 1) // bm * bm
    pend = jnp.cumsum(padded)
    pstart = pend - padded
    dest = pstart[top_idx] + rank
    n_blocks = (n * TOP_K + N_EXPERTS * (bm - 1) + bm - 1) // bm
    blk_first_row = jnp.arange(n_blocks, dtype=pend.dtype) * bm
    blk_expert = jnp.minimum(
        jnp.sum((pend[None, :] <= blk_first_row[:, None]).astype(I32), axis=1),
        N_EXPERTS - 1).astype(I32)
    n_valid = (pend[-1:] // bm).astype(I32)
    pad_start = (pstart + counts).astype(I32)
    return dest.reshape(n * TOP_K).astype(I32), blk_expert, n_valid, pad_start, n_blocks


def _layer(xf, b, s, p):
    qkv, rest = _in_proj(xf, p["norm_mix_g"], p["w_in"])
    slopes = jnp.exp2(-8.0 / ATTN_HEADS * jnp.arange(1, ATTN_HEADS + 1, dtype=F32))
    attn = _attention(qkv.reshape(b, s, 3 * ATTN_WIDTH), slopes)
    rw = _rwkv_mix(
        rest.reshape(b, s, SHIFT_WIDTH), p["shift_mu"], p["decay_w0"], p["w_decay_up"],
        p["iclr_a0"], p["w_iclr_up"], p["w_gate_lr_up"], p["k_k"], p["k_a"],
        p["r_k"].reshape(-1), p["gn_w"], p["gn_b"])
    flat = lambda t: t.reshape(b * s, -1)
    h, xn, idx_pad, gate_pad, rank_pad, counts, wgu_bf, wd_bf = _mix_out(
        flat(attn), flat(rw), xf, p["w_out"], p["norm_ffn_g"], p["w_router"], p["b_router"],
        p["w_gate_up"], p["w_down"])
    wgu_bf = wgu_bf.reshape(N_EXPERTS, D_MODEL, 2 * D_FF)
    wd_bf = wd_bf.reshape(N_EXPERTS, D_FF, D_MODEL)
    bm = EXPERT_BLOCK
    dest, blk_expert, n_valid, pad_start, n_blocks = _routing(
        idx_pad[:, :TOP_K], rank_pad[:, :TOP_K], counts[0, :N_EXPERTS].astype(I32), bm)
    xs = _dispatch(xn, dest, pad_start, n_valid, n_blocks, bm)
    ys = _experts(xs, blk_expert, n_valid, wgu_bf, p["b_gate_up"], wd_bf, p["b_down"], n_blocks, bm)
    return h, gate_pad, ys, dest


def kernel(x, norm_mix_g, w_in, shift_mu, decay_w0, w_decay_up, iclr_a0, w_iclr_up, w_gate_lr_up,
           k_k, k_a, r_k, gn_w, gn_b, w_out, norm_ffn_g, w_router, b_router, w_gate_up,
           b_gate_up, w_down, b_down, norm_final_g):
    b, s, d = x.shape
    assert d == D_MODEL and w_in.shape[0] == 1, "single-layer block"
    names = ("norm_mix_g", "w_in", "shift_mu", "decay_w0", "w_decay_up", "iclr_a0", "w_iclr_up",
             "w_gate_lr_up", "k_k", "k_a", "r_k", "gn_w", "gn_b", "w_out", "norm_ffn_g",
             "w_router", "b_router", "w_gate_up", "b_gate_up", "w_down", "b_down")
    vals = (norm_mix_g, w_in, shift_mu, decay_w0, w_decay_up, iclr_a0, w_iclr_up, w_gate_lr_up,
            k_k, k_a, r_k, gn_w, gn_b, w_out, norm_ffn_g, w_router, b_router, w_gate_up,
            b_gate_up, w_down, b_down)
    p = {nm: v[0] for nm, v in zip(names, vals)}
    h, gate_pad, ys, dest = _layer(x.reshape(b * s, d), b, s, p)
    out = _combine(h, gate_pad, ys, dest, norm_final_g)
    return out.reshape(b, s, d)
```

```python
import functools
import math

import jax
import jax.numpy as jnp
from jax import lax
from jax.experimental import pallas as pl
from jax.experimental.pallas import tpu as pltpu

F32 = jnp.float32
BF16 = jnp.bfloat16
I32 = jnp.int32

D_MODEL = 1024
HEAD_DIM = 64
ATTN_WIDTH = 512
ATTN_HEADS = ATTN_WIDTH // HEAD_DIM
RWKV_WIDTH = 512
DILATED_PATTERNS = ((128, 1), (512, 4), (2048, 16))
ATTN_BLOCK = 128
DECAY_LORA = 64
ICLR_LORA = 64
GATE_LORA = 128
SHIFT_WIDTH = 3 * RWKV_WIDTH + DECAY_LORA + ICLR_LORA + GATE_LORA
N_EXPERTS = 32
TOP_K = 4
D_FF = D_MODEL
SWIGLU_LIMIT = 7.0
SWIGLU_ALPHA = 1.702
RMS_EPS = 1e-5
GN_EPS = 64e-5

LANES = 128
SUBLANES = 8
MXU_DEPTH = 256
ROW_TILE = D_MODEL // LANES
assert ROW_TILE == SUBLANES, "a D_MODEL-wide f32 row must fill exactly one (8,128) tile"
CHUNK = 64
MASK_VALUE = -1e30
LOG2_E = 1.4426950408889634
ATTN_GROUP = 4
VMEM_LIMIT = 56 * 1024 * 1024

TOKEN_TILE = 512
RWKV_TILE = 512
EXPERT_BLOCK = 1024


def _params(semantics, vmem=VMEM_LIMIT):
    return pltpu.CompilerParams(dimension_semantics=semantics, vmem_limit_bytes=vmem)


def _dot(a, b):
    return jnp.dot(a.astype(BF16), b.astype(BF16), preferred_element_type=F32)


def _dot_nt(a, b):
    return lax.dot_general(a.astype(BF16), b.astype(BF16), (((1,), (1,)), ((), ())),
                           preferred_element_type=F32)


def _dot_tn(a, b):
    return lax.dot_general(a.astype(BF16), b.astype(BF16), (((0,), (0,)), ((), ())),
                           preferred_element_type=F32)


def _split3(x):
    h = x.astype(BF16)
    r1 = x - h.astype(F32)
    m = r1.astype(BF16)
    l = (r1 - m.astype(F32)).astype(BF16)
    return h, m, l


def _dot_exact_lhs(a_bf16, x):
    h, m, l = _split3(x)
    d = lambda y: jnp.dot(a_bf16, y, preferred_element_type=F32)
    return d(h) + d(m) + d(l)


def _dot_hi(a, b):
    ah = a.astype(BF16)
    al = (a - ah.astype(F32)).astype(BF16)
    bh = b.astype(BF16)
    bl = (b - bh.astype(F32)).astype(BF16)
    d = lambda x, y: jnp.dot(x, y, preferred_element_type=F32)
    return d(ah, bh) + d(ah, bl) + d(al, bh)


def _store_row_tiles(ref, x):
    rows = x.shape[0]
    for s in range(ROW_TILE):
        ref[pl.ds(s, rows, stride=ROW_TILE), :] = x[:, s * LANES:(s + 1) * LANES]


def _load_row_tiles(ref, rows):
    return jnp.concatenate(
        [ref[pl.ds(s, rows, stride=ROW_TILE), :] for s in range(ROW_TILE)], axis=1)


def _sigmoid(x):
    return 1.0 / (1.0 + jnp.exp(-x))


def _inproj_kernel(x_ref, g_ref, wq_ref, wr_ref, qkv_ref, rest_ref):
    x = x_ref[...]
    xn = x * lax.rsqrt(jnp.mean(x * x, axis=-1, keepdims=True) + RMS_EPS) * g_ref[...]
    xb = xn.astype(BF16)
    qkv_ref[...] = jnp.dot(xb, wq_ref[...], preferred_element_type=F32)
    rest_ref[...] = jnp.dot(xb, wr_ref[...], preferred_element_type=F32)


def _in_proj(xf, g, w_in):
    n = xf.shape[0]
    tm = TOKEN_TILE
    wq = w_in[:, :3 * ATTN_WIDTH].astype(BF16)
    wr = w_in[:, 3 * ATTN_WIDTH:].astype(BF16)
    return pl.pallas_call(
        _inproj_kernel,
        grid=(n // tm,),
        in_specs=[
            pl.BlockSpec((tm, D_MODEL), lambda i: (i, 0)),
            pl.BlockSpec((1, D_MODEL), lambda i: (0, 0)),
            pl.BlockSpec((D_MODEL, 3 * ATTN_WIDTH), lambda i: (0, 0)),
            pl.BlockSpec((D_MODEL, SHIFT_WIDTH), lambda i: (0, 0)),
        ],
        out_specs=[
            pl.BlockSpec((tm, 3 * ATTN_WIDTH), lambda i: (i, 0)),
            pl.BlockSpec((tm, SHIFT_WIDTH), lambda i: (i, 0)),
        ],
        out_shape=[
            jax.ShapeDtypeStruct((n, 3 * ATTN_WIDTH), F32),
            jax.ShapeDtypeStruct((n, SHIFT_WIDTH), F32),
        ],
        compiler_params=_params(("parallel",)),
        name="in_proj",
    )(xf, g.reshape(1, D_MODEL), wq, wr)


def _attn_kernel(slopes_ref, q_ref, k_ref, v_ref, o_ref, bias_ref, m_acc, d_acc, n_acc, *, seq):
    c = ATTN_BLOCK
    hp = pl.program_id(1)
    lane = lax.broadcasted_iota(I32, (c, LANES), 1)
    lo_half = lane < HEAD_DIM
    qi = lax.broadcasted_iota(I32, (c, 2 * c), 0)
    kj = lax.broadcasted_iota(I32, (c, 2 * c), 1)
    diff = qi + c - kj
    for p, (window, dil) in enumerate(DILATED_PATTERNS):
        steps = window // dil
        valid = (diff >= 0) & (diff <= steps)
        dist = (dil * diff).astype(F32)
        for hh in range(2):
            slope = slopes_ref[hp * 2 + hh] * LOG2_E
            bias = jnp.where(valid, -slope * dist, MASK_VALUE)
            bias_ref[2 * (p * 2 + hh)] = bias
            bias_ref[2 * (p * 2 + hh) + 1] = jnp.where(kj >= c, bias, MASK_VALUE)

    scale = LOG2_E / math.sqrt(HEAD_DIM)
    order = sorted(range(len(DILATED_PATTERNS)), key=lambda p: -DILATED_PATTERNS[p][1])

    def rows(start, dil):
        if dil == 1:
            return pl.ds(pl.multiple_of(start, c), c)
        return pl.ds(start, c, stride=dil)

    def group(i, carry, p, dil, nb):
        blocks = []
        for g in range(ATTN_GROUP):
            j = i * ATTN_GROUP + g
            if nb >= ATTN_GROUP:
                r, n = j // nb, j % nb
                first = (n == 0) if g == 0 else False
            else:
                r, n = j // nb, g % nb
                first = n == 0
            blocks.append((r + dil * c * n, r + dil * c * jnp.maximum(n - 1, 0), first))
        qs, ks, vs = [], [], []
        kc = vc = None
        for g, (cur, prev, first) in enumerate(blocks):
            qs.append(q_ref[0, rows(cur, dil), :] * scale)
            kp, vp = kc, vc
            kc = k_ref[0, rows(cur, dil), :].astype(BF16)
            vc = v_ref[0, rows(cur, dil), :].astype(BF16)
            if first is True:
                ks.append(kc)
                vs.append(vc)
                continue
            if g == 0:
                kp = k_ref[0, rows(prev, dil), :].astype(BF16)
                vp = v_ref[0, rows(prev, dil), :].astype(BF16)
            ks.append(jnp.concatenate([kp, kc], axis=0))
            vs.append(jnp.concatenate([vp, vc], axis=0))
        scores = []
        for (cur, prev, first), qf, kcat in zip(blocks, qs, ks):
            for hh in range(2):
                head_mask = lo_half if hh == 0 else jnp.logical_not(lo_half)
                qh = jnp.where(head_mask, qf, 0.0).astype(BF16)
                s = lax.dot_general(qh, kcat, (((1,), (1,)), ((), ())), preferred_element_type=F32)
                slot = 2 * (p * 2 + hh)
                if first is True:
                    s = s + bias_ref[slot][:, c:]
                elif first is False:
                    s = s + bias_ref[slot]
                else:
                    s = s + bias_ref[slot + first.astype(I32)]
                scores.append(s)
        probs = []
        for s in scores:
            m = jnp.max(s, axis=-1, keepdims=True)
            e = jnp.exp2(s - m)
            probs.append((m, jnp.sum(e, axis=-1, keepdims=True), e.astype(BF16)))
        for bi, (cur, prev, first) in enumerate(blocks):
            (m0, d0, e0), (m1, d1, e1) = probs[2 * bi], probs[2 * bi + 1]
            o0 = jnp.dot(e0, vs[bi], preferred_element_type=F32)
            o1 = jnp.dot(e1, vs[bi], preferred_element_type=F32)
            m_b = jnp.where(lo_half, m0, m1)
            d_b = jnp.where(lo_half, d0, d1)
            n_b = jnp.where(lo_half, o0, o1)
            sl = rows(cur, dil)
            if p == order[0]:
                m_acc[sl, :] = m_b
                d_acc[sl, :] = d_b
                n_acc[sl, :] = n_b
            else:
                m_o = m_acc[sl, :]
                m_n = jnp.maximum(m_o, m_b)
                a_o = jnp.exp2(m_o - m_n)
                a_b = jnp.exp2(m_b - m_n)
                d_n = d_acc[sl, :] * a_o + d_b * a_b
                n_n = n_acc[sl, :] * a_o + n_b * a_b
                if p == order[-1]:
                    o_ref[0, sl, :] = n_n / d_n
                else:
                    d_acc[sl, :] = d_n
                    n_acc[sl, :] = n_n
                    m_acc[sl, :] = m_n
        return carry

    for p in order:
        dil = DILATED_PATTERNS[p][1]
        nb = seq // (c * dil)
        n_groups = (nb * dil) // ATTN_GROUP
        lax.fori_loop(0, n_groups, functools.partial(group, p=p, dil=dil, nb=nb), 0)


def _attention(qkv, slopes):
    b, s, _ = qkv.shape
    for window, dil in DILATED_PATTERNS:
        nb = s // (ATTN_BLOCK * dil)
        assert s % (ATTN_BLOCK * dil) == 0 and window // dil <= ATTN_BLOCK
        assert nb % ATTN_GROUP == 0 or (ATTN_GROUP % nb == 0 and (nb * dil) % ATTN_GROUP == 0)
    n_pairs = ATTN_WIDTH // LANES
    blk = lambda off: pl.BlockSpec((1, s, LANES), lambda bi, hp, sl: (bi, 0, off + hp))
    return pl.pallas_call(
        functools.partial(_attn_kernel, seq=s),
        grid_spec=pltpu.PrefetchScalarGridSpec(
            num_scalar_prefetch=1,
            grid=(b, n_pairs),
            in_specs=[blk(0), blk(n_pairs), blk(2 * n_pairs)],
            out_specs=pl.BlockSpec((1, s, LANES), lambda bi, hp, sl: (bi, 0, hp)),
            scratch_shapes=[
                pltpu.VMEM((4 * len(DILATED_PATTERNS), ATTN_BLOCK, 2 * ATTN_BLOCK), F32),
                pltpu.VMEM((s, LANES), F32),
                pltpu.VMEM((s, LANES), F32),
                pltpu.VMEM((s, LANES), F32),
            ],
        ),
        out_shape=jax.ShapeDtypeStruct((b, s, ATTN_WIDTH), F32),
        compiler_params=_params(("parallel", "parallel")),
        name="dilated_attention",
    )(slopes, qkv, qkv, qkv)


def _seg_sum(x, ones_ref, passes):
    hw = ones_ref.shape[0]
    terms = [x.astype(BF16)]
    if passes == 2:
        terms.append((x - terms[0].astype(F32)).astype(BF16))
    parts = []
    for j in range(x.shape[1] // hw):
        sl = slice(j * hw, (j + 1) * hw)
        parts.append(sum(jnp.dot(t[:, sl], ones_ref[...], preferred_element_type=F32) for t in terms))
    return jnp.concatenate(parts, axis=1)


def _rwkv_kernel(rest_ref, mu_ref, w0_ref, wdu_ref, a0_ref, wau_ref, wg_ref, kk_ref, ka_ref,
                 rk_ref, gnw_ref, gnb_ref, o_ref, st_ref, prev_ref, tri_ref, ones_ref, *, n_chunks):
    l = CHUNK
    pw = LANES
    n_pairs = RWKV_WIDTH // pw
    tc = n_chunks * l
    w = RWKV_WIDTH

    @pl.when(pl.program_id(1) == 0)
    def _():
        st_ref[...] = jnp.zeros_like(st_ref)
        prev_ref[...] = jnp.zeros_like(prev_ref)
        ri = lax.broadcasted_iota(I32, tri_ref.shape, 0)
        ci = lax.broadcasted_iota(I32, tri_ref.shape, 1)
        tri_ref[...] = jnp.where((ri >= ci) & (ri // l == ci // l), 1.0, 0.0).astype(BF16)
        hr = lax.broadcasted_iota(I32, ones_ref.shape, 0) // HEAD_DIM
        hc = lax.broadcasted_iota(I32, ones_ref.shape, 1) // HEAD_DIM
        ones_ref[...] = jnp.where(hr == hc, 1.0, 0.0).astype(BF16)

    z = rest_ref[0]
    zrow = lax.broadcasted_iota(I32, z.shape, 0)
    zprev = jnp.where(zrow == 0, prev_ref[...], pltpu.roll(z, 1, 0))
    prev_ref[...] = z[tc - 1:tc, :]
    xs = z + (zprev - z) * mu_ref[...]
    r = xs[:, :w]
    k_in = xs[:, w:2 * w]
    v = xs[:, 2 * w:3 * w]
    wd = xs[:, 3 * w:3 * w + DECAY_LORA]
    ad = xs[:, 3 * w + DECAY_LORA:3 * w + DECAY_LORA + ICLR_LORA]
    gd = xs[:, 3 * w + DECAY_LORA + ICLR_LORA:]
    zz = w0_ref[...] + _dot_hi(jnp.tanh(wd), wdu_ref[...])
    lw = -math.exp(-0.5) * _sigmoid(zz)
    a = _sigmoid(a0_ref[...] + _dot(ad, wau_ref[...]))
    gate = _dot(_sigmoid(gd), wg_ref[...])
    kk = k_in * kk_ref[...]
    kk = kk * lax.rsqrt(jnp.maximum(_seg_sum(kk * kk, ones_ref, 2), 1e-24))
    k = k_in * (1.0 + (a - 1.0) * ka_ref[...])
    tr = tri_ref.shape[0]
    cum = jnp.concatenate(
        [_dot_exact_lhs(tri_ref[...], lw[j * tr:(j + 1) * tr]) for j in range(tc // tr)], axis=0)
    tot = jnp.concatenate(
        [jnp.broadcast_to(cum[c * l + l - 1:c * l + l, :], (l, w)) for c in range(n_chunks)], axis=0)
    p_in = jnp.exp(cum)
    p_ex = jnp.exp(cum - lw)
    p_inv = jnp.exp(-cum)
    p_end = jnp.exp(tot - cum)
    p_tot = jnp.exp(tot)
    kka = kk * a
    al = -kk * p_ex
    be = kka * p_inv
    kt = k * p_inv
    rt = r * p_in
    bh = kka * p_end
    kh = k * p_end

    row = lax.broadcasted_iota(I32, (l, pw), 0)
    col = lax.broadcasted_iota(I32, (l, pw), 1) % l
    strict = row > col
    incl = row >= col
    eye_ss = jnp.where(row == col, 1.0, 0.0).astype(F32)
    row_bd = lax.broadcasted_iota(I32, (2 * l, pw), 0)
    lane_bd = lax.broadcasted_iota(I32, (2 * l, pw), 1)
    diag_mask = (row_bd // l) == (lane_bd // l)
    eye_bd = row_bd == lane_bd

    def bd(x):
        return jnp.where(diag_mask, jnp.concatenate([x, x], axis=0), 0.0)

    items = [(c, p) for c in range(n_chunks) for p in range(n_pairs)]
    cut = lambda t, c, p: t[c * l:(c + 1) * l, p * pw:(p + 1) * pw]

    zeros_bd = jnp.zeros((2 * l, pw), F32)
    zeros_ss = jnp.zeros((l, pw), F32)
    a_ab, a_rb, a_ak, a_rk, akv = {}, {}, {}, {}, {}
    for it in items:
        lhs = jnp.concatenate([cut(al, *it), cut(rt, *it)], axis=0)
        g = _dot_nt(lhs, jnp.concatenate([bd(cut(be, *it)), bd(cut(kt, *it))], axis=0))
        a_ab[it] = jnp.where(strict, g[:l, :pw], 0.0)
        a_ak[it] = jnp.where(strict, g[:l, pw:], 0.0)
        a_rb[it] = jnp.where(incl, g[l:, :pw], 0.0)
        a_rk[it] = jnp.where(incl, g[l:, pw:], 0.0)
    for it in items:
        akv[it] = _dot(a_ak[it], bd(cut(v, *it)))
    xs = {it: _dot(a_ab[it], bd(a_ab[it])) for it in items}
    ts = {it: eye_ss + a_ab[it] for it in items}
    for _ in range(int(math.log2(l)) - 2):
        for it in items:
            both = _dot(jnp.concatenate([xs[it], ts[it]], axis=0), bd(xs[it]))
            xs[it] = both[:l]
            ts[it] = ts[it] + both[l:]
    for it in items:
        ts[it] = ts[it] + _dot(ts[it], bd(xs[it]))
    w_t, u0 = {}, {}
    for it in items:
        wu = _dot(ts[it], jnp.concatenate([bd(cut(al, *it)), bd(akv[it])], axis=1))
        w_t[it] = wu[:, :pw]
        u0[it] = wu[:, pw:]
    r_hat, y0, m_bd, c_bd = {}, {}, {}, {}
    for it in items:
        v_i = cut(v, *it)
        ry = _dot(jnp.concatenate([a_rb[it], a_rk[it]], axis=1),
                  jnp.concatenate([jnp.concatenate([bd(w_t[it]), bd(u0[it])], axis=1),
                                   jnp.concatenate([zeros_bd, bd(v_i)], axis=1)], axis=0))
        r_hat[it] = cut(rt, *it) + ry[:, :pw]
        y0[it] = ry[:, pw:]
        mc = _dot_tn(jnp.concatenate([cut(bh, *it), cut(kh, *it)], axis=0),
                     jnp.concatenate([jnp.concatenate([w_t[it], u0[it]], axis=1),
                                      jnp.concatenate([zeros_ss, v_i], axis=1)], axis=0))
        decay = jnp.broadcast_to(cut(p_tot, *it)[:1], (2 * l, pw))
        m_bd[it] = jnp.where(diag_mask, mc[:, :pw], 0.0) + jnp.where(eye_bd, decay, 0.0)
        c_bd[it] = jnp.where(diag_mask, mc[:, pw:], 0.0)
    st = [st_ref[p] for p in range(n_pairs)]
    for c in range(n_chunks):
        for p in range(n_pairs):
            it = (c, p)
            both = _dot(jnp.concatenate([r_hat[it], m_bd[it]], axis=0), st[p])
            o_ref[0, c * l:(c + 1) * l, p * pw:(p + 1) * pw] = both[:l] + y0[it]
            st[p] = both[l:] + c_bd[it]
    for p in range(n_pairs):
        st_ref[p] = st[p]

    y = o_ref[0]
    inv_n = 1.0 / HEAD_DIM
    mean = _seg_sum(y, ones_ref, 1) * inv_n
    yc = y - mean
    var = _seg_sum(yc * yc, ones_ref, 1) * inv_n
    yn = yc * lax.rsqrt(var + GN_EPS) * gnw_ref[...] + gnb_ref[...]
    bonus = _seg_sum(r * k * rk_ref[...], ones_ref, 1) * v
    o_ref[0] = (yn + bonus) * gate


def _rwkv_mix(rest, shift_mu, decay_w0, w_decay_up, iclr_a0, w_iclr_up, w_gate_lr_up, k_k, k_a,
              r_k, gn_w, gn_b):
    b, s, _ = rest.shape
    w = RWKV_WIDTH
    tc = RWKV_TILE
    assert s % tc == 0 and tc % MXU_DEPTH == 0 and MXU_DEPTH % CHUNK == 0
    row = lambda x: x.reshape(1, -1)
    full = lambda shape: pl.BlockSpec(shape, lambda bi, i: (0, 0))
    return pl.pallas_call(
        functools.partial(_rwkv_kernel, n_chunks=tc // CHUNK),
        grid=(b, s // tc),
        in_specs=[
            pl.BlockSpec((1, tc, SHIFT_WIDTH), lambda bi, i: (bi, i, 0)),
            full((1, SHIFT_WIDTH)), full((1, w)), full((DECAY_LORA, w)), full((1, w)),
            full((ICLR_LORA, w)), full((GATE_LORA, w)), full((1, w)), full((1, w)),
            full((1, w)), full((1, w)), full((1, w)),
        ],
        out_specs=pl.BlockSpec((1, tc, w), lambda bi, i: (bi, i, 0)),
        out_shape=jax.ShapeDtypeStruct((b, s, w), F32),
        scratch_shapes=[
            pltpu.VMEM((w // LANES, LANES, LANES), F32),
            pltpu.VMEM((1, SHIFT_WIDTH), F32),
            pltpu.VMEM((MXU_DEPTH, MXU_DEPTH), BF16),
            pltpu.VMEM((MXU_DEPTH, MXU_DEPTH), BF16),
        ],
        compiler_params=_params(("parallel", "arbitrary")),
        name="rwkv7_mix",
    )(rest, row(shift_mu), row(decay_w0), w_decay_up, row(iclr_a0), w_iclr_up, w_gate_lr_up,
      row(k_k), row(k_a), row(r_k), row(gn_w), row(gn_b))


def _mix_kernel(attn_ref, rw_ref, x_ref, wo_ref, nf_ref, wr_ref, br_ref, wgu_ref, wd_ref,
                h_out, xn_out, idx_out, gate_out, rank_out, cnt_out, wgu_bf_ref, wd_bf_ref, cnt_ref):
    wgu_bf_ref[...] = wgu_ref[...].astype(BF16)
    wd_bf_ref[...] = wd_ref[...].astype(BF16)

    @pl.when(pl.program_id(0) == 0)
    def _():
        cnt_ref[...] = jnp.zeros_like(cnt_ref)

    mixed = (jnp.dot(attn_ref[...].astype(BF16), wo_ref[:ATTN_WIDTH, :], preferred_element_type=F32)
             + jnp.dot(rw_ref[...].astype(BF16), wo_ref[ATTN_WIDTH:, :], preferred_element_type=F32))
    h = x_ref[...] + mixed
    h_out[...] = h
    xn = h * lax.rsqrt(jnp.mean(h * h, axis=-1, keepdims=True) + RMS_EPS) * nf_ref[...]
    _store_row_tiles(xn_out, xn)
    logits = _dot_hi(xn, wr_ref[...]) + br_ref[...]
    eidx = lax.broadcasted_iota(I32, logits.shape, 1).astype(F32)
    lane = lax.broadcasted_iota(I32, (logits.shape[0], LANES), 1)
    idx_pad = jnp.zeros((logits.shape[0], LANES), F32)
    val_pad = jnp.full((logits.shape[0], LANES), MASK_VALUE, F32)
    cur = logits
    sels = []
    for kth in range(TOP_K):
        m = jnp.max(cur, axis=-1, keepdims=True)
        sel = jnp.min(jnp.where(cur == m, eidx, float(N_EXPERTS)), axis=-1, keepdims=True)
        sels.append(sel)
        idx_pad = jnp.where(lane == kth, sel, idx_pad)
        val_pad = jnp.where(lane == kth, m, val_pad)
        cur = jnp.where(eidx == sel, -jnp.inf, cur)
    top = jnp.max(val_pad, axis=-1, keepdims=True)
    e = jnp.exp(val_pad - top)
    gate_out[...] = e / jnp.sum(e, axis=-1, keepdims=True)
    idx_out[...] = idx_pad.astype(I32)

    tm = logits.shape[0]
    lane_f = lane.astype(F32)
    picked = [lane_f == sel for sel in sels]
    onehot = sum(jnp.where(pk, 1.0, 0.0) for pk in picked)
    tr = lax.broadcasted_iota(I32, (tm, tm), 0)
    tc = lax.broadcasted_iota(I32, (tm, tm), 1)
    earlier = jnp.where(tr > tc, 1.0, 0.0).astype(BF16)
    before = jnp.dot(earlier, onehot.astype(BF16), preferred_element_type=F32) + cnt_ref[...]
    rank_pad = jnp.zeros((tm, LANES), F32)
    for kth, pk in enumerate(picked):
        rank_k = jnp.sum(jnp.where(pk, before, 0.0), axis=-1, keepdims=True)
        rank_pad = jnp.where(lane == kth, rank_k, rank_pad)
    rank_out[...] = rank_pad.astype(I32)
    counts = cnt_ref[...] + jnp.sum(onehot, axis=0, keepdims=True)
    cnt_ref[...] = counts
    cnt_out[...] = counts


def _mix_out(attn, rw, xf, w_out, norm_ffn_g, w_router, b_router, w_gate_up, w_down):
    n = xf.shape[0]
    tm = TOKEN_TILE
    n_steps = n // tm
    w_rows = N_EXPERTS * D_MODEL
    assert w_rows % n_steps == 0 and (w_rows // n_steps) % (2 * SUBLANES) == 0
    wr = w_rows // n_steps
    wgu_spec = pl.BlockSpec((wr, 2 * D_FF), lambda i: (i, 0))
    wd_spec = pl.BlockSpec((wr, D_MODEL), lambda i: (i, 0))
    row = lambda x: x.reshape(1, -1)
    half = pl.BlockSpec((tm, RWKV_WIDTH), lambda i: (i, 0))
    wide = pl.BlockSpec((tm, D_MODEL), lambda i: (i, 0))
    pad = pl.BlockSpec((tm, LANES), lambda i: (i, 0))
    full = lambda shape: pl.BlockSpec(shape, lambda i: (0, 0))
    return pl.pallas_call(
        _mix_kernel,
        grid=(n // tm,),
        in_specs=[half, half, wide, full((D_MODEL, D_MODEL)), full((1, D_MODEL)),
                  full((D_MODEL, N_EXPERTS)), full((1, N_EXPERTS)), wgu_spec, wd_spec],
        out_specs=[wide, pl.BlockSpec((tm * ROW_TILE, LANES), lambda i: (i, 0)), pad, pad, pad,
                   full((1, LANES)), wgu_spec, wd_spec],
        out_shape=[jax.ShapeDtypeStruct((n, D_MODEL), F32),
                   jax.ShapeDtypeStruct((n * ROW_TILE, LANES), F32),
                   jax.ShapeDtypeStruct((n, LANES), I32), jax.ShapeDtypeStruct((n, LANES), F32),
                   jax.ShapeDtypeStruct((n, LANES), I32), jax.ShapeDtypeStruct((1, LANES), F32),
                   jax.ShapeDtypeStruct((w_rows, 2 * D_FF), BF16),
                   jax.ShapeDtypeStruct((w_rows, D_MODEL), BF16)],
        scratch_shapes=[pltpu.VMEM((1, LANES), F32)],
        compiler_params=_params(("arbitrary",)),
        name="mix_out_router",
    )(attn, rw, xf, w_out.astype(BF16), row(norm_ffn_g), w_router, row(b_router),
      w_gate_up.reshape(w_rows, 2 * D_FF), w_down.reshape(w_rows, D_MODEL))


def _dispatch_kernel(pad_ref, nvalid_ref, dest_ref, x_ref, out_ref, zbuf, sem, zsem,
                     *, tm, bm, n_blocks):
    tile = lambda row, n_rows: pl.ds(pl.multiple_of(row * ROW_TILE, ROW_TILE), n_rows * ROW_TILE)

    @pl.when(pl.program_id(0) == 0)
    def _():
        zbuf[...] = jnp.zeros_like(zbuf)

        def fill(start):
            cp = pltpu.make_async_copy(zbuf, out_ref.at[tile(start, bm), :], zsem)
            cp.start()
            cp.wait()

        def pad_fill(e, carry):
            fill(pad_ref[e])
            return carry

        def tail_fill(blk, carry):
            fill(blk * bm)
            return carry

        lax.fori_loop(0, N_EXPERTS, pad_fill, 0)
        lax.fori_loop(nvalid_ref[0], n_blocks + 1, tail_fill, 0)

    def issue(j, carry):
        for kth in range(TOP_K):
            pltpu.make_async_copy(
                x_ref.at[tile(j, 1), :],
                out_ref.at[tile(dest_ref[j * TOP_K + kth], 1), :],
                sem).start(priority=kth % 2)
        return carry

    lax.fori_loop(0, tm, issue, 0, unroll=8)
    for kth in range(TOP_K):
        pltpu.make_async_copy(x_ref, out_ref.at[tile(0, tm), :], sem).wait()


def _dispatch(xn_tiles, dest_flat, pad_start, n_valid, n_blocks, bm):
    n = xn_tiles.shape[0] // ROW_TILE
    tm = TOKEN_TILE
    return pl.pallas_call(
        functools.partial(_dispatch_kernel, tm=tm, bm=bm, n_blocks=n_blocks),
        grid_spec=pltpu.PrefetchScalarGridSpec(
            num_scalar_prefetch=2,
            grid=(n // tm,),
            in_specs=[
                pl.BlockSpec((tm * TOP_K,), lambda i, ps, nv: (i,), memory_space=pltpu.SMEM),
                pl.BlockSpec((tm * ROW_TILE, LANES), lambda i, ps, nv: (i, 0)),
            ],
            out_specs=pl.BlockSpec(memory_space=pl.ANY),
            scratch_shapes=[pltpu.VMEM((bm * ROW_TILE, LANES), F32), pltpu.SemaphoreType.DMA,
                            pltpu.SemaphoreType.DMA],
        ),
        out_shape=jax.ShapeDtypeStruct(((n_blocks + 1) * bm * ROW_TILE, LANES), F32),
        compiler_params=_params(("arbitrary",)),
        name="moe_dispatch",
    )(pad_start, n_valid, dest_flat, xn_tiles)


def _expert_kernel(be_ref, nvalid_ref, x_ref, wgu_ref, bgu_ref, wd_ref, bd_ref, o_ref):
    i = pl.program_id(0)
    cw = MXU_DEPTH
    wgu_bf = wgu_ref.at[0]
    wd_bf = wd_ref.at[0]

    @pl.when(i < nvalid_ref[0])
    def _():
        bm = x_ref.shape[0] // ROW_TILE
        x = _load_row_tiles(x_ref, bm).astype(BF16)
        acts = []
        gus = []
        for j in range(D_FF // cw):
            g = jnp.dot(x, wgu_bf[:, j * cw:(j + 1) * cw], preferred_element_type=F32)
            u = jnp.dot(x, wgu_bf[:, D_FF + j * cw:D_FF + (j + 1) * cw],
                        preferred_element_type=F32)
            gus.append((g + bgu_ref[0, :, j * cw:(j + 1) * cw],
                        u + bgu_ref[0, :, D_FF + j * cw:D_FF + (j + 1) * cw]))
        for g, u in gus:
            g = jnp.minimum(g, SWIGLU_LIMIT)
            u = jnp.clip(u, -SWIGLU_LIMIT, SWIGLU_LIMIT)
            acts.append(((u + 1.0) * (g * _sigmoid(SWIGLU_ALPHA * g))).astype(BF16))
        out = jnp.dot(jnp.concatenate(acts, axis=1), wd_bf[...], preferred_element_type=F32)
        _store_row_tiles(o_ref, out + bd_ref[0])

    @pl.when(i >= nvalid_ref[0])
    def _():
        o_ref[...] = jnp.zeros_like(o_ref)


def _experts(xs, blk_expert, n_valid, w_gate_up, b_gate_up, w_down, b_down, n_blocks, bm):
    n_rows = n_blocks * bm
    return pl.pallas_call(
        _expert_kernel,
        grid_spec=pltpu.PrefetchScalarGridSpec(
            num_scalar_prefetch=2,
            grid=(n_blocks,),
            in_specs=[
                pl.BlockSpec((bm * ROW_TILE, LANES), lambda i, be, nv: (i, 0)),
                pl.BlockSpec((1, D_MODEL, 2 * D_FF), lambda i, be, nv: (be[i], 0, 0)),
                pl.BlockSpec((1, 1, 2 * D_FF), lambda i, be, nv: (be[i], 0, 0)),
                pl.BlockSpec((1, D_FF, D_MODEL), lambda i, be, nv: (be[i], 0, 0)),
                pl.BlockSpec((1, 1, D_MODEL), lambda i, be, nv: (be[i], 0, 0)),
            ],
            out_specs=pl.BlockSpec((bm * ROW_TILE, LANES), lambda i, be, nv: (i, 0)),
        ),
        out_shape=jax.ShapeDtypeStruct((n_rows * ROW_TILE, LANES), F32),
        compiler_params=_params(("arbitrary",)),
        name="moe_experts",
    )(blk_expert, n_valid, xs, w_gate_up, b_gate_up.reshape(N_EXPERTS, 1, 2 * D_FF),
      w_down, b_down.reshape(N_EXPERTS, 1, D_MODEL))


def _combine_kernel(dest_ref, dest_next_ref, h_ref, gate_ref, g_ref, ys_ref, o_ref, buf, sem,
                    *, tm, n_tiles):
    i = pl.program_id(0)
    slot = i % 2
    tile = lambda row, n_rows: pl.ds(pl.multiple_of(row * ROW_TILE, ROW_TILE), n_rows * ROW_TILE)

    def issue(dref, s):
        def body(j, carry):
            for kth in range(TOP_K):
                pltpu.make_async_copy(
                    ys_ref.at[tile(dref[j * TOP_K + kth], 1), :],
                    buf.at[s, kth, tile(j, 1), :],
                    sem.at[s]).start(priority=kth % 2)
            return carry

        lax.fori_loop(0, tm, body, 0, unroll=8)

    @pl.when(i == 0)
    def _():
        issue(dest_ref, 0)

    def step(s):
        @pl.when(i + 1 < n_tiles)
        def _():
            issue(dest_next_ref, 1 - s)

        for kth in range(TOP_K):
            pltpu.make_async_copy(ys_ref.at[tile(0, tm), :], buf.at[s, kth], sem.at[s]).wait()
        gates = gate_ref[...]
        hf = h_ref[...]
        for kth in range(TOP_K):
            hf = hf + gates[:, kth:kth + 1] * _load_row_tiles(buf.at[s, kth], tm)
        o_ref[...] = hf * lax.rsqrt(jnp.mean(hf * hf, axis=-1, keepdims=True) + RMS_EPS) * g_ref[...]

    for s in range(2):
        pl.when(slot == s)(functools.partial(step, s))


def _combine(h, gate_pad, ys, dest_flat, norm_final_g):
    n = h.shape[0]
    tm = TOKEN_TILE
    n_tiles = n // tm
    return pl.pallas_call(
        functools.partial(_combine_kernel, tm=tm, n_tiles=n_tiles),
        grid=(n_tiles,),
        in_specs=[
            pl.BlockSpec((tm * TOP_K,), lambda i: (i,), memory_space=pltpu.SMEM),
            pl.BlockSpec((tm * TOP_K,), lambda i: (jnp.minimum(i + 1, n_tiles - 1),),
                         memory_space=pltpu.SMEM),
            pl.BlockSpec((tm, D_MODEL), lambda i: (i, 0)),
            pl.BlockSpec((tm, LANES), lambda i: (i, 0)),
            pl.BlockSpec((1, D_MODEL), lambda i: (0, 0)),
            pl.BlockSpec(memory_space=pl.ANY),
        ],
        out_specs=pl.BlockSpec((tm, D_MODEL), lambda i: (i, 0)),
        out_shape=jax.ShapeDtypeStruct((n, D_MODEL), F32),
        scratch_shapes=[pltpu.VMEM((2, TOP_K, tm * ROW_TILE, LANES), F32),
                        pltpu.SemaphoreType.DMA((2,))],
        compiler_params=_params(("arbitrary",)),
        name="moe_combine_norm",
    )(dest_flat, dest_flat, h, gate_pad, norm_final_g.reshape(1, D_MODEL), ys)


def _routing(top_idx, rank, counts, bm):
    n = top_idx.shape[0]
    padded = (counts + bm - 1) // bm * bm
    pend = jnp.cumsum(padded)
    pstart = pend - padded
    dest = pstart[top_idx] + rank
    n_blocks = (n * TOP_K + N_EXPERTS * (bm - 1) + bm - 1) // bm
    blk_first_row = jnp.arange(n_blocks, dtype=pend.dtype) * bm
    blk_expert = jnp.minimum(
        jnp.sum((pend[None, :] <= blk_first_row[:, None]).astype(I32), axis=1),
        N_EXPERTS - 1).astype(I32)
    n_valid = (pend[-1:] // bm).astype(I32)
    pad_start = (pstart + counts).astype(I32)
    return dest.reshape(n * TOP_K).astype(I32), blk_expert, n_valid, pad_start, n_blocks


def _layer(xf, b, s, p):
    qkv, rest = _in_proj(xf, p["norm_mix_g"], p["w_in"])
    slopes = jnp.exp2(-8.0 / ATTN_HEADS * jnp.arange(1, ATTN_HEADS + 1, dtype=F32))
    attn = _attention(qkv.reshape(b, s, 3 * ATTN_WIDTH), slopes)
    rw = _rwkv_mix(
        rest.reshape(b, s, SHIFT_WIDTH), p["shift_mu"], p["decay_w0"], p["w_decay_up"],
        p["iclr_a0"], p["w_iclr_up"], p["w_gate_lr_up"], p["k_k"], p["k_a"],
        p["r_k"].reshape(-1), p["gn_w"], p["gn_b"])
    flat = lambda t: t.reshape(b * s, -1)
    h, xn, idx_pad, gate_pad, rank_pad, counts, wgu_bf, wd_bf = _mix_out(
        flat(attn), flat(rw), xf, p["w_out"], p["norm_ffn_g"], p["w_router"], p["b_router"],
        p["w_gate_up"], p["w_down"])
    wgu_bf = wgu_bf.reshape(N_EXPERTS, D_MODEL, 2 * D_FF)
    wd_bf = wd_bf.reshape(N_EXPERTS, D_FF, D_MODEL)
    bm = EXPERT_BLOCK
    dest, blk_expert, n_valid, pad_start, n_blocks = _routing(
        idx_pad[:, :TOP_K], rank_pad[:, :TOP_K], counts[0, :N_EXPERTS].astype(I32), bm)
    xs = _dispatch(xn, dest, pad_start, n_valid, n_blocks, bm)
    ys = _experts(xs, blk_expert, n_valid, wgu_bf, p["b_gate_up"], wd_bf, p["b_down"], n_blocks, bm)
    return h, gate_pad, ys, dest


def kernel(x, norm_mix_g, w_in, shift_mu, decay_w0, w_decay_up, iclr_a0, w_iclr_up, w_gate_lr_up,
           k_k, k_a, r_k, gn_w, gn_b, w_out, norm_ffn_g, w_router, b_router, w_gate_up,
           b_gate_up, w_down, b_down, norm_final_g):
    b, s, d = x.shape
    assert d == D_MODEL and w_in.shape[0] == 1, "single-layer block"
    names = ("norm_mix_g", "w_in", "shift_mu", "decay_w0", "w_decay_up", "iclr_a0", "w_iclr_up",
             "w_gate_lr_up", "k_k", "k_a", "r_k", "gn_w", "gn_b", "w_out", "norm_ffn_g",
             "w_router", "b_router", "w_gate_up", "b_gate_up", "w_down", "b_down")
    vals = (norm_mix_g, w_in, shift_mu, decay_w0, w_decay_up, iclr_a0, w_iclr_up, w_gate_lr_up,
            k_k, k_a, r_k, gn_w, gn_b, w_out, norm_ffn_g, w_router, b_router, w_gate_up,
            b_gate_up, w_down, b_down)
    p = {nm: v[0] for nm, v in zip(names, vals)}
    h, gate_pad, ys, dest = _layer(x.reshape(b * s, d), b, s, p)
    out = _combine(h, gate_pad, ys, dest, norm_final_g)
    return out.reshape(b, s, d)
```
